```python
import math
import jax
import jax.numpy as jnp
from jax import lax
import numpy as np

D_MODEL = 2048
BATCH = 2
SEQ = 4096
DEPTH = 2

CTX_LEN = 256
GRID_W = 64
HEAD_DIM = 128
CHUNK = 64
GLA_HEADS = 4
GLA_DK = 64
GLA_DV = 128
GLA_RANK = 16
GLA_GATE_NORM = 16.0
GDN_HEADS = 4
GDN_DK = 128
GDN_DV = 128
GDN_CONV = 5
GDN_CONV_CH = 2 * GDN_HEADS * GDN_DK + GDN_HEADS * GDN_DV
ATTN_HEADS = 8
ATTN_KV_HEADS = 2
ATTN_GROUP = ATTN_HEADS // ATTN_KV_HEADS
ROPE_THETA = 10000.0
Q_BLOCK = 128
N_EXPERTS = 16
EC_FACTOR = 2
EXPERT_FF = 2048
N_MOD = 6
DEEPNORM_ALPHA = (2 * DEPTH) ** 0.25
DEEPNORM_BETA = (8 * DEPTH) ** -0.25
MIX_WIDTH = GLA_HEADS * GLA_DV + GDN_HEADS * GDN_DV + ATTN_HEADS * HEAD_DIM
IN_SPLITS = (
    GLA_HEADS * GLA_DK, GLA_HEADS * GLA_DK, GLA_HEADS * GLA_DV, GLA_HEADS * GLA_DV, 2 * GLA_RANK,
    GDN_HEADS * GDN_DK, GDN_HEADS * GDN_DK, GDN_HEADS * GDN_DV, GDN_HEADS * GDN_DV, 2 * GDN_HEADS, 2 * GDN_HEADS,
    ATTN_HEADS * HEAD_DIM, ATTN_KV_HEADS * HEAD_DIM, ATTN_KV_HEADS * HEAD_DIM,
)
IN_WIDTH = sum(IN_SPLITS)

kernel_name = 'hybrid_gla_gdn_gqa_ec_moe_diffusion_block'


def _layer_norm(x, gain, bias, eps=1e-5):
    xf = x.astype(jnp.float32)
    mu = jnp.mean(xf, axis=-1, keepdims=True)
    var = jnp.mean(jnp.square(xf - mu), axis=-1, keepdims=True)
    return ((xf - mu) * lax.rsqrt(var + eps) * gain + bias).astype(x.dtype)


def _rms_norm(x, gain, eps=1e-6):
    xf = x.astype(jnp.float32)
    return xf * lax.rsqrt(jnp.mean(xf * xf, axis=-1, keepdims=True) + eps) * gain


def _l2norm(x, eps=1e-6):
    return x * lax.rsqrt(jnp.sum(x * x, axis=-1, keepdims=True) + eps)


def _modulate(x, shift, scale):
    return x * (1.0 + scale) + shift


def _split_cols(p):
    points = np.cumsum(np.array(IN_SPLITS))[:-1].tolist()
    return jnp.split(p, points, axis=-1)


def _heads(a, n_heads):
    b, t, _ = a.shape
    return a.reshape(b, t, n_heads, -1).transpose(0, 2, 1, 3)


def _depthwise_conv(x, w):
    k = w.shape[0]
    return lax.conv_general_dilated(
        x, w.astype(x.dtype)[:, None, :], window_strides=(1,), padding=[(k // 2, k // 2)],
        dimension_numbers=('NWC', 'WIO', 'NWC'), feature_group_count=x.shape[-1])


def _axial_rope_tables(n_tokens):
    rows = n_tokens // GRID_W
    row = jnp.broadcast_to(jnp.arange(rows)[:, None], (rows, GRID_W)).reshape(-1).astype(jnp.float32)
    col = jnp.broadcast_to(jnp.arange(GRID_W)[None, :], (rows, GRID_W)).reshape(-1).astype(jnp.float32)
    half = HEAD_DIM // 2
    inv = ROPE_THETA ** (-jnp.arange(0, half, 2, dtype=jnp.float32) / half)
    ang = jnp.concatenate([row[:, None] * inv, col[:, None] * inv], axis=-1)
    return jnp.cos(ang), jnp.sin(ang)


def _apply_axial_rope(x, cos, sin):
    n = x.shape[1]
    quarter = HEAD_DIM // 4
    bshape = (1, n) + (1,) * (x.ndim - 3) + (2, quarter)
    c = cos.reshape(bshape)
    s = sin.reshape(bshape)
    xr = x.reshape(x.shape[:-1] + (2, 2, quarter))
    x1 = xr[..., 0, :]
    x2 = xr[..., 1, :]
    return jnp.stack([x1 * c - x2 * s, x2 * c + x1 * s], axis=-2).reshape(x.shape)


def _gla_chunked(q, k, v, log_a, s0, with_out):
    b_, h_, t_, dk = q.shape
    dv = v.shape[-1]
    n = t_ // CHUNK
    q = q.reshape(b_, h_, n, CHUNK, dk)
    k = k.reshape(b_, h_, n, CHUNK, dk)
    v = v.reshape(b_, h_, n, CHUNK, dv)
    cum = jnp.cumsum(log_a.reshape(b_, h_, n, CHUNK, dk), axis=3)
    cum_last = cum[:, :, :, -1]
    kv = jnp.einsum('bhnck,bhncv->bhnkv', k * jnp.exp(cum_last[:, :, :, None] - cum), v)
    d_last = jnp.exp(cum_last)
    seq = lambda a: jnp.moveaxis(a, 2, 0)
    if with_out:
        q_dec = q * jnp.exp(cum)
        lower = jnp.tril(jnp.ones((CHUNK, CHUNK), dtype=bool))
        a_intra = jnp.where(lower, jnp.einsum('bhnik,bhnjk->bhnij', q_dec, k * jnp.exp(-cum)), 0.0)
        o_intra = jnp.einsum('bhnij,bhnjv->bhniv', a_intra, v)
        xs = (seq(kv), seq(d_last), seq(q_dec))
    else:
        xs = (seq(kv), seq(d_last))

    def step(state, inp):
        nxt = inp[1][..., None] * state + inp[0]
        if not with_out:
            return nxt, None
        return nxt, jnp.einsum('bhck,bhkv->bhcv', inp[2], state)

    state, o_inter = lax.scan(step, s0, xs)
    if not with_out:
        return None, state
    o = o_intra + jnp.moveaxis(o_inter, 0, 2)
    return o.reshape(b_, h_, t_, dv), state


def _gdn_chunked(q, k, v, g, beta, s0, with_out):
    b_, h_, t_, dk = q.shape
    dv = v.shape[-1]
    n = t_ // CHUNK
    q = q.reshape(b_, h_, n, CHUNK, dk)
    k = k.reshape(b_, h_, n, CHUNK, dk)
    v = v.reshape(b_, h_, n, CHUNK, dv)
    g = jnp.cumsum(g.reshape(b_, h_, n, CHUNK), axis=-1)
    beta = beta.reshape(b_, h_, n, CHUNK)
    k_beta = k * beta[..., None]
    v_beta = v * beta[..., None]
    lower = jnp.tril(jnp.ones((CHUNK, CHUNK), dtype=bool))
    strict = jnp.tril(jnp.ones((CHUNK, CHUNK), dtype=bool), -1)
    gamma = jnp.where(lower, jnp.exp(jnp.where(lower, g[..., :, None] - g[..., None, :], 0.0)), 0.0)
    l_mat = jnp.where(strict, jnp.einsum('bhnid,bhnjd->bhnij', k_beta, k) * gamma, 0.0)
    eye = jnp.eye(CHUNK, dtype=l_mat.dtype)
    t_mat = lax.linalg.triangular_solve(l_mat + eye, jnp.broadcast_to(eye, l_mat.shape),
                                        left_side=True, lower=True, unit_diagonal=True)
    u = jnp.einsum('bhnij,bhnjv->bhniv', t_mat, v_beta)
    w = jnp.einsum('bhnij,bhnjk->bhnik', t_mat, k_beta * jnp.exp(g)[..., None])
    g_last = g[..., -1]
    k_dec = k * jnp.exp(g_last[..., None] - g)[..., None]
    d_last = jnp.exp(g_last)
    seq = lambda a: jnp.moveaxis(a, 2, 0)
    if with_out:
        q_dec = q * jnp.exp(g)[..., None]
        a_intra = jnp.where(lower, jnp.einsum('bhnik,bhnjk->bhnij', q, k) * gamma, 0.0)
        xs = tuple(seq(a) for a in (u, w, k_dec, d_last, q_dec, a_intra))
    else:
        xs = tuple(seq(a) for a in (u, w, k_dec, d_last))

    def step(state, inp):
        u_c, w_c, k_c, d_c = inp[:4]
        v_new = u_c - jnp.einsum('bhck,bhkv->bhcv', w_c, state)
        nxt = d_c[..., None, None] * state + jnp.einsum('bhck,bhcv->bhkv', k_c, v_new)
        if not with_out:
            return nxt, None
        q_c, a_c = inp[4:]
        o = jnp.einsum('bhck,bhkv->bhcv', q_c, state) + jnp.einsum('bhij,bhjv->bhiv', a_c, v_new)
        return nxt, o

    state, o = lax.scan(step, s0, xs)
    if not with_out:
        return None, state
    return jnp.moveaxis(o, 0, 2).reshape(b_, h_, t_, dv), state


def _bidirectional(chunk_fn, ctx_in, lat_in, s0, with_ctx_out):
    o_ctx, o_lat = None, None
    for direction in range(2):
        tf = (lambda a: a) if direction == 0 else (lambda a: jnp.flip(a, axis=2))
        oc, s_ctx = chunk_fn(*[tf(a) for a in ctx_in[direction]], s0, with_ctx_out)
        ol, _ = chunk_fn(*[tf(a) for a in lat_in[direction]], s_ctx, True)
        ol = tf(ol)
        o_lat = ol if o_lat is None else o_lat + ol
        if with_ctx_out:
            oc = tf(oc)
            o_ctx = oc if o_ctx is None else o_ctx + oc
    return o_ctx, o_lat


def _gla_mixer(parts_ctx, parts_lat, w_up, b_up, norm_gain, with_ctx_out):
    def prep(parts):
        q, k, v, g, r = parts
        b_, t_, _ = q.shape
        q = _heads(q, GLA_HEADS).astype(jnp.float32) * GLA_DK ** -0.5
        k = _heads(k, GLA_HEADS).astype(jnp.float32)
        v = _heads(v, GLA_HEADS).astype(jnp.float32)
        r = r.reshape(b_, t_, 2, GLA_RANK)
        logit = jnp.einsum('btzr,zrk->zbtk', r, w_up) + b_up[:, None, None, :]
        log_a = jax.nn.log_sigmoid(logit.astype(jnp.float32)) / GLA_GATE_NORM
        log_a = log_a.reshape(2, b_, t_, GLA_HEADS, GLA_DK).transpose(0, 1, 3, 2, 4)
        return [(q, k, v, log_a[0]), (q, k, v, log_a[1])], g

    def finish(o, g):
        b_, _, t_, _ = o.shape
        o = _rms_norm(o.transpose(0, 2, 1, 3), norm_gain).reshape(b_, t_, GLA_HEADS * GLA_DV)
        return o * jax.nn.silu(g.astype(jnp.float32))

    ctx_in, g_ctx = prep(parts_ctx)
    lat_in, g_lat = prep(parts_lat)
    s0 = jnp.zeros((parts_lat[0].shape[0], GLA_HEADS, GLA_DK, GLA_DV), jnp.float32)
    o_ctx, o_lat = _bidirectional(_gla_chunked, ctx_in, lat_in, s0, with_ctx_out)
    return (finish(o_ctx, g_ctx) if with_ctx_out else None), finish(o_lat, g_lat)


def _gdn_mixer(parts_ctx, parts_lat, conv_w, a_log, dt_bias, norm_gain, with_ctx_out):
    def prep(parts):
        q, k, v, z, b, a = parts
        b_, t_, _ = q.shape
        qkv = jax.nn.silu(_depthwise_conv(jnp.concatenate([q, k, v], axis=-1), conv_w))
        q, k, v = jnp.split(qkv, [GDN_HEADS * GDN_DK, 2 * GDN_HEADS * GDN_DK], axis=-1)
        q = _l2norm(_heads(q, GDN_HEADS).astype(jnp.float32)) * GDN_DK ** -0.5
        k = _l2norm(_heads(k, GDN_HEADS).astype(jnp.float32))
        v = _heads(v, GDN_HEADS).astype(jnp.float32)
        beta = jax.nn.sigmoid(b.astype(jnp.float32)).reshape(b_, t_, 2, GDN_HEADS).transpose(2, 0, 3, 1)
        a = a.astype(jnp.float32).reshape(b_, t_, 2, GDN_HEADS).transpose(2, 0, 3, 1)
        g = -jnp.exp(a_log)[:, None, :, None] * jax.nn.softplus(a + dt_bias[:, None, :, None])
        return [(q, k, v, g[0], beta[0]), (q, k, v, g[1], beta[1])], z

    def finish(o, z):
        b_, _, t_, _ = o.shape
        o = _rms_norm(o.transpose(0, 2, 1, 3), norm_gain)
        o = o * jax.nn.silu(z.astype(jnp.float32).reshape(b_, t_, GDN_HEADS, GDN_DV))
        return o.reshape(b_, t_, GDN_HEADS * GDN_DV)

    ctx_in, z_ctx = prep(parts_ctx)
    lat_in, z_lat = prep(parts_lat)
    s0 = jnp.zeros((parts_lat[0].shape[0], GDN_HEADS, GDN_DK, GDN_DV), jnp.float32)
    o_ctx, o_lat = _bidirectional(_gdn_chunked, ctx_in, lat_in, s0, with_ctx_out)
    return (finish(o_ctx, z_ctx) if with_ctx_out else None), finish(o_lat, z_lat)


def _attend(q, k, v):
    s = jnp.einsum('bqhgd,bkhd->bhgqk', q, k).astype(jnp.float32) * HEAD_DIM ** -0.5
    p = jax.nn.softmax(s, axis=-1).astype(v.dtype)
    return jnp.einsum('bhgqk,bkhd->bqhgd', p, v)


def _gqa_mixer(parts_ctx, parts_lat, qk_gain, cos, sin, with_ctx_out):
    def prep(parts, rotary):
        q, k, v = parts
        b_, t_, _ = q.shape
        q = _rms_norm(q.reshape(b_, t_, ATTN_KV_HEADS, ATTN_GROUP, HEAD_DIM), qk_gain[0])
        k = _rms_norm(k.reshape(b_, t_, ATTN_KV_HEADS, HEAD_DIM), qk_gain[1])
        v = v.reshape(b_, t_, ATTN_KV_HEADS, HEAD_DIM).astype(jnp.float32)
        if rotary:
            q = _apply_axial_rope(q, cos, sin)
            k = _apply_axial_rope(k, cos, sin)
        return q, k, v

    q_c, k_c, v_c = prep(parts_ctx, False)
    q_l, k_l, v_l = prep(parts_lat, True)
    k_all = jnp.concatenate([k_c, k_l], axis=1)
    v_all = jnp.concatenate([v_c, v_l], axis=1)
    b_, n_, _, _, _ = q_l.shape
    nb = n_ // Q_BLOCK
    q_blocks = q_l.reshape(b_, nb, Q_BLOCK, ATTN_KV_HEADS, ATTN_GROUP, HEAD_DIM).transpose(1, 0, 2, 3, 4, 5)
    o_lat = lax.map(lambda qb: _attend(qb, k_all, v_all), q_blocks)
    o_lat = o_lat.transpose(1, 0, 2, 3, 4, 5).reshape(b_, n_, ATTN_HEADS * HEAD_DIM)
    o_ctx = None
    if with_ctx_out:
        o_ctx = _attend(q_c, k_c, v_c).reshape(b_, q_c.shape[1], ATTN_HEADS * HEAD_DIM)
    return o_ctx, o_lat


def _expert_choice_ffn(h, router, w1, w3, w2):
    b_, t_, _ = h.shape
    cap = EC_FACTOR * t_ // N_EXPERTS
    logits = jnp.einsum('btd,de->bet', h, router).astype(jnp.float32)
    aff = jax.nn.softmax(logits, axis=1)
    gate, idx = lax.top_k(aff, cap)
    bidx = jnp.arange(b_)[:, None, None]
    xs = h[bidx, idx]
    a = jnp.einsum('becd,edf->becf', xs, w1)
    u = jnp.einsum('becd,edf->becf', xs, w3)
    y = jnp.einsum('becf,efd->becd', jax.nn.silu(a) * u, w2) * gate[..., None].astype(h.dtype)
    return jnp.zeros_like(h).at[bidx, idx].add(y)


def setup_inputs(seed: int = 0) -> dict:
    key = jax.random.key(seed)
    ks = jax.random.split(key, 24)
    f32 = jnp.float32
    d = D_MODEL

    def nrm(k, shape, scale):
        return jax.random.normal(k, shape, f32) * scale

    dt = jnp.exp(jax.random.uniform(ks[10], (DEPTH, 2, GDN_HEADS), f32, math.log(1e-3), math.log(1e-1)))
    return {
        'x': nrm(ks[0], (BATCH, SEQ, d), 1.0),
        'c': nrm(ks[1], (BATCH, d), 1.0),
        'ctx': nrm(ks[2], (BATCH, CTX_LEN, d), 1.0),
        'c_ctx': nrm(ks[3], (d,), 1.0),
        'w_ada': nrm(ks[4], (DEPTH, d, N_MOD * d), 0.5 * d ** -0.5),
        'b_ada': nrm(ks[5], (DEPTH, N_MOD * d), 0.02),
        'w_in': nrm(ks[6], (DEPTH, d, IN_WIDTH), d ** -0.5),
        'w_out': nrm(ks[7], (DEPTH, MIX_WIDTH, d), DEEPNORM_BETA * MIX_WIDTH ** -0.5),
        'gla_w_up': nrm(ks[8], (DEPTH, 2, GLA_RANK, GLA_HEADS * GLA_DK), GLA_RANK ** -0.5),
        'gla_b_up': nrm(ks[9], (DEPTH, 2, GLA_HEADS * GLA_DK), 0.1),
        'gla_norm': 1.0 + nrm(ks[11], (DEPTH, GLA_DV), 0.02),
        'gdn_conv': nrm(ks[12], (DEPTH, GDN_CONV, GDN_CONV_CH), GDN_CONV ** -0.5),
        'gdn_a_log': jnp.log(jax.random.uniform(ks[13], (DEPTH, 2, GDN_HEADS), f32, 1.0, 16.0)),
        'gdn_dt_bias': dt + jnp.log(-jnp.expm1(-dt)),
        'gdn_norm': 1.0 + nrm(ks[14], (DEPTH, GDN_DV), 0.02),
        'attn_qk_norm': 1.0 + nrm(ks[15], (DEPTH, 2, HEAD_DIM), 0.02),
        'ln_gain': 1.0 + nrm(ks[16], (DEPTH, 2, d), 0.02),
        'ln_bias': nrm(ks[17], (DEPTH, 2, d), 0.02),
        'router': nrm(ks[18], (DEPTH, d, N_EXPERTS), d ** -0.5),
        'w1': nrm(ks[19], (DEPTH, N_EXPERTS, d, EXPERT_FF), d ** -0.5),
        'w3': nrm(ks[20], (DEPTH, N_EXPERTS, d, EXPERT_FF), d ** -0.5),
        'w2': nrm(ks[21], (DEPTH, N_EXPERTS, EXPERT_FF, d), DEEPNORM_BETA * EXPERT_FF ** -0.5),
    }


def reference(x, c, ctx, c_ctx, w_ada, b_ada, w_in, w_out, gla_w_up, gla_b_up, gla_norm,
              gdn_conv, gdn_a_log, gdn_dt_bias, gdn_norm, attn_qk_norm, ln_gain, ln_bias,
              router, w1, w3, w2):
    bsz, n_lat, d = x.shape
    cos, sin = _axial_rope_tables(n_lat)
    x_lat, x_ctx = x, ctx
    for l in range(DEPTH):
        last = l == DEPTH - 1
        mod_lat = (jax.nn.silu(c) @ w_ada[l] + b_ada[l]).reshape(bsz, N_MOD, 1, d)
        mod_ctx = (jax.nn.silu(c_ctx) @ w_ada[l] + b_ada[l]).reshape(N_MOD, 1, 1, d)

        p_lat = _split_cols(_modulate(x_lat, mod_lat[:, 0], mod_lat[:, 1]) @ w_in[l])
        p_ctx = _split_cols(_modulate(x_ctx, mod_ctx[0], mod_ctx[1]) @ w_in[l])
        gla_c, gla_l = _gla_mixer(p_ctx[0:5], p_lat[0:5], gla_w_up[l], gla_b_up[l], gla_norm[l], not last)
        gdn_c, gdn_l = _gdn_mixer(p_ctx[5:11], p_lat[5:11], gdn_conv[l], gdn_a_log[l], gdn_dt_bias[l],
                                  gdn_norm[l], not last)
        att_c, att_l = _gqa_mixer(p_ctx[11:14], p_lat[11:14], attn_qk_norm[l], cos, sin, not last)
        y_lat = jnp.concatenate([gla_l, gdn_l, att_l], axis=-1).astype(x_lat.dtype) @ w_out[l]
        x_lat = _layer_norm(DEEPNORM_ALPHA * x_lat + mod_lat[:, 2] * y_lat, ln_gain[l, 0], ln_bias[l, 0])
        if not last:
            y_ctx = jnp.concatenate([gla_c, gdn_c, att_c], axis=-1).astype(x_ctx.dtype) @ w_out[l]
            x_ctx = _layer_norm(DEEPNORM_ALPHA * x_ctx + mod_ctx[2] * y_ctx, ln_gain[l, 0], ln_bias[l, 0])

        f_lat = _expert_choice_ffn(_modulate(x_lat, mod_lat[:, 3], mod_lat[:, 4]), router[l], w1[l], w3[l], w2[l])
        x_lat = _layer_norm(DEEPNORM_ALPHA * x_lat + mod_lat[:, 5] * f_lat, ln_gain[l, 1], ln_bias[l, 1])
        if not last:
            f_ctx = _expert_choice_ffn(_modulate(x_ctx, mod_ctx[3], mod_ctx[4]), router[l], w1[l], w3[l], w2[l])
            x_ctx = _layer_norm(DEEPNORM_ALPHA * x_ctx + mod_ctx[5] * f_ctx, ln_gain[l, 1], ln_bias[l, 1])
    return x_lat
```

```python
import functools
import math

import numpy as np
import jax
import jax.numpy as jnp
from jax import lax
from jax.experimental import pallas as pl
from jax.experimental.pallas import tpu as pltpu

F32 = jnp.float32
BF16 = jnp.bfloat16

D_MODEL = 2048
DEPTH = 2
GRID_W = 64
HEAD_DIM = 128
CHUNK = 64
GLA_HEADS = 4
GLA_DK = 64
GLA_DV = 128
GLA_RANK = 16
GLA_GATE_NORM = 16.0
GDN_HEADS = 4
GDN_DK = 128
GDN_DV = 128
GDN_CONV = 5
ATTN_HEADS = 8
ATTN_KV_HEADS = 2
ATTN_GROUP = ATTN_HEADS // ATTN_KV_HEADS
ROPE_THETA = 10000.0
N_EXPERTS = 16
EC_FACTOR = 2
EXPERT_FF = 2048
N_MOD = 6
DEEPNORM_ALPHA = (2 * DEPTH) ** 0.25

_SRC = dict(gla_q=(0, 256), gla_k=(256, 256), gla_v=(512, 512), gla_g=(1024, 512), gla_r=(1536, 32),
            gdn_q=(1568, 512), gdn_k=(2080, 512), gdn_v=(2592, 512), gdn_z=(3104, 512), gdn_b=(3616, 8),
            gdn_a=(3624, 8), att_q=(3632, 1024), att_k=(4656, 256), att_v=(4912, 256))
_MAIN_ORDER = ("att_q", "gla_v", "gla_g", "gdn_q", "gdn_k", "gdn_v", "gdn_z", "gla_q", "gla_k", "att_k", "att_v")
_SMALL_ORDER = ("gla_r", "gdn_b", "gdn_a")
_OFF = {}
_o = 0
for _n in _MAIN_ORDER:
    _OFF[_n] = _o
    _o += _SRC[_n][1]
MAIN_W = _o
_o = 0
for _n in _SMALL_ORDER:
    _OFF[_n] = _o
    _o += _SRC[_n][1]
SMALL_W = 128

VMEM_LIMIT = 56 * 1024 * 1024


def _cparams(n_axes):
    return pltpu.CompilerParams(dimension_semantics=("arbitrary",) * n_axes, vmem_limit_bytes=VMEM_LIMIT)


def _split_bf16(a):
    hi = a.astype(BF16)
    lo = (a - hi.astype(F32)).astype(BF16)
    return hi, lo


def _sigmoid(x):
    return 1.0 / (1.0 + jnp.exp(-x))


def _ada_kernel(c_ref, w_ref, b_ref, o_ref):
    c = c_ref[...]
    s = (c * _sigmoid(c)).astype(BF16)
    o_ref[...] = jnp.dot(s, w_ref[...].astype(BF16), preferred_element_type=F32) + b_ref[...]


def _ada(cond8, w_ada, b_ada, tn=1024):
    depth, d, n = w_ada.shape
    return pl.pallas_call(
        _ada_kernel,
        out_shape=jax.ShapeDtypeStruct((depth, 8, n), F32),
        grid=(depth, n // tn),
        in_specs=[pl.BlockSpec((8, d), lambda l, j: (0, 0)),
                  pl.BlockSpec((None, d, tn), lambda l, j: (l, 0, j)),
                  pl.BlockSpec((None, 1, tn), lambda l, j: (l, 0, j))],
        out_specs=pl.BlockSpec((None, 8, tn), lambda l, j: (l, 0, j)),
        compiler_params=_cparams(2),
        name="ada",
    )(cond8, w_ada, b_ada.reshape(depth, 1, n))


def _inproj_kernel(x_ref, shift_ref, scale_ref, w_ref, wsh_ref, wsl_ref, o_ref, os_ref, h_ref):
    @pl.when(pl.program_id(1) == 0)
    def _():
        h = x_ref[...] * (1.0 + scale_ref[...]) + shift_ref[...]
        hi, lo = _split_bf16(h)
        h_ref[...] = hi
        wsh = wsh_ref[...]
        os_ref[...] = (jnp.dot(hi, wsh, preferred_element_type=F32)
                       + jnp.dot(lo, wsh, preferred_element_type=F32)
                       + jnp.dot(hi, wsl_ref[...], preferred_element_type=F32))

    o_ref[...] = jnp.dot(h_ref[...], w_ref[...], preferred_element_type=F32)


def _inproj(x2d, mods, row_of_tile, w_main, ws_hi, ws_lo, tm, tn=1024):
    rows, d = x2d.shape
    n = w_main.shape[1]
    return pl.pallas_call(
        _inproj_kernel,
        out_shape=(jax.ShapeDtypeStruct((rows, n), F32), jax.ShapeDtypeStruct((rows, SMALL_W), F32)),
        grid=(rows // tm, n // tn),
        in_specs=[pl.BlockSpec((tm, d), lambda i, j: (i, 0)),
                  pl.BlockSpec((None, 1, d), lambda i, j: (row_of_tile(i) * N_MOD + 0, 0, 0)),
                  pl.BlockSpec((None, 1, d), lambda i, j: (row_of_tile(i) * N_MOD + 1, 0, 0)),
                  pl.BlockSpec((d, tn), lambda i, j: (0, j)),
                  pl.BlockSpec((d, SMALL_W), lambda i, j: (0, 0)),
                  pl.BlockSpec((d, SMALL_W), lambda i, j: (0, 0))],
        out_specs=(pl.BlockSpec((tm, tn), lambda i, j: (i, j)),
                   pl.BlockSpec((tm, SMALL_W), lambda i, j: (i, 0))),
        scratch_shapes=[pltpu.VMEM((tm, d), BF16)],
        compiler_params=_cparams(2),
        name="inproj",
    )(x2d, mods, mods, w_main, ws_hi, ws_lo)


def _rms_rope(x, gain, cos, sin):
    y = x * lax.rsqrt(jnp.mean(x * x, axis=-1, keepdims=True) + 1e-6) * gain
    if cos is not None:
        lane = lax.broadcasted_iota(jnp.int32, y.shape, 1)
        partner = jnp.where((lane % 64) < 32, pltpu.roll(y, 96, 1), pltpu.roll(y, 32, 1))
        y = y * cos + partner * sin
    return y


def _attn_kernel(*refs, seg_rope, q_rope, seg_len):
    n_seg = len(seg_len)
    q_ref = refs[0]
    kv_refs = refs[1:1 + 2 * n_seg]
    pos = 1 + 2 * n_seg
    gain_ref = refs[pos]
    pos += 1
    cq_ref = sq_ref = ck_ref = sk_ref = None
    if q_rope:
        cq_ref, sq_ref = refs[pos], refs[pos + 1]
        pos += 2
    if any(seg_rope):
        ck_ref, sk_ref = refs[pos], refs[pos + 1]
        pos += 2
    o_ref, k_scr, v_scr = refs[pos], refs[pos + 1], refs[pos + 2]
    tq = q_ref.shape[0]

    @pl.when(pl.program_id(2) == 0)
    def _():
        off = 0
        for s in range(n_seg):
            k = kv_refs[2 * s][...]
            cos = ck_ref[...] if seg_rope[s] else None
            sin = sk_ref[...] if seg_rope[s] else None
            k_scr[off:off + seg_len[s], :] = _rms_rope(k, gain_ref[1:2, :], cos, sin).astype(BF16)
            v_scr[off:off + seg_len[s], :] = kv_refs[2 * s + 1][...].astype(BF16)
            off += seg_len[s]

    q = q_ref[...]
    cq = cq_ref[...] if q_rope else None
    sq = sq_ref[...] if q_rope else None
    qs = []
    for g in range(ATTN_GROUP):
        qg = _rms_rope(q[:, g * HEAD_DIM:(g + 1) * HEAD_DIM], gain_ref[0:1, :], cq, sq)
        qs.append((qg * (HEAD_DIM ** -0.5)).astype(BF16))
    q4 = jnp.concatenate(qs, axis=0)
    s = lax.dot_general(q4, k_scr[...], (((1,), (1,)), ((), ())), preferred_element_type=F32)
    m = jnp.max(s, axis=-1, keepdims=True)
    p = jnp.exp(s - m)
    l = jnp.sum(p, axis=-1, keepdims=True)
    o = jnp.dot(p.astype(BF16), v_scr[...], preferred_element_type=F32) / l
    o_ref[...] = jnp.concatenate([o[g * tq:(g + 1) * tq] for g in range(ATTN_GROUP)], axis=1).astype(o_ref.dtype)


def _attention(p_q, kv_segs, gain, rope_tabs, bsz, q_rope, tq):
    rows = p_q.shape[0]
    nq_rows = rows // bsz
    nq = nq_rows // tq
    qcol = _OFF["att_q"] // (ATTN_GROUP * HEAD_DIM)
    kcol = _OFF["att_k"] // HEAD_DIM
    vcol = _OFF["att_v"] // HEAD_DIM
    seg_len = tuple(n for _, n, _ in kv_segs)
    seg_rope = tuple(r for _, _, r in kv_segs)
    in_specs = [pl.BlockSpec((tq, ATTN_GROUP * HEAD_DIM), lambda b, h, i: (b * nq + i, qcol + h))]
    args = [p_q]
    for arr, n, _ in kv_segs:
        in_specs.append(pl.BlockSpec((n, HEAD_DIM), lambda b, h, i: (b, kcol + h)))
        in_specs.append(pl.BlockSpec((n, HEAD_DIM), lambda b, h, i: (b, vcol + h)))
        args += [arr, arr]
    in_specs.append(pl.BlockSpec((2, HEAD_DIM), lambda b, h, i: (0, 0)))
    args.append(gain)
    cos_t, sin_t = rope_tabs
    if q_rope:
        in_specs += [pl.BlockSpec((tq, HEAD_DIM), lambda b, h, i: (i, 0))] * 2
        args += [cos_t, sin_t]
    if any(seg_rope):
        n_r = [n for _, n, r in kv_segs if r][0]
        in_specs += [pl.BlockSpec((n_r, HEAD_DIM), lambda b, h, i: (0, 0))] * 2
        args += [cos_t, sin_t]
    nk = sum(seg_len)
    return pl.pallas_call(
        functools.partial(_attn_kernel, seg_rope=seg_rope, q_rope=q_rope, seg_len=seg_len),
        out_shape=jax.ShapeDtypeStruct((rows, ATTN_HEADS * HEAD_DIM), BF16),
        grid=(bsz, ATTN_KV_HEADS, nq),
        in_specs=in_specs,
        out_specs=pl.BlockSpec((tq, ATTN_GROUP * HEAD_DIM), lambda b, h, i: (b * nq + i, h)),
        scratch_shapes=[pltpu.VMEM((nk, HEAD_DIM), BF16), pltpu.VMEM((nk, HEAD_DIM), BF16)],
        compiler_params=_cparams(3),
        name="attention",
    )(*args)


def _layer_norm_rows(z, gain, bias):
    mu = jnp.mean(z, axis=-1, keepdims=True)
    zc = z - mu
    var = jnp.mean(zc * zc, axis=-1, keepdims=True)
    return zc * lax.rsqrt(var + 1e-5) * gain + bias


def _outproj_kernel(gla_ref, gdn_ref, att_ref, x_ref, gate_ref, w_ref, gain_ref, bias_ref, o_ref):
    mix = jnp.concatenate([gla_ref[...].astype(BF16), gdn_ref[...].astype(BF16), att_ref[...].astype(BF16)], axis=1)
    y = jnp.dot(mix, w_ref[...], preferred_element_type=F32)
    z = DEEPNORM_ALPHA * x_ref[...] + gate_ref[...] * y
    o_ref[...] = _layer_norm_rows(z, gain_ref[...], bias_ref[...])


def _outproj(gla, gdn, att, x2d, mods, row_of_tile, w_out, gain, bias, tm):
    rows, d = x2d.shape
    return pl.pallas_call(
        _outproj_kernel,
        out_shape=jax.ShapeDtypeStruct((rows, d), F32),
        grid=(rows // tm,),
        in_specs=[pl.BlockSpec((tm, gla.shape[1]), lambda i: (i, 0)),
                  pl.BlockSpec((tm, gdn.shape[1]), lambda i: (i, 0)),
                  pl.BlockSpec((tm, att.shape[1]), lambda i: (i, 0)),
                  pl.BlockSpec((tm, d), lambda i: (i, 0)),
                  pl.BlockSpec((None, 1, d), lambda i: (row_of_tile(i) * N_MOD + 2, 0, 0)),
                  pl.BlockSpec((w_out.shape[0], d), lambda i: (0, 0)),
                  pl.BlockSpec((1, d), lambda i: (0, 0)),
                  pl.BlockSpec((1, d), lambda i: (0, 0))],
        out_specs=pl.BlockSpec((tm, d), lambda i: (i, 0)),
        compiler_params=_cparams(1),
        name="outproj_ln",
    )(gla, gdn, att, x2d, mods, w_out, gain.reshape(1, d), bias.reshape(1, d))


def _router_kernel(x_ref, shift_ref, scale_ref, rh_ref, rl_ref, h_ref, lg_ref):
    h = x_ref[...] * (1.0 + scale_ref[...]) + shift_ref[...]
    hi, lo = _split_bf16(h)
    h_ref[...] = hi
    rh = rh_ref[...]
    lg_ref[...] = (jnp.dot(hi, rh, preferred_element_type=F32) + jnp.dot(lo, rh, preferred_element_type=F32)
                   + jnp.dot(hi, rl_ref[...], preferred_element_type=F32))


def _router(x2d, mods, row_of_tile, r_hi, r_lo, tm):
    rows, d = x2d.shape
    return pl.pallas_call(
        _router_kernel,
        out_shape=(jax.ShapeDtypeStruct((rows, d), BF16), jax.ShapeDtypeStruct((rows, 128), F32)),
        grid=(rows // tm,),
        in_specs=[pl.BlockSpec((tm, d), lambda i: (i, 0)),
                  pl.BlockSpec((None, 1, d), lambda i: (row_of_tile(i) * N_MOD + 3, 0, 0)),
                  pl.BlockSpec((None, 1, d), lambda i: (row_of_tile(i) * N_MOD + 4, 0, 0)),
                  pl.BlockSpec((d, 128), lambda i: (0, 0)),
                  pl.BlockSpec((d, 128), lambda i: (0, 0))],
        out_specs=(pl.BlockSpec((tm, d), lambda i: (i, 0)), pl.BlockSpec((tm, 128), lambda i: (i, 0))),
        compiler_params=_cparams(1),
        name="router",
    )(x2d, mods, mods, r_hi, r_lo)


def _ffn_kernel(x_ref, g_ref, w1_ref, w3_ref, w2_ref, o_ref):
    f = pl.program_id(1)
    w1 = w1_ref[...].astype(BF16)
    w3 = w3_ref[...].astype(BF16)
    w2 = w2_ref[...].astype(BF16)
    rc = x_ref.shape[0] // 2
    for r0 in (0, rc):
        x = x_ref[r0:r0 + rc, :]
        a = jnp.dot(x, w1, preferred_element_type=F32)
        u = jnp.dot(x, w3, preferred_element_type=F32)
        hmid = (a * _sigmoid(a) * u).astype(BF16)
        y = jnp.dot(hmid, w2, preferred_element_type=F32)

        @pl.when(f == 0)
        def _():
            o_ref[r0:r0 + rc, :] = y

        @pl.when(f > 0)
        def _():
            o_ref[r0:r0 + rc, :] += y

    @pl.when(f == pl.num_programs(1) - 1)
    def _():
        o_ref[...] = o_ref[...] * g_ref[...]


def _ffn(xs, gates, w1, w3, w2, tf=256):
    n_e, r, d = xs.shape
    ff = w1.shape[2]
    return pl.pallas_call(
        _ffn_kernel,
        out_shape=jax.ShapeDtypeStruct((n_e, r, d), F32),
        grid=(n_e, ff // tf),
        in_specs=[pl.BlockSpec((None, r, d), lambda e, f: (e, 0, 0)),
                  pl.BlockSpec((None, r, 1), lambda e, f: (e, 0, 0)),
                  pl.BlockSpec((None, d, tf), lambda e, f: (e, 0, f)),
                  pl.BlockSpec((None, d, tf), lambda e, f: (e, 0, f)),
                  pl.BlockSpec((None, tf, d), lambda e, f: (e, f, 0))],
        out_specs=pl.BlockSpec((None, r, d), lambda e, f: (e, 0, 0)),
        compiler_params=_cparams(2),
        name="expert_ffn",
    )(xs, gates, w1, w3, w2)


def _resln_kernel(x_ref, f_ref, gate_ref, gain_ref, bias_ref, o_ref):
    z = DEEPNORM_ALPHA * x_ref[...] + gate_ref[...] * f_ref[...]
    o_ref[...] = _layer_norm_rows(z, gain_ref[...], bias_ref[...])


def _resln(x2d, f2d, mods, row_of_tile, gain, bias, tm):
    rows, d = x2d.shape
    return pl.pallas_call(
        _resln_kernel,
        out_shape=jax.ShapeDtypeStruct((rows, d), F32),
        grid=(rows // tm,),
        in_specs=[pl.BlockSpec((tm, d), lambda i: (i, 0)),
                  pl.BlockSpec((tm, d), lambda i: (i, 0)),
                  pl.BlockSpec((None, 1, d), lambda i: (row_of_tile(i) * N_MOD + 5, 0, 0)),
                  pl.BlockSpec((1, d), lambda i: (0, 0)),
                  pl.BlockSpec((1, d), lambda i: (0, 0))],
        out_specs=pl.BlockSpec((tm, d), lambda i: (i, 0)),
        compiler_params=_cparams(1),
        name="residual_ln",
    )(x2d, f2d, mods, gain.reshape(1, d), bias.reshape(1, d))


def _heads(a, n_heads):
    b, t, _ = a.shape
    return a.reshape(b, t, n_heads, -1).transpose(0, 2, 1, 3)


def _rms_norm(x, gain, eps=1e-6):
    return x * lax.rsqrt(jnp.mean(x * x, axis=-1, keepdims=True) + eps) * gain


def _l2norm(x, eps=1e-6):
    return x * lax.rsqrt(jnp.sum(x * x, axis=-1, keepdims=True) + eps)


def _gla_chunked(q, k, v, log_a, s0, with_out):
    b_, h_, t_, dk = q.shape
    dv = v.shape[-1]
    n = t_ // CHUNK
    q = q.reshape(b_, h_, n, CHUNK, dk)
    k = k.reshape(b_, h_, n, CHUNK, dk)
    v = v.reshape(b_, h_, n, CHUNK, dv)
    cum = jnp.cumsum(log_a.reshape(b_, h_, n, CHUNK, dk), axis=3)
    cum_last = cum[:, :, :, -1]
    kv = jnp.einsum('bhnck,bhncv->bhnkv', k * jnp.exp(cum_last[:, :, :, None] - cum), v)
    d_last = jnp.exp(cum_last)
    seq = lambda a: jnp.moveaxis(a, 2, 0)
    if with_out:
        q_dec = q * jnp.exp(cum)
        lower = jnp.tril(jnp.ones((CHUNK, CHUNK), dtype=bool))
        a_intra = jnp.where(lower, jnp.einsum('bhnik,bhnjk->bhnij', q_dec, k * jnp.exp(-cum)), 0.0)
        o_intra = jnp.einsum('bhnij,bhnjv->bhniv', a_intra, v)
        xs = (seq(kv), seq(d_last), seq(q_dec))
    else:
        xs = (seq(kv), seq(d_last))

    def step(state, inp):
        nxt = inp[1][..., None] * state + inp[0]
        if not with_out:
            return nxt, None
        return nxt, jnp.einsum('bhck,bhkv->bhcv', inp[2], state)

    state, o_inter = lax.scan(step, s0, xs)
    if not with_out:
        return None, state
    o = o_intra + jnp.moveaxis(o_inter, 0, 2)
    return o.reshape(b_, h_, t_, dv), state


def _gdn_chunked(q, k, v, g, beta, s0, with_out):
    b_, h_, t_, dk = q.shape
    dv = v.shape[-1]
    n = t_ // CHUNK
    q = q.reshape(b_, h_, n, CHUNK, dk)
    k = k.reshape(b_, h_, n, CHUNK, dk)
    v = v.reshape(b_, h_, n, CHUNK, dv)
    g = jnp.cumsum(g.reshape(b_, h_, n, CHUNK), axis=-1)
    beta = beta.reshape(b_, h_, n, CHUNK)
    k_beta = k * beta[..., None]
    v_beta = v * beta[..., None]
    lower = jnp.tril(jnp.ones((CHUNK, CHUNK), dtype=bool))
    strict = jnp.tril(jnp.ones((CHUNK, CHUNK), dtype=bool), -1)
    gamma = jnp.where(lower, jnp.exp(jnp.where(lower, g[..., :, None] - g[..., None, :], 0.0)), 0.0)
    l_mat = jnp.where(strict, jnp.einsum('bhnid,bhnjd->bhnij', k_beta, k) * gamma, 0.0)
    eye = jnp.eye(CHUNK, dtype=l_mat.dtype)
    t_mat = lax.linalg.triangular_solve(l_mat + eye, jnp.broadcast_to(eye, l_mat.shape),
                                        left_side=True, lower=True, unit_diagonal=True)
    u = jnp.einsum('bhnij,bhnjv->bhniv', t_mat, v_beta)
    w = jnp.einsum('bhnij,bhnjk->bhnik', t_mat, k_beta * jnp.exp(g)[..., None])
    g_last = g[..., -1]
    k_dec = k * jnp.exp(g_last[..., None] - g)[..., None]
    d_last = jnp.exp(g_last)
    seq = lambda a: jnp.moveaxis(a, 2, 0)
    if with_out:
        q_dec = q * jnp.exp(g)[..., None]
        a_intra = jnp.where(lower, jnp.einsum('bhnik,bhnjk->bhnij', q, k) * gamma, 0.0)
        xs = tuple(seq(a) for a in (u, w, k_dec, d_last, q_dec, a_intra))
    else:
        xs = tuple(seq(a) for a in (u, w, k_dec, d_last))

    def step(state, inp):
        u_c, w_c, k_c, d_c = inp[:4]
        v_new = u_c - jnp.einsum('bhck,bhkv->bhcv', w_c, state)
        nxt = d_c[..., None, None] * state + jnp.einsum('bhck,bhcv->bhkv', k_c, v_new)
        if not with_out:
            return nxt, None
        q_c, a_c = inp[4:]
        o = jnp.einsum('bhck,bhkv->bhcv', q_c, state) + jnp.einsum('bhij,bhjv->bhiv', a_c, v_new)
        return nxt, o

    state, o = lax.scan(step, s0, xs)
    if not with_out:
        return None, state
    return jnp.moveaxis(o, 0, 2).reshape(b_, h_, t_, dv), state


def _bidirectional(chunk_fn, ctx_in, lat_in, s0, with_ctx_out):
    o_ctx, o_lat = None, None
    for direction in range(2):
        tf = (lambda a: a) if direction == 0 else (lambda a: jnp.flip(a, axis=2))
        oc, s_ctx = chunk_fn(*[tf(a) for a in ctx_in[direction]], s0, with_ctx_out)
        ol, _ = chunk_fn(*[tf(a) for a in lat_in[direction]], s_ctx, True)
        ol = tf(ol)
        o_lat = ol if o_lat is None else o_lat + ol
        if with_ctx_out:
            oc = tf(oc)
            o_ctx = oc if o_ctx is None else o_ctx + oc
    return o_ctx, o_lat


def _gla_mixer(parts_ctx, parts_lat, w_up, b_up, norm_gain, with_ctx_out):
    def prep(parts):
        q, k, v, g, r = parts
        b_, t_, _ = q.shape
        q = _heads(q, GLA_HEADS) * GLA_DK ** -0.5
        k = _heads(k, GLA_HEADS)
        v = _heads(v, GLA_HEADS)
        r = r.reshape(b_, t_, 2, GLA_RANK)
        logit = jnp.einsum('btzr,zrk->zbtk', r, w_up, precision=lax.Precision.HIGHEST) + b_up[:, None, None, :]
        log_a = jax.nn.log_sigmoid(logit) / GLA_GATE_NORM
        log_a = log_a.reshape(2, b_, t_, GLA_HEADS, GLA_DK).transpose(0, 1, 3, 2, 4)
        return [(q, k, v, log_a[0]), (q, k, v, log_a[1])], g

    def finish(o, g):
        b_, _, t_, _ = o.shape
        o = _rms_norm(o.transpose(0, 2, 1, 3), norm_gain).reshape(b_, t_, GLA_HEADS * GLA_DV)
        return o * jax.nn.silu(g)

    ctx_in, g_ctx = prep(parts_ctx)
    lat_in, g_lat = prep(parts_lat)
    s0 = jnp.zeros((parts_lat[0].shape[0], GLA_HEADS, GLA_DK, GLA_DV), F32)
    o_ctx, o_lat = _bidirectional(_gla_chunked, ctx_in, lat_in, s0, with_ctx_out)
    return (finish(o_ctx, g_ctx) if with_ctx_out else None), finish(o_lat, g_lat)


def _depthwise_conv(x, w):
    k = w.shape[0]
    return lax.conv_general_dilated(
        x, w[:, None, :], window_strides=(1,), padding=[(k // 2, k // 2)],
        dimension_numbers=('NWC', 'WIO', 'NWC'), feature_group_count=x.shape[-1])


def _gdn_mixer(parts_ctx, parts_lat, conv_w, a_log, dt_bias, norm_gain, with_ctx_out):
    def prep(parts):
        q, k, v, z, b, a = parts
        b_, t_, _ = q.shape
        qkv = jax.nn.silu(_depthwise_conv(jnp.concatenate([q, k, v], axis=-1), conv_w))
        q, k, v = jnp.split(qkv, [GDN_HEADS * GDN_DK, 2 * GDN_HEADS * GDN_DK], axis=-1)
        q = _l2norm(_heads(q, GDN_HEADS)) * GDN_DK ** -0.5
        k = _l2norm(_heads(k, GDN_HEADS))
        v = _heads(v, GDN_HEADS)
        beta = jax.nn.sigmoid(b).reshape(b_, t_, 2, GDN_HEADS).transpose(2, 0, 3, 1)
        a = a.reshape(b_, t_, 2, GDN_HEADS).transpose(2, 0, 3, 1)
        g = -jnp.exp(a_log)[:, None, :, None] * jax.nn.softplus(a + dt_bias[:, None, :, None])
        return [(q, k, v, g[0], beta[0]), (q, k, v, g[1], beta[1])], z

    def finish(o, z):
        b_, _, t_, _ = o.shape
        o = _rms_norm(o.transpose(0, 2, 1, 3), norm_gain)
        o = o * jax.nn.silu(z.reshape(b_, t_, GDN_HEADS, GDN_DV))
        return o.reshape(b_, t_, GDN_HEADS * GDN_DV)

    ctx_in, z_ctx = prep(parts_ctx)
    lat_in, z_lat = prep(parts_lat)
    s0 = jnp.zeros((parts_lat[0].shape[0], GDN_HEADS, GDN_DK, GDN_DV), F32)
    o_ctx, o_lat = _bidirectional(_gdn_chunked, ctx_in, lat_in, s0, with_ctx_out)
    return (finish(o_ctx, z_ctx) if with_ctx_out else None), finish(o_lat, z_lat)


def _rope_tables(n_tokens):
    rows = n_tokens // GRID_W
    row = jnp.broadcast_to(jnp.arange(rows)[:, None], (rows, GRID_W)).reshape(-1).astype(F32)
    col = jnp.broadcast_to(jnp.arange(GRID_W)[None, :], (rows, GRID_W)).reshape(-1).astype(F32)
    half = HEAD_DIM // 2
    inv = ROPE_THETA ** (-jnp.arange(0, half, 2, dtype=F32) / half)
    a_row = row[:, None] * inv
    a_col = col[:, None] * inv
    cos_t = jnp.concatenate([jnp.cos(a_row), jnp.cos(a_row), jnp.cos(a_col), jnp.cos(a_col)], axis=-1)
    sin_t = jnp.concatenate([-jnp.sin(a_row), jnp.sin(a_row), -jnp.sin(a_col), jnp.sin(a_col)], axis=-1)
    return cos_t, sin_t


def _permute_w_in(w):
    main = jnp.concatenate([w[:, _SRC[n][0]:_SRC[n][0] + _SRC[n][1]] for n in _MAIN_ORDER], axis=1)
    small = jnp.concatenate([w[:, _SRC[n][0]:_SRC[n][0] + _SRC[n][1]] for n in _SMALL_ORDER], axis=1)
    small = jnp.pad(small, ((0, 0), (0, SMALL_W - small.shape[1])))
    return main, small


def _cols(p, name, bsz):
    w = _SRC[name][1]
    return p[:, _OFF[name]:_OFF[name] + w].reshape(bsz, -1, w)


def _moe(x2d_sets, mods, row_fns, r_hi, r_lo, w1, w3, w2, gain, bias, bsz, tms):
    hs, gates, idxs, caps = [], [], [], []
    for x2d, row_fn, tm in zip(x2d_sets, row_fns, tms):
        h, lg = _router(x2d, mods, row_fn, r_hi, r_lo, tm)
        t = x2d.shape[0] // bsz
        cap = EC_FACTOR * t // N_EXPERTS
        aff = jax.nn.softmax(lg[:, :N_EXPERTS].reshape(bsz, t, N_EXPERTS), axis=-1).transpose(0, 2, 1)
        gate, idx = lax.top_k(aff, cap)
        hs.append(h.reshape(bsz, t, -1))
        gates.append(gate)
        idxs.append(idx)
        caps.append(cap)
    bidx = jnp.arange(bsz)[:, None, None]
    xs = jnp.concatenate([h[bidx, idx].transpose(1, 0, 2, 3).reshape(N_EXPERTS, bsz * cap, -1)
                          for h, idx, cap in zip(hs, idxs, caps)], axis=1)
    gs = jnp.concatenate([g.transpose(1, 0, 2).reshape(N_EXPERTS, bsz * cap) for g, cap in zip(gates, caps)], axis=1)
    y = _ffn(xs, gs[..., None], w1, w3, w2)
    outs = []
    off = 0
    for x2d, row_fn, tm, idx, cap in zip(x2d_sets, row_fns, tms, idxs, caps):
        t = x2d.shape[0] // bsz
        ys = y[:, off:off + bsz * cap].reshape(N_EXPERTS, bsz, cap, -1).transpose(1, 0, 2, 3)
        off += bsz * cap
        f = jnp.zeros((bsz, t, x2d.shape[1]), F32).at[bidx, idx].add(ys)
        outs.append(_resln(x2d, f.reshape(x2d.shape), mods, row_fn, gain, bias, tm))
    return outs


def kernel(x, c, ctx, c_ctx, w_ada, b_ada, w_in, w_out, gla_w_up, gla_b_up, gla_norm, gdn_conv, gdn_a_log,
           gdn_dt_bias, gdn_norm, attn_qk_norm, ln_gain, ln_bias, router, w1, w3, w2):
    bsz, n_lat, d = x.shape
    n_ctx = ctx.shape[1]
    depth = w_ada.shape[0]
    cos_t, sin_t = _rope_tables(n_lat)

    cond8 = jnp.zeros((8, d), F32).at[:bsz].set(c).at[bsz].set(c_ctx)
    mods_all = _ada(cond8, w_ada, b_ada)

    tm_lat, tm_ctx = 512, n_ctx
    lat_row = lambda i: i // (n_lat // tm_lat)
    ctx_row = lambda i: bsz + 0 * i
    tm_lat_ln = 256
    lat_row_ln = lambda i: i // (n_lat // tm_lat_ln)

    x_lat = x.reshape(bsz * n_lat, d)
    x_ctx = ctx.reshape(bsz * n_ctx, d)
    for l in range(depth):
        last = l == depth - 1
        mods = mods_all[l].reshape(8 * N_MOD, 1, d)
        w_main, w_small = _permute_w_in(w_in[l])
        w_main = w_main.astype(BF16)
        ws_hi, ws_lo = _split_bf16(w_small)

        p_lat, ps_lat = _inproj(x_lat, mods, lat_row, w_main, ws_hi, ws_lo, tm_lat)
        p_ctx, ps_ctx = _inproj(x_ctx, mods, ctx_row, w_main, ws_hi, ws_lo, tm_ctx)

        def parts(p, ps, names):
            out = []
            for n in names:
                if n in _SMALL_ORDER:
                    w = _SRC[n][1]
                    out.append(ps[:, _OFF[n]:_OFF[n] + w].reshape(bsz, -1, w))
                else:
                    out.append(_cols(p, n, bsz))
            return out

        gla_names = ("gla_q", "gla_k", "gla_v", "gla_g", "gla_r")
        gdn_names = ("gdn_q", "gdn_k", "gdn_v", "gdn_z", "gdn_b", "gdn_a")
        gla_c, gla_l = _gla_mixer(parts(p_ctx, ps_ctx, gla_names), parts(p_lat, ps_lat, gla_names),
                                  gla_w_up[l], gla_b_up[l], gla_norm[l], not last)
        gdn_c, gdn_l = _gdn_mixer(parts(p_ctx, ps_ctx, gdn_names), parts(p_lat, ps_lat, gdn_names),
                                  gdn_conv[l], gdn_a_log[l], gdn_dt_bias[l], gdn_norm[l], not last)

        att_l = _attention(p_lat, [(p_ctx, n_ctx, False), (p_lat, n_lat, True)], attn_qk_norm[l],
                           (cos_t, sin_t), bsz, True, 128)
        w_out_b = w_out[l].astype(BF16)
        x_lat = _outproj(gla_l.reshape(bsz * n_lat, -1), gdn_l.reshape(bsz * n_lat, -1), att_l, x_lat, mods,
                         lat_row_ln, w_out_b, ln_gain[l, 0], ln_bias[l, 0], tm_lat_ln)
        r_hi, r_lo = _split_bf16(jnp.pad(router[l], ((0, 0), (0, 128 - N_EXPERTS))))
        if not last:
            att_c = _attention(p_ctx, [(p_ctx, n_ctx, False)], attn_qk_norm[l], (cos_t, sin_t), bsz, False, n_ctx)
            x_ctx = _outproj(gla_c.reshape(bsz * n_ctx, -1), gdn_c.reshape(bsz * n_ctx, -1), att_c, x_ctx, mods,
                             ctx_row, w_out_b, ln_gain[l, 0], ln_bias[l, 0], n_ctx)
            x_lat, x_ctx = _moe([x_lat, x_ctx], mods, [lat_row_ln, ctx_row], r_hi, r_lo, w1[l], w3[l], w2[l],
                                ln_gain[l, 1], ln_bias[l, 1], bsz, [tm_lat_ln, n_ctx])
        else:
            (x_lat,) = _moe([x_lat], mods, [lat_row_ln], r_hi, r_lo, w1[l], w3[l], w2[l],
                            ln_gain[l, 1], ln_bias[l, 1], bsz, [tm_lat_ln])
    return x_lat.reshape(bsz, n_lat, d)
```

```python
import functools

import jax
import jax.numpy as jnp
from jax import lax
from jax.experimental import pallas as pl
from jax.experimental.pallas import tpu as pltpu

F32 = jnp.float32
BF16 = jnp.bfloat16

D_MODEL = 2048
DEPTH = 2
GRID_W = 64
HEAD_DIM = 128
CHUNK = 64
GLA_HEADS = 4
GLA_DK = 64
GLA_DV = 128
GLA_RANK = 16
GLA_GATE_NORM = 16.0
GDN_HEADS = 4
GDN_DK = 128
GDN_DV = 128
GDN_CONV = 5
ATTN_HEADS = 8
ATTN_KV_HEADS = 2
ATTN_GROUP = ATTN_HEADS // ATTN_KV_HEADS
ROPE_THETA = 10000.0
N_EXPERTS = 16
EC_FACTOR = 2
N_MOD = 6
DEEPNORM_ALPHA = (2 * DEPTH) ** 0.25

_SRC = dict(gla_q=(0, 256), gla_k=(256, 256), gla_v=(512, 512), gla_g=(1024, 512), gla_r=(1536, 32),
            gdn_q=(1568, 512), gdn_k=(2080, 512), gdn_v=(2592, 512), gdn_z=(3104, 512), gdn_b=(3616, 8),
            gdn_a=(3624, 8), att_q=(3632, 1024), att_k=(4656, 256), att_v=(4912, 256))
_MAIN_ORDER = ("att_q", "gla_v", "gdn_q", "gdn_k", "gdn_v", "gla_g", "gdn_z", "gla_q", "gla_k", "att_k", "att_v")
_SMALL_ORDER = ("gla_r", "gdn_b", "gdn_a")
_OFF = {}
_o = 0
for _n in _MAIN_ORDER:
    _OFF[_n] = _o
    _o += _SRC[_n][1]
MAIN_W = _o
_o = 0
for _n in _SMALL_ORDER:
    _OFF[_n] = _o
    _o += _SRC[_n][1]
SMALL_W = 128
MIX_HW = 512
SCAN_TB = 256

VMEM_LIMIT = 56 * 1024 * 1024

_NT = (((1,), (1,)), ((), ()))
_TN = (((0,), (0,)), ((), ()))


def _cparams(n_axes):
    return pltpu.CompilerParams(dimension_semantics=("arbitrary",) * n_axes, vmem_limit_bytes=VMEM_LIMIT)


def _split_bf16(a):
    hi = a.astype(BF16)
    lo = (a - hi.astype(F32)).astype(BF16)
    return hi, lo


def _dot(a, b):
    return jnp.dot(a, b, preferred_element_type=F32)


def _dot2(a, m):
    hi, lo = _split_bf16(a)
    return _dot(hi, m) + _dot(lo, m)


def _dot2_left(m, a):
    hi, lo = _split_bf16(a)
    return _dot(m, hi) + _dot(m, lo)


def _sigmoid(x):
    return 1.0 / (1.0 + jnp.exp(-x))


def _softplus(x):
    return jnp.maximum(x, 0.0) + jnp.log1p(jnp.exp(-jnp.abs(x)))


def _ada_kernel(c_ref, w_ref, b_ref, o_ref):
    c = c_ref[...]
    s = (c * _sigmoid(c)).astype(BF16)
    o_ref[...] = _dot(s, w_ref[...].astype(BF16)) + b_ref[...]


def _ada(cond8, w_ada, b_ada, tn=1024):
    depth, d, n = w_ada.shape
    return pl.pallas_call(
        _ada_kernel,
        out_shape=jax.ShapeDtypeStruct((depth, 8, n), F32),
        grid=(depth, n // tn),
        in_specs=[pl.BlockSpec((8, d), lambda l, j: (0, 0)),
                  pl.BlockSpec((None, d, tn), lambda l, j: (l, 0, j)),
                  pl.BlockSpec((None, 1, tn), lambda l, j: (l, 0, j))],
        out_specs=pl.BlockSpec((None, 8, tn), lambda l, j: (l, 0, j)),
        compiler_params=_cparams(2),
        name="ada",
    )(cond8, w_ada, b_ada.reshape(depth, 1, n))


def _inproj_kernel(x_ref, shift_ref, scale_ref, w_ref, wsh_ref, wsl_ref, o_ref, os_ref, h_ref):
    @pl.when(pl.program_id(1) == 0)
    def _():
        h = x_ref[...] * (1.0 + scale_ref[...]) + shift_ref[...]
        hi, lo = _split_bf16(h)
        h_ref[...] = hi
        wsh = wsh_ref[...]
        os_ref[...] = _dot(hi, wsh) + _dot(lo, wsh) + _dot(hi, wsl_ref[...])

    o_ref[...] = _dot(h_ref[...], w_ref[...])


def _inproj(x2d, mods, row_of_tile, w_main, ws_hi, ws_lo, tm, tn=1024):
    rows, d = x2d.shape
    n = w_main.shape[1]
    return pl.pallas_call(
        _inproj_kernel,
        out_shape=(jax.ShapeDtypeStruct((rows, n), F32), jax.ShapeDtypeStruct((rows, SMALL_W), F32)),
        grid=(rows // tm, n // tn),
        in_specs=[pl.BlockSpec((tm, d), lambda i, j: (i, 0)),
                  pl.BlockSpec((None, 1, d), lambda i, j: (row_of_tile(i) * N_MOD + 0, 0, 0)),
                  pl.BlockSpec((None, 1, d), lambda i, j: (row_of_tile(i) * N_MOD + 1, 0, 0)),
                  pl.BlockSpec((d, tn), lambda i, j: (0, j)),
                  pl.BlockSpec((d, SMALL_W), lambda i, j: (0, 0)),
                  pl.BlockSpec((d, SMALL_W), lambda i, j: (0, 0))],
        out_specs=(pl.BlockSpec((tm, tn), lambda i, j: (i, j)),
                   pl.BlockSpec((tm, SMALL_W), lambda i, j: (i, 0))),
        scratch_shapes=[pltpu.VMEM((tm, d), BF16)],
        compiler_params=_cparams(2),
        name="inproj",
    )(x2d, mods, mods, w_main, ws_hi, ws_lo)


def _rms_rope(x, gain, cos, sin):
    y = x * lax.rsqrt(jnp.mean(x * x, axis=-1, keepdims=True) + 1e-6) * gain
    if cos is not None:
        lane = lax.broadcasted_iota(jnp.int32, y.shape, 1)
        partner = jnp.where((lane % 64) < 32, pltpu.roll(y, 96, 1), pltpu.roll(y, 32, 1))
        y = y * cos + partner * sin
    return y


def _attn_kernel(*refs, seg_rope, q_rope, seg_len):
    n_seg = len(seg_len)
    q_ref = refs[0]
    kv_refs = refs[1:1 + 2 * n_seg]
    pos = 1 + 2 * n_seg
    gain_ref = refs[pos]
    pos += 1
    cq_ref = sq_ref = ck_ref = sk_ref = None
    if q_rope:
        cq_ref, sq_ref = refs[pos], refs[pos + 1]
        pos += 2
    if any(seg_rope):
        ck_ref, sk_ref = refs[pos], refs[pos + 1]
        pos += 2
    o_ref, k_scr, v_scr = refs[pos], refs[pos + 1], refs[pos + 2]
    tq = q_ref.shape[0]

    @pl.when(pl.program_id(2) == 0)
    def _():
        off = 0
        for s in range(n_seg):
            k = kv_refs[2 * s][...]
            cos = ck_ref[...] if seg_rope[s] else None
            sin = sk_ref[...] if seg_rope[s] else None
            k_scr[off:off + seg_len[s], :] = _rms_rope(k, gain_ref[1:2, :], cos, sin).astype(BF16)
            v_scr[off:off + seg_len[s], :] = kv_refs[2 * s + 1][...].astype(BF16)
            off += seg_len[s]

    q = q_ref[...]
    cq = cq_ref[...] if q_rope else None
    sq = sq_ref[...] if q_rope else None
    qs = []
    for g in range(ATTN_GROUP):
        qg = _rms_rope(q[:, g * HEAD_DIM:(g + 1) * HEAD_DIM], gain_ref[0:1, :], cq, sq)
        qs.append((qg * (HEAD_DIM ** -0.5)).astype(BF16))
    q4 = jnp.concatenate(qs, axis=0)
    s = lax.dot_general(q4, k_scr[...], _NT, preferred_element_type=F32)
    m = jnp.max(s, axis=-1, keepdims=True)
    p = jnp.exp(s - m)
    l = jnp.sum(p, axis=-1, keepdims=True)
    o = _dot(p.astype(BF16), v_scr[...]) / l
    o_ref[...] = jnp.concatenate([o[g * tq:(g + 1) * tq] for g in range(ATTN_GROUP)], axis=1).astype(o_ref.dtype)


def _attention(p_q, kv_segs, gain, rope_tabs, bsz, q_rope, tq):
    rows = p_q.shape[0]
    nq_rows = rows // bsz
    nq = nq_rows // tq
    qcol = _OFF["att_q"] // (ATTN_GROUP * HEAD_DIM)
    kcol = _OFF["att_k"] // HEAD_DIM
    vcol = _OFF["att_v"] // HEAD_DIM
    seg_len = tuple(n for _, n, _ in kv_segs)
    seg_rope = tuple(r for _, _, r in kv_segs)
    in_specs = [pl.BlockSpec((tq, ATTN_GROUP * HEAD_DIM), lambda b, h, i: (b * nq + i, qcol + h))]
    args = [p_q]
    for arr, n, _ in kv_segs:
        in_specs.append(pl.BlockSpec((n, HEAD_DIM), lambda b, h, i: (b, kcol + h)))
        in_specs.append(pl.BlockSpec((n, HEAD_DIM), lambda b, h, i: (b, vcol + h)))
        args += [arr, arr]
    in_specs.append(pl.BlockSpec((2, HEAD_DIM), lambda b, h, i: (0, 0)))
    args.append(gain)
    cos_t, sin_t = rope_tabs
    if q_rope:
        in_specs += [pl.BlockSpec((tq, HEAD_DIM), lambda b, h, i: (i, 0))] * 2
        args += [cos_t, sin_t]
    if any(seg_rope):
        n_r = [n for _, n, r in kv_segs if r][0]
        in_specs += [pl.BlockSpec((n_r, HEAD_DIM), lambda b, h, i: (0, 0))] * 2
        args += [cos_t, sin_t]
    nk = sum(seg_len)
    return pl.pallas_call(
        functools.partial(_attn_kernel, seg_rope=seg_rope, q_rope=q_rope, seg_len=seg_len),
        out_shape=jax.ShapeDtypeStruct((rows, ATTN_HEADS * HEAD_DIM), BF16),
        grid=(bsz, ATTN_KV_HEADS, nq),
        in_specs=in_specs,
        out_specs=pl.BlockSpec((tq, ATTN_GROUP * HEAD_DIM), lambda b, h, i: (b * nq + i, h)),
        scratch_shapes=[pltpu.VMEM((nk, HEAD_DIM), BF16), pltpu.VMEM((nk, HEAD_DIM), BF16)],
        compiler_params=_cparams(3),
        name="attention",
    )(*args)


def _chunk_masks(tb):
    i = lax.broadcasted_iota(jnp.int32, (tb, tb), 0)
    j = lax.broadcasted_iota(jnp.int32, (tb, tb), 1)
    same = (i // CHUNK) == (j // CHUNK)
    return i, j, same


def _gla_kernel(qk_f, v_f, s_f, qk_b, v_b, s_b, wup_ref, bup_ref, tri_ref, st0_ref,
                of_ref, ob_ref, stout_ref, st_scr):
    step = pl.program_id(1)

    @pl.when(step == 0)
    def _():
        st_scr[...] = st0_ref[...]

    tb = qk_f.shape[0]
    nch = tb // CHUNK
    hk = GLA_HEADS * GLA_DK
    lane = lax.broadcasted_iota(jnp.int32, (CHUNK, hk), 1)
    head_of_lane = lane // GLA_DK
    r4 = lax.broadcasted_iota(jnp.int32, (GLA_HEADS * CHUNK, CHUNK), 0) % CHUNK
    c4 = lax.broadcasted_iota(jnp.int32, (GLA_HEADS * CHUNK, CHUNK), 1)
    bd_r = lax.broadcasted_iota(jnp.int32, (MIX_HW, hk), 0) // GLA_DV
    bd_c = lax.broadcasted_iota(jnp.int32, (MIX_HW, hk), 1) // GLA_DK
    bd_mask = bd_r == bd_c

    for z, (qk_ref, v_ref, s_ref, o_ref) in enumerate(((qk_f, v_f, s_f, of_ref), (qk_b, v_b, s_b, ob_ref))):
        qk = qk_ref[...]
        q = qk[:, :hk] * (GLA_DK ** -0.5)
        k = qk[:, hk:]
        v = v_ref[...].astype(BF16)
        hi, lo = _split_bf16(s_ref[...])
        wh = wup_ref[z, 0]
        logit = _dot(hi, wh) + _dot(lo, wh) + _dot(hi, wup_ref[z, 1]) + bup_ref[z]
        log_a = -_softplus(-logit) / GLA_GATE_NORM
        cum = _dot2_left(tri_ref[z], log_a)
        intra_mask = (c4 <= r4) if z == 0 else (c4 >= r4)
        last = CHUNK - 1 if z == 0 else 0
        st = st_scr[z]
        outs = [None] * nch
        for c in (range(nch) if z == 0 else range(nch - 1, -1, -1)):
            r0 = c * CHUNK
            cum_c = cum[r0:r0 + CHUNK]
            tot = cum_c[last:last + 1, :]
            q_dec = q[r0:r0 + CHUNK] * jnp.exp(cum_c)
            k_c = k[r0:r0 + CHUNK]
            k_inv = (k_c * jnp.exp(-cum_c)).astype(BF16)
            k_dec = (k_c * jnp.exp(tot - cum_c)).astype(BF16)
            q4 = jnp.concatenate([jnp.where(head_of_lane == h, q_dec, 0.0) for h in range(GLA_HEADS)],
                                 axis=0).astype(BF16)
            a4 = lax.dot_general(q4, k_inv, _NT, preferred_element_type=F32)
            a4 = jnp.where(intra_mask, a4, 0.0).astype(BF16)
            v_c = v[r0:r0 + CHUNK]
            o4 = _dot(a4, v_c)
            o_intra = jnp.concatenate(
                [o4[h * CHUNK:(h + 1) * CHUNK, h * GLA_DV:(h + 1) * GLA_DV] for h in range(GLA_HEADS)], axis=1)
            o_inter = lax.dot_general(q_dec.astype(BF16), st.astype(BF16), _NT, preferred_element_type=F32)
            outs[c] = o_intra + o_inter
            kv_t = lax.dot_general(v_c, k_dec, _TN, preferred_element_type=F32)
            st = st * jnp.exp(tot) + jnp.where(bd_mask, kv_t, 0.0)
        o_ref[...] = jnp.concatenate(outs, axis=0)
        st_scr[z] = st

    @pl.when(step == pl.num_programs(1) - 1)
    def _():
        stout_ref[...] = st_scr[...]


def _gla(p, ps, t, bsz, wup, bup, tri, st0):
    tb = SCAN_TB
    nblk = t // tb
    qk_col = _OFF["gla_q"] // (2 * GLA_HEADS * GLA_DK)
    v_col = _OFF["gla_v"] // MIX_HW
    fwd = lambda b, i: b * nblk + i
    bwd = lambda b, i: b * nblk + (nblk - 1 - i)
    hk = GLA_HEADS * GLA_DK
    return pl.pallas_call(
        _gla_kernel,
        out_shape=(jax.ShapeDtypeStruct((bsz * t, MIX_HW), F32), jax.ShapeDtypeStruct((bsz * t, MIX_HW), F32),
                   jax.ShapeDtypeStruct((bsz, 2, MIX_HW, hk), F32)),
        grid=(bsz, nblk),
        in_specs=[pl.BlockSpec((tb, 2 * hk), lambda b, i: (fwd(b, i), qk_col)),
                  pl.BlockSpec((tb, MIX_HW), lambda b, i: (fwd(b, i), v_col)),
                  pl.BlockSpec((tb, SMALL_W), lambda b, i: (fwd(b, i), 0)),
                  pl.BlockSpec((tb, 2 * hk), lambda b, i: (bwd(b, i), qk_col)),
                  pl.BlockSpec((tb, MIX_HW), lambda b, i: (bwd(b, i), v_col)),
                  pl.BlockSpec((tb, SMALL_W), lambda b, i: (bwd(b, i), 0)),
                  pl.BlockSpec((2, 2, SMALL_W, hk), lambda b, i: (0, 0, 0, 0)),
                  pl.BlockSpec((2, 1, hk), lambda b, i: (0, 0, 0)),
                  pl.BlockSpec((2, tb, tb), lambda b, i: (0, 0, 0)),
                  pl.BlockSpec((None, 2, MIX_HW, hk), lambda b, i: (b, 0, 0, 0))],
        out_specs=(pl.BlockSpec((tb, MIX_HW), lambda b, i: (fwd(b, i), 0)),
                   pl.BlockSpec((tb, MIX_HW), lambda b, i: (bwd(b, i), 0)),
                   pl.BlockSpec((None, 2, MIX_HW, hk), lambda b, i: (b, 0, 0, 0))),
        scratch_shapes=[pltpu.VMEM((2, MIX_HW, hk), F32)],
        compiler_params=_cparams(2),
        name="gla_scan",
    )(p, p, ps, p, p, ps, wup, bup, tri, st0)


def _gdn_prep_kernel(x_ref, prev_ref, next_ref, w_ref, o_ref):
    i = pl.program_id(1)
    tb = x_ref.shape[0]
    halo = prev_ref.shape[0]
    prev = jnp.where(i > 0, prev_ref[...], 0.0)
    nxt = jnp.where(i < pl.num_programs(1) - 1, next_ref[...], 0.0)
    ext = jnp.concatenate([prev, x_ref[...], nxt], axis=0)
    w = w_ref[...]
    acc = None
    for j in range(GDN_CONV):
        shift = GDN_CONV // 2 - j
        rolled = ext if shift == 0 else pltpu.roll(ext, shift % (tb + 2 * halo), 0)
        term = rolled[halo:halo + tb] * w[j:j + 1, :]
        acc = term if acc is None else acc + term
    y = acc * _sigmoid(acc)
    qk_w = 2 * GDN_HEADS * GDN_DK
    outs = []
    for h in range(2 * GDN_HEADS):
        yh = y[:, h * GDN_DK:(h + 1) * GDN_DK]
        yh = yh * lax.rsqrt(jnp.sum(yh * yh, axis=-1, keepdims=True) + 1e-6)
        if h < GDN_HEADS:
            yh = yh * (GDN_DK ** -0.5)
        outs.append(yh)
    outs.append(y[:, qk_w:])
    o_ref[...] = jnp.concatenate(outs, axis=1)


def _gdn_prep(p, t, bsz, conv_w):
    tb = SCAN_TB
    halo = 8
    nblk = t // tb
    width = conv_w.shape[1]
    col = _OFF["gdn_q"] // width
    n_halo_blocks = bsz * t // halo
    per = tb // halo
    return pl.pallas_call(
        _gdn_prep_kernel,
        out_shape=jax.ShapeDtypeStruct((bsz * t, width), F32),
        grid=(bsz, nblk),
        in_specs=[pl.BlockSpec((tb, width), lambda b, i: (b * nblk + i, col)),
                  pl.BlockSpec((halo, width), lambda b, i: (jnp.maximum((b * nblk + i) * per - 1, 0), col)),
                  pl.BlockSpec((halo, width),
                               lambda b, i: (jnp.minimum((b * nblk + i + 1) * per, n_halo_blocks - 1), col)),
                  pl.BlockSpec((GDN_CONV, width), lambda b, i: (0, 0))],
        out_specs=pl.BlockSpec((tb, width), lambda b, i: (b * nblk + i, 0)),
        compiler_params=_cparams(2),
        name="gdn_conv",
    )(p, p, p, conv_w)


def _gdn_kernel(x_f, s_f, x_b, s_b, sel_ref, ealog_ref, dtb_ref, tri_ref, s0_ref,
                of_ref, ob_ref, sout_ref, s_scr):
    step = pl.program_id(1)

    @pl.when(step == 0)
    def _():
        s_scr[...] = s0_ref[...]

    tb = x_f.shape[0]
    nch = tb // CHUNK
    i_idx, j_idx, same = _chunk_masks(tb)
    xor = i_idx ^ j_idx
    eye = jnp.where(i_idx == j_idx, 1.0, 0.0)
    ones8 = jnp.ones((8, tb), BF16)
    hw = GDN_HEADS * GDN_DK

    for z, (x_ref, s_ref, o_ref) in enumerate(((x_f, s_f, of_ref), (x_b, s_b, ob_ref))):
        before = (j_idx <= i_idx) if z == 0 else (j_idx >= i_idx)
        incl = same & before
        incl_t = same & ((i_idx <= j_idx) if z == 0 else (i_idx >= j_idx))
        strict = incl & (i_idx != j_idx)
        sm = s_ref[...]
        beta_all = _sigmoid(sm)
        g_all = -ealog_ref[...] * _softplus(sm + dtb_ref[...])
        b_sel = _dot2(beta_all, sel_ref[2 * z])
        g_sel = _dot2(g_all, sel_ref[2 * z + 1])
        c_col = _dot2_left(tri_ref[z], g_sel)
        x = x_ref[...]
        last = CHUNK - 1 if z == 0 else 0
        head_outs = []
        for h in range(GDN_HEADS):
            q_h = x[:, h * GDN_DK:(h + 1) * GDN_DK]
            k_h = x[:, hw + h * GDN_DK:hw + (h + 1) * GDN_DK]
            v_h = x[:, 2 * hw + h * GDN_DV:2 * hw + (h + 1) * GDN_DV]
            bc = b_sel[:, h * GDN_DK:(h + 1) * GDN_DK]
            gs = g_sel[:, h * GDN_DK:(h + 1) * GDN_DK]
            cc = c_col[:, h * GDN_DK:(h + 1) * GDN_DK]
            kb = k_h * bc
            vb = (v_h * bc).astype(BF16)
            k_bf = k_h.astype(BF16)
            xg = jnp.where(incl_t, jnp.concatenate([gs] * (tb // GDN_DK), axis=1), 0.0)
            c_row = _dot2_left(ones8, xg)[0:1, :]
            dm = jnp.concatenate([cc] * (tb // GDN_DK), axis=1) - c_row
            gamma = jnp.where(incl, jnp.exp(jnp.where(incl, dm, 0.0)), 0.0)
            kk = lax.dot_general(kb.astype(BF16), k_bf, _NT, preferred_element_type=F32)
            l_mat = jnp.where(strict, kk * gamma, 0.0)
            t_inv = eye - jnp.where(xor == 1, l_mat, 0.0)
            s = 2
            while s < CHUNK:
                lvl = (xor >= s) & (xor < 2 * s)
                c_off = jnp.where(lvl, l_mat, 0.0).astype(BF16)
                t_bf = t_inv.astype(BF16)
                t_inv = t_inv - _dot(_dot(t_bf, c_off).astype(BF16), t_bf)
                s *= 2
            t_bf = t_inv.astype(BF16)
            u = _dot(t_bf, vb)
            w = _dot(t_bf, (kb * jnp.exp(cc)).astype(BF16))
            qk = lax.dot_general(q_h.astype(BF16), k_bf, _NT, preferred_element_type=F32)
            a_int = jnp.where(incl, qk * gamma, 0.0).astype(BF16)
            state = s_scr[z, h]
            outs = [None] * nch
            for c in (range(nch) if z == 0 else range(nch - 1, -1, -1)):
                r0 = c * CHUNK
                cc_c = cc[r0:r0 + CHUNK]
                g_last = cc_c[last:last + 1, :]
                k_dec = (k_h[r0:r0 + CHUNK] * jnp.exp(g_last - cc_c)).astype(BF16)
                q_dec = (q_h[r0:r0 + CHUNK] * jnp.exp(cc_c)).astype(BF16)
                s_bf = state.astype(BF16)
                v_new = u[r0:r0 + CHUNK] - _dot(w[r0:r0 + CHUNK].astype(BF16), s_bf)
                v_new_bf = v_new.astype(BF16)
                zeros = jnp.zeros((CHUNK, GDN_DV), BF16)
                v_pad = jnp.concatenate([v_new_bf if cc_i == c else zeros for cc_i in range(nch)], axis=0)
                outs[c] = _dot(q_dec, s_bf) + _dot(a_int[r0:r0 + CHUNK], v_pad)
                state = state * jnp.exp(g_last) + lax.dot_general(k_dec, v_new_bf, _TN, preferred_element_type=F32)
            s_scr[z, h] = state
            head_outs.append(jnp.concatenate(outs, axis=0))
        o_ref[...] = jnp.concatenate(head_outs, axis=1)

    @pl.when(step == pl.num_programs(1) - 1)
    def _():
        sout_ref[...] = s_scr[...]


def _gdn(xc, ps, t, bsz, sel, ealog, dtb, tri, s0):
    tb = SCAN_TB
    nblk = t // tb
    width = xc.shape[1]
    fwd = lambda b, i: b * nblk + i
    bwd = lambda b, i: b * nblk + (nblk - 1 - i)
    return pl.pallas_call(
        _gdn_kernel,
        out_shape=(jax.ShapeDtypeStruct((bsz * t, MIX_HW), F32), jax.ShapeDtypeStruct((bsz * t, MIX_HW), F32),
                   jax.ShapeDtypeStruct((bsz, 2, GDN_HEADS, GDN_DK, GDN_DV), F32)),
        grid=(bsz, nblk),
        in_specs=[pl.BlockSpec((tb, width), lambda b, i: (fwd(b, i), 0)),
                  pl.BlockSpec((tb, SMALL_W), lambda b, i: (fwd(b, i), 0)),
                  pl.BlockSpec((tb, width), lambda b, i: (bwd(b, i), 0)),
                  pl.BlockSpec((tb, SMALL_W), lambda b, i: (bwd(b, i), 0)),
                  pl.BlockSpec((4, SMALL_W, MIX_HW), lambda b, i: (0, 0, 0)),
                  pl.BlockSpec((1, SMALL_W), lambda b, i: (0, 0)),
                  pl.BlockSpec((1, SMALL_W), lambda b, i: (0, 0)),
                  pl.BlockSpec((2, tb, tb), lambda b, i: (0, 0, 0)),
                  pl.BlockSpec((None, 2, GDN_HEADS, GDN_DK, GDN_DV), lambda b, i: (b, 0, 0, 0, 0))],
        out_specs=(pl.BlockSpec((tb, MIX_HW), lambda b, i: (fwd(b, i), 0)),
                   pl.BlockSpec((tb, MIX_HW), lambda b, i: (bwd(b, i), 0)),
                   pl.BlockSpec((None, 2, GDN_HEADS, GDN_DK, GDN_DV), lambda b, i: (b, 0, 0, 0, 0))),
        scratch_shapes=[pltpu.VMEM((2, GDN_HEADS, GDN_DK, GDN_DV), F32)],
        compiler_params=_cparams(2),
        name="gdn_scan",
    )(xc, ps, xc, ps, sel, ealog, dtb, tri, s0)


def _layer_norm_rows(z, gain, bias):
    mu = jnp.mean(z, axis=-1, keepdims=True)
    zc = z - mu
    var = jnp.mean(zc * zc, axis=-1, keepdims=True)
    return zc * lax.rsqrt(var + 1e-5) * gain + bias


def _mixer_finish(o, gate, gain, n_heads, dv):
    outs = []
    for h in range(n_heads):
        oh = o[:, h * dv:(h + 1) * dv]
        oh = oh * lax.rsqrt(jnp.mean(oh * oh, axis=-1, keepdims=True) + 1e-6) * gain
        gh = gate[:, h * dv:(h + 1) * dv]
        outs.append((oh * (gh * _sigmoid(gh))).astype(BF16))
    return outs


def _outproj_kernel(glaf_ref, glab_ref, gdnf_ref, gdnb_ref, g_ref, z_ref, att_ref, x_ref, gate_ref, w_ref,
                    ngla_ref, ngdn_ref, gain_ref, bias_ref, o_ref):
    parts = _mixer_finish(glaf_ref[...] + glab_ref[...], g_ref[...], ngla_ref[...], GLA_HEADS, GLA_DV)
    parts += _mixer_finish(gdnf_ref[...] + gdnb_ref[...], z_ref[...], ngdn_ref[...], GDN_HEADS, GDN_DV)
    parts.append(att_ref[...])
    y = _dot(jnp.concatenate(parts, axis=1), w_ref[...])
    z = DEEPNORM_ALPHA * x_ref[...] + gate_ref[...] * y
    o_ref[...] = _layer_norm_rows(z, gain_ref[...], bias_ref[...])


def _outproj(gla_f, gla_b, gdn_f, gdn_b, p, att, x2d, mods, row_of_tile, w_out, n_gla, n_gdn, gain, bias, tm):
    rows, d = x2d.shape
    g_col = _OFF["gla_g"] // MIX_HW
    z_col = _OFF["gdn_z"] // MIX_HW
    mix = lambda: pl.BlockSpec((tm, MIX_HW), lambda i: (i, 0))
    vec = lambda n: pl.BlockSpec((1, n), lambda i: (0, 0))
    return pl.pallas_call(
        _outproj_kernel,
        out_shape=jax.ShapeDtypeStruct((rows, d), F32),
        grid=(rows // tm,),
        in_specs=[mix(), mix(), mix(), mix(),
                  pl.BlockSpec((tm, MIX_HW), lambda i: (i, g_col)),
                  pl.BlockSpec((tm, MIX_HW), lambda i: (i, z_col)),
                  pl.BlockSpec((tm, att.shape[1]), lambda i: (i, 0)),
                  pl.BlockSpec((tm, d), lambda i: (i, 0)),
                  pl.BlockSpec((None, 1, d), lambda i: (row_of_tile(i) * N_MOD + 2, 0, 0)),
                  pl.BlockSpec((w_out.shape[0], d), lambda i: (0, 0)),
                  vec(GLA_DV), vec(GDN_DV), vec(d), vec(d)],
        out_specs=pl.BlockSpec((tm, d), lambda i: (i, 0)),
        compiler_params=_cparams(1),
        name="outproj_ln",
    )(gla_f, gla_b, gdn_f, gdn_b, p, p, att, x2d, mods, w_out, n_gla.reshape(1, -1), n_gdn.reshape(1, -1),
      gain.reshape(1, d), bias.reshape(1, d))


def _router_kernel(x_ref, shift_ref, scale_ref, rh_ref, rl_ref, h_ref, lg_ref):
    h = x_ref[...] * (1.0 + scale_ref[...]) + shift_ref[...]
    hi, lo = _split_bf16(h)
    h_ref[...] = hi
    rh = rh_ref[...]
    lg_ref[...] = _dot(hi, rh) + _dot(lo, rh) + _dot(hi, rl_ref[...])


def _router(x2d, mods, row_of_tile, r_hi, r_lo, tm):
    rows, d = x2d.shape
    return pl.pallas_call(
        _router_kernel,
        out_shape=(jax.ShapeDtypeStruct((rows, d), BF16), jax.ShapeDtypeStruct((rows, 128), F32)),
        grid=(rows // tm,),
        in_specs=[pl.BlockSpec((tm, d), lambda i: (i, 0)),
                  pl.BlockSpec((None, 1, d), lambda i: (row_of_tile(i) * N_MOD + 3, 0, 0)),
                  pl.BlockSpec((None, 1, d), lambda i: (row_of_tile(i) * N_MOD + 4, 0, 0)),
                  pl.BlockSpec((d, 128), lambda i: (0, 0)),
                  pl.BlockSpec((d, 128), lambda i: (0, 0))],
        out_specs=(pl.BlockSpec((tm, d), lambda i: (i, 0)), pl.BlockSpec((tm, 128), lambda i: (i, 0))),
        compiler_params=_cparams(1),
        name="router",
    )(x2d, mods, mods, r_hi, r_lo)


def _ffn_kernel(x_ref, g_ref, w1_ref, w3_ref, w2_ref, o_ref):
    f = pl.program_id(1)
    w1 = w1_ref[...].astype(BF16)
    w3 = w3_ref[...].astype(BF16)
    w2 = w2_ref[...].astype(BF16)
    rc = x_ref.shape[0] // 2
    for r0 in (0, rc):
        x = x_ref[r0:r0 + rc, :]
        a = _dot(x, w1)
        u = _dot(x, w3)
        hmid = (a * _sigmoid(a) * u).astype(BF16)
        y = _dot(hmid, w2)

        @pl.when(f == 0)
        def _():
            o_ref[r0:r0 + rc, :] = y

        @pl.when(f > 0)
        def _():
            o_ref[r0:r0 + rc, :] += y

    @pl.when(f == pl.num_programs(1) - 1)
    def _():
        o_ref[...] = o_ref[...] * g_ref[...]


def _ffn(xs, gates, w1, w3, w2, tf=256):
    n_e, r, d = xs.shape
    ff = w1.shape[2]
    return pl.pallas_call(
        _ffn_kernel,
        out_shape=jax.ShapeDtypeStruct((n_e, r, d), F32),
        grid=(n_e, ff // tf),
        in_specs=[pl.BlockSpec((None, r, d), lambda e, f: (e, 0, 0)),
                  pl.BlockSpec((None, r, 1), lambda e, f: (e, 0, 0)),
                  pl.BlockSpec((None, d, tf), lambda e, f: (e, 0, f)),
                  pl.BlockSpec((None, d, tf), lambda e, f: (e, 0, f)),
                  pl.BlockSpec((None, tf, d), lambda e, f: (e, f, 0))],
        out_specs=pl.BlockSpec((None, r, d), lambda e, f: (e, 0, 0)),
        compiler_params=_cparams(2),
        name="expert_ffn",
    )(xs, gates, w1, w3, w2)


def _resln_kernel(x_ref, f_ref, gate_ref, gain_ref, bias_ref, o_ref):
    z = DEEPNORM_ALPHA * x_ref[...] + gate_ref[...] * f_ref[...]
    o_ref[...] = _layer_norm_rows(z, gain_ref[...], bias_ref[...])


def _resln(x2d, f2d, mods, row_of_tile, gain, bias, tm):
    rows, d = x2d.shape
    return pl.pallas_call(
        _resln_kernel,
        out_shape=jax.ShapeDtypeStruct((rows, d), F32),
        grid=(rows // tm,),
        in_specs=[pl.BlockSpec((tm, d), lambda i: (i, 0)),
                  pl.BlockSpec((tm, d), lambda i: (i, 0)),
                  pl.BlockSpec((None, 1, d), lambda i: (row_of_tile(i) * N_MOD + 5, 0, 0)),
                  pl.BlockSpec((1, d), lambda i: (0, 0)),
                  pl.BlockSpec((1, d), lambda i: (0, 0))],
        out_specs=pl.BlockSpec((tm, d), lambda i: (i, 0)),
        compiler_params=_cparams(1),
        name="residual_ln",
    )(x2d, f2d, mods, gain.reshape(1, d), bias.reshape(1, d))


def _rope_tables(n_tokens):
    rows = n_tokens // GRID_W
    row = jnp.broadcast_to(jnp.arange(rows)[:, None], (rows, GRID_W)).reshape(-1).astype(F32)
    col = jnp.broadcast_to(jnp.arange(GRID_W)[None, :], (rows, GRID_W)).reshape(-1).astype(F32)
    half = HEAD_DIM // 2
    inv = ROPE_THETA ** (-jnp.arange(0, half, 2, dtype=F32) / half)
    a_row = row[:, None] * inv
    a_col = col[:, None] * inv
    cos_t = jnp.concatenate([jnp.cos(a_row), jnp.cos(a_row), jnp.cos(a_col), jnp.cos(a_col)], axis=-1)
    sin_t = jnp.concatenate([-jnp.sin(a_row), jnp.sin(a_row), -jnp.sin(a_col), jnp.sin(a_col)], axis=-1)
    return cos_t, sin_t


def _permute_w_in(w):
    main = jnp.concatenate([w[:, _SRC[n][0]:_SRC[n][0] + _SRC[n][1]] for n in _MAIN_ORDER], axis=1)
    small = jnp.concatenate([w[:, _SRC[n][0]:_SRC[n][0] + _SRC[n][1]] for n in _SMALL_ORDER], axis=1)
    small = jnp.pad(small, ((0, 0), (0, SMALL_W - small.shape[1])))
    return main, small


def _scan_tri(tb):
    i = jnp.arange(tb)[:, None]
    j = jnp.arange(tb)[None, :]
    same = (i // CHUNK) == (j // CHUNK)
    return jnp.stack([same & (j <= i), same & (j >= i)]).astype(BF16)


def _gla_params(w_up, b_up):
    hk = GLA_HEADS * GLA_DK
    w = jnp.zeros((2, SMALL_W, hk), F32)
    for z in range(2):
        r0 = _OFF["gla_r"] + z * GLA_RANK
        w = w.at[z, r0:r0 + GLA_RANK].set(w_up[z])
    hi, lo = _split_bf16(w)
    return jnp.stack([hi, lo], axis=1), b_up.reshape(2, 1, hk)


def _gdn_params(a_log, dt_bias):
    sel = jnp.zeros((4, SMALL_W, MIX_HW), F32)
    ealog = jnp.zeros((1, SMALL_W), F32)
    dtb = jnp.zeros((1, SMALL_W), F32)
    for z in range(2):
        for h in range(GDN_HEADS):
            lane_b = _OFF["gdn_b"] + z * GDN_HEADS + h
            lane_a = _OFF["gdn_a"] + z * GDN_HEADS + h
            sel = sel.at[2 * z, lane_b, h * GDN_DK:(h + 1) * GDN_DK].set(1.0)
            sel = sel.at[2 * z + 1, lane_a, h * GDN_DK:(h + 1) * GDN_DK].set(1.0)
            ealog = ealog.at[0, lane_a].set(jnp.exp(a_log[z, h]))
            dtb = dtb.at[0, lane_a].set(dt_bias[z, h])
    return sel.astype(BF16), ealog, dtb


def _moe(x2d_sets, mods, row_fns, r_hi, r_lo, w1, w3, w2, gain, bias, bsz, tms):
    hs, gates, idxs, caps = [], [], [], []
    for x2d, row_fn, tm in zip(x2d_sets, row_fns, tms):
        h, lg = _router(x2d, mods, row_fn, r_hi, r_lo, tm)
        t = x2d.shape[0] // bsz
        cap = EC_FACTOR * t // N_EXPERTS
        aff = jax.nn.softmax(lg[:, :N_EXPERTS].reshape(bsz, t, N_EXPERTS), axis=-1).transpose(0, 2, 1)
        gate, idx = lax.top_k(aff, cap)
        hs.append(h.reshape(bsz, t, -1))
        gates.append(gate)
        idxs.append(idx)
        caps.append(cap)
    bidx = jnp.arange(bsz)[:, None, None]
    xs = jnp.concatenate([h[bidx, idx].transpose(1, 0, 2, 3).reshape(N_EXPERTS, bsz * cap, -1)
                          for h, idx, cap in zip(hs, idxs, caps)], axis=1)
    gs = jnp.concatenate([g.transpose(1, 0, 2).reshape(N_EXPERTS, bsz * cap) for g, cap in zip(gates, caps)], axis=1)
    y = _ffn(xs, gs[..., None], w1, w3, w2)
    outs = []
    off = 0
    for x2d, row_fn, tm, idx, cap in zip(x2d_sets, row_fns, tms, idxs, caps):
        t = x2d.shape[0] // bsz
        ys = y[:, off:off + bsz * cap].reshape(N_EXPERTS, bsz, cap, -1).transpose(1, 0, 2, 3)
        off += bsz * cap
        f = jnp.zeros((bsz, t, x2d.shape[1]), F32).at[bidx, idx].add(ys)
        outs.append(_resln(x2d, f.reshape(x2d.shape), mods, row_fn, gain, bias, tm))
    return outs


def kernel(x, c, ctx, c_ctx, w_ada, b_ada, w_in, w_out, gla_w_up, gla_b_up, gla_norm, gdn_conv, gdn_a_log,
           gdn_dt_bias, gdn_norm, attn_qk_norm, ln_gain, ln_bias, router, w1, w3, w2):
    bsz, n_lat, d = x.shape
    n_ctx = ctx.shape[1]
    depth = w_ada.shape[0]
    cos_t, sin_t = _rope_tables(n_lat)
    tri = _scan_tri(SCAN_TB)

    cond8 = jnp.zeros((8, d), F32).at[:bsz].set(c).at[bsz].set(c_ctx)
    mods_all = _ada(cond8, w_ada, b_ada)

    tm_lat, tm_ctx = 512, n_ctx
    lat_row = lambda i: i // (n_lat // tm_lat)
    ctx_row = lambda i: bsz + 0 * i
    tm_lat_ln = 256
    lat_row_ln = lambda i: i // (n_lat // tm_lat_ln)

    x_lat = x.reshape(bsz * n_lat, d)
    x_ctx = ctx.reshape(bsz * n_ctx, d)
    for l in range(depth):
        last = l == depth - 1
        mods = mods_all[l].reshape(8 * N_MOD, 1, d)
        w_main, w_small = _permute_w_in(w_in[l])
        w_main = w_main.astype(BF16)
        ws_hi, ws_lo = _split_bf16(w_small)

        p_lat, ps_lat = _inproj(x_lat, mods, lat_row, w_main, ws_hi, ws_lo, tm_lat)
        p_ctx, ps_ctx = _inproj(x_ctx, mods, ctx_row, w_main, ws_hi, ws_lo, tm_ctx)

        wup, bup = _gla_params(gla_w_up[l], gla_b_up[l])
        gla_s0 = jnp.zeros((bsz, 2, MIX_HW, GLA_HEADS * GLA_DK), F32)
        gla_cf, gla_cb, gla_s = _gla(p_ctx, ps_ctx, n_ctx, bsz, wup, bup, tri, gla_s0)
        gla_lf, gla_lb, _ = _gla(p_lat, ps_lat, n_lat, bsz, wup, bup, tri, gla_s)

        sel, ealog, dtb = _gdn_params(gdn_a_log[l], gdn_dt_bias[l])
        gdn_s0 = jnp.zeros((bsz, 2, GDN_HEADS, GDN_DK, GDN_DV), F32)
        xc_ctx = _gdn_prep(p_ctx, n_ctx, bsz, gdn_conv[l])
        xc_lat = _gdn_prep(p_lat, n_lat, bsz, gdn_conv[l])
        gdn_cf, gdn_cb, gdn_s = _gdn(xc_ctx, ps_ctx, n_ctx, bsz, sel, ealog, dtb, tri, gdn_s0)
        gdn_lf, gdn_lb, _ = _gdn(xc_lat, ps_lat, n_lat, bsz, sel, ealog, dtb, tri, gdn_s)

        att_l = _attention(p_lat, [(p_ctx, n_ctx, False), (p_lat, n_lat, True)], attn_qk_norm[l],
                           (cos_t, sin_t), bsz, True, 128)
        w_out_b = w_out[l].astype(BF16)
        x_lat = _outproj(gla_lf, gla_lb, gdn_lf, gdn_lb, p_lat, att_l, x_lat, mods, lat_row_ln, w_out_b,
                         gla_norm[l], gdn_norm[l], ln_gain[l, 0], ln_bias[l, 0], tm_lat_ln)
        r_hi, r_lo = _split_bf16(jnp.pad(router[l], ((0, 0), (0, 128 - N_EXPERTS))))
        if not last:
            att_c = _attention(p_ctx, [(p_ctx, n_ctx, False)], attn_qk_norm[l], (cos_t, sin_t), bsz, False, n_ctx)
            x_ctx = _outproj(gla_cf, gla_cb, gdn_cf, gdn_cb, p_ctx, att_c, x_ctx, mods, ctx_row, w_out_b,
                             gla_norm[l], gdn_norm[l], ln_gain[l, 0], ln_bias[l, 0], n_ctx)
            x_lat, x_ctx = _moe([x_lat, x_ctx], mods, [lat_row_ln, ctx_row], r_hi, r_lo, w1[l], w3[l], w2[l],
                                ln_gain[l, 1], ln_bias[l, 1], bsz, [tm_lat_ln, n_ctx])
        else:
            (x_lat,) = _moe([x_lat], mods, [lat_row_ln], r_hi, r_lo, w1[l], w3[l], w2[l],
                            ln_gain[l, 1], ln_bias[l, 1], bsz, [tm_lat_ln])
    return x_lat.reshape(bsz, n_lat, d)
```

```python
import functools

import jax
import jax.numpy as jnp
from jax import lax
from jax.experimental import pallas as pl
from jax.experimental.pallas import tpu as pltpu

F32 = jnp.float32
BF16 = jnp.bfloat16

D_MODEL = 2048
DEPTH = 2
GRID_W = 64
HEAD_DIM = 128
CHUNK = 64
GLA_HEADS = 4
GLA_DK = 64
GLA_DV = 128
GLA_RANK = 16
GLA_GATE_NORM = 16.0
GDN_HEADS = 4
GDN_DK = 128
GDN_DV = 128
GDN_CONV = 5
ATTN_HEADS = 8
ATTN_KV_HEADS = 2
ATTN_GROUP = ATTN_HEADS // ATTN_KV_HEADS
ROPE_THETA = 10000.0
N_EXPERTS = 16
EC_FACTOR = 2
N_MOD = 6
DEEPNORM_ALPHA = (2 * DEPTH) ** 0.25

_SRC = dict(gla_q=(0, 256), gla_k=(256, 256), gla_v=(512, 512), gla_g=(1024, 512), gla_r=(1536, 32),
            gdn_q=(1568, 512), gdn_k=(2080, 512), gdn_v=(2592, 512), gdn_z=(3104, 512), gdn_b=(3616, 8),
            gdn_a=(3624, 8), att_q=(3632, 1024), att_k=(4656, 256), att_v=(4912, 256))
_MAIN_ORDER = ("att_q", "gla_v", "gdn_q", "gdn_k", "gdn_v", "gla_g", "gdn_z", "gla_q", "gla_k", "att_k", "att_v")
_SMALL_ORDER = ("gla_r", "gdn_b", "gdn_a")
_OFF = {}
_o = 0
for _n in _MAIN_ORDER:
    _OFF[_n] = _o
    _o += _SRC[_n][1]
MAIN_W = _o
_o = 0
for _n in _SMALL_ORDER:
    _OFF[_n] = _o
    _o += _SRC[_n][1]
SMALL_W = 128
MIX_HW = 512
SCAN_TB = 256

VMEM_LIMIT = 56 * 1024 * 1024

_NT = (((1,), (1,)), ((), ()))
_TN = (((0,), (0,)), ((), ()))


def _cparams(n_axes):
    return pltpu.CompilerParams(dimension_semantics=("arbitrary",) * n_axes, vmem_limit_bytes=VMEM_LIMIT)


def _split_bf16(a):
    hi = a.astype(BF16)
    lo = (a - hi.astype(F32)).astype(BF16)
    return hi, lo


def _dot(a, b):
    return jnp.dot(a, b, preferred_element_type=F32)


def _dot2(a, m):
    hi, lo = _split_bf16(a)
    return _dot(hi, m) + _dot(lo, m)


def _dot2_left(m, a):
    hi, lo = _split_bf16(a)
    return _dot(m, hi) + _dot(m, lo)


def _sigmoid(x):
    return 1.0 / (1.0 + jnp.exp(-x))


def _softplus(x):
    return jnp.maximum(x, 0.0) + jnp.log1p(jnp.exp(-jnp.abs(x)))


def _ada_kernel(c_ref, w_ref, b_ref, o_ref):
    c = c_ref[...]
    s = (c * _sigmoid(c)).astype(BF16)
    o_ref[...] = _dot(s, w_ref[...].astype(BF16)) + b_ref[...]


def _ada(cond8, w_ada, b_ada, tn=1024):
    depth, d, n = w_ada.shape
    return pl.pallas_call(
        _ada_kernel,
        out_shape=jax.ShapeDtypeStruct((depth, 8, n), F32),
        grid=(depth, n // tn),
        in_specs=[pl.BlockSpec((8, d), lambda l, j: (0, 0)),
                  pl.BlockSpec((None, d, tn), lambda l, j: (l, 0, j)),
                  pl.BlockSpec((None, 1, tn), lambda l, j: (l, 0, j))],
        out_specs=pl.BlockSpec((None, 8, tn), lambda l, j: (l, 0, j)),
        compiler_params=_cparams(2),
        name="ada",
    )(cond8, w_ada, b_ada.reshape(depth, 1, n))


def _inproj_kernel(x_ref, shift_ref, scale_ref, w_ref, wsh_ref, wsl_ref, o_ref, os_ref, h_ref):
    @pl.when(pl.program_id(1) == 0)
    def _():
        h = x_ref[...] * (1.0 + scale_ref[...]) + shift_ref[...]
        hi, lo = _split_bf16(h)
        h_ref[...] = hi
        wsh = wsh_ref[...]
        os_ref[...] = _dot(hi, wsh) + _dot(lo, wsh) + _dot(hi, wsl_ref[...])

    o_ref[...] = _dot(h_ref[...], w_ref[...])


def _inproj(x2d, mods, row_of_tile, w_main, ws_hi, ws_lo, tm, tn=1024):
    rows, d = x2d.shape
    n = w_main.shape[1]
    return pl.pallas_call(
        _inproj_kernel,
        out_shape=(jax.ShapeDtypeStruct((rows, n), F32), jax.ShapeDtypeStruct((rows, SMALL_W), F32)),
        grid=(rows // tm, n // tn),
        in_specs=[pl.BlockSpec((tm, d), lambda i, j: (i, 0)),
                  pl.BlockSpec((None, 1, d), lambda i, j: (row_of_tile(i) * N_MOD + 0, 0, 0)),
                  pl.BlockSpec((None, 1, d), lambda i, j: (row_of_tile(i) * N_MOD + 1, 0, 0)),
                  pl.BlockSpec((d, tn), lambda i, j: (0, j)),
                  pl.BlockSpec((d, SMALL_W), lambda i, j: (0, 0)),
                  pl.BlockSpec((d, SMALL_W), lambda i, j: (0, 0))],
        out_specs=(pl.BlockSpec((tm, tn), lambda i, j: (i, j)),
                   pl.BlockSpec((tm, SMALL_W), lambda i, j: (i, 0))),
        scratch_shapes=[pltpu.VMEM((tm, d), BF16)],
        compiler_params=_cparams(2),
        name="inproj",
    )(x2d, mods, mods, w_main, ws_hi, ws_lo)


def _rms_rope(x, gain, cos, sin):
    y = x * lax.rsqrt(jnp.mean(x * x, axis=-1, keepdims=True) + 1e-6) * gain
    if cos is not None:
        lane = lax.broadcasted_iota(jnp.int32, y.shape, 1)
        partner = jnp.where((lane % 64) < 32, pltpu.roll(y, 96, 1), pltpu.roll(y, 32, 1))
        y = y * cos + partner * sin
    return y


KV_W = ATTN_KV_HEADS * HEAD_DIM + ATTN_KV_HEADS * 2 * HEAD_DIM
LOG2E = 1.4426950408889634


def _qkv_prep_kernel(*refs, rope):
    if rope:
        q_ref, kv_ref, gain_ref, cos_ref, sin_ref, qo_ref, kvo_ref = refs
        cos, sin = cos_ref[...], sin_ref[...]
    else:
        q_ref, kv_ref, gain_ref, qo_ref, kvo_ref = refs
        cos = sin = None
    q = q_ref[...]
    q_scale = (HEAD_DIM ** -0.5) * LOG2E
    qo_ref[...] = jnp.concatenate(
        [(_rms_rope(q[:, g * HEAD_DIM:(g + 1) * HEAD_DIM], gain_ref[0:1, :], cos, sin) * q_scale).astype(BF16)
         for g in range(ATTN_HEADS)], axis=1)
    kv = kv_ref[...]
    outs = [_rms_rope(kv[:, h * HEAD_DIM:(h + 1) * HEAD_DIM], gain_ref[1:2, :], cos, sin).astype(BF16)
            for h in range(ATTN_KV_HEADS)]
    ones = jnp.ones((kv.shape[0], HEAD_DIM), BF16)
    for h in range(ATTN_KV_HEADS):
        outs += [kv[:, (ATTN_KV_HEADS + h) * HEAD_DIM:(ATTN_KV_HEADS + h + 1) * HEAD_DIM].astype(BF16), ones]
    kvo_ref[...] = jnp.concatenate(outs, axis=1)


def _qkv_prep(p, gain, rope_tabs, n_per_sample, tm):
    rows = p.shape[0]
    qw = ATTN_HEADS * HEAD_DIM
    kvw = 2 * ATTN_KV_HEADS * HEAD_DIM
    per = n_per_sample // tm
    in_specs = [pl.BlockSpec((tm, qw), lambda i: (i, _OFF["att_q"] // qw)),
                pl.BlockSpec((tm, kvw), lambda i: (i, _OFF["att_k"] // kvw)),
                pl.BlockSpec((2, HEAD_DIM), lambda i: (0, 0))]
    args = [p, p, gain]
    if rope_tabs is not None:
        in_specs += [pl.BlockSpec((tm, HEAD_DIM), lambda i: (i % per, 0))] * 2
        args += list(rope_tabs)
    return pl.pallas_call(
        functools.partial(_qkv_prep_kernel, rope=rope_tabs is not None),
        out_shape=(jax.ShapeDtypeStruct((rows, qw), BF16), jax.ShapeDtypeStruct((rows, KV_W), BF16)),
        grid=(rows // tm,),
        in_specs=in_specs,
        out_specs=(pl.BlockSpec((tm, qw), lambda i: (i, 0)), pl.BlockSpec((tm, KV_W), lambda i: (i, 0))),
        compiler_params=_cparams(1),
        name="qkv_prep",
    )(*args)


def _attn_kernel(*refs, n_seg):
    q_ref = refs[0]
    k_refs = refs[1:1 + 2 * n_seg:2]
    v_refs = refs[2:2 + 2 * n_seg:2]
    o_ref = refs[1 + 2 * n_seg]
    outs = []
    for g in range(ATTN_GROUP):
        qg = q_ref[:, g * HEAD_DIM:(g + 1) * HEAD_DIM]
        s = jnp.concatenate([lax.dot_general(qg, k_ref[...], _NT, preferred_element_type=F32) for k_ref in k_refs],
                            axis=1)
        p = jnp.exp2(s - jnp.max(s, axis=-1, keepdims=True)).astype(BF16)
        acc = None
        off = 0
        for v_ref in v_refs:
            n = v_ref.shape[0]
            part = _dot(p[:, off:off + n], v_ref[...])
            acc = part if acc is None else acc + part
            off += n
        outs.append((acc[:, :HEAD_DIM] / acc[:, HEAD_DIM:]).astype(o_ref.dtype))
    o_ref[...] = jnp.concatenate(outs, axis=1)


def _attention(qn, kv_segs, bsz, tq):
    rows = qn.shape[0]
    nq = rows // bsz // tq
    gw = ATTN_GROUP * HEAD_DIM
    in_specs = [pl.BlockSpec((tq, gw), lambda b, h, i: (b * nq + i, h))]
    args = [qn]
    for arr, n in kv_segs:
        in_specs.append(pl.BlockSpec((n, HEAD_DIM), lambda b, h, i: (b, h)))
        in_specs.append(pl.BlockSpec((n, 2 * HEAD_DIM), lambda b, h, i: (b, 1 + h)))
        args += [arr, arr]
    return pl.pallas_call(
        functools.partial(_attn_kernel, n_seg=len(kv_segs)),
        out_shape=jax.ShapeDtypeStruct((rows, ATTN_HEADS * HEAD_DIM), BF16),
        grid=(bsz, ATTN_KV_HEADS, nq),
        in_specs=in_specs,
        out_specs=pl.BlockSpec((tq, gw), lambda b, h, i: (b * nq + i, h)),
        compiler_params=_cparams(3),
        name="attention",
    )(*args)


def _chunk_masks(tb):
    i = lax.broadcasted_iota(jnp.int32, (tb, tb), 0)
    j = lax.broadcasted_iota(jnp.int32, (tb, tb), 1)
    same = (i // CHUNK) == (j // CHUNK)
    return i, j, same


def _gla_kernel(qk_f, v_f, s_f, qk_b, v_b, s_b, wup_ref, bup_ref, tri_ref, st0_ref,
                of_ref, ob_ref, stout_ref, st_scr):
    step = pl.program_id(1)

    @pl.when(step == 0)
    def _():
        st_scr[...] = st0_ref[...]

    tb = qk_f.shape[0]
    nch = tb // CHUNK
    hk = GLA_HEADS * GLA_DK
    lane = lax.broadcasted_iota(jnp.int32, (CHUNK, hk), 1)
    head_of_lane = lane // GLA_DK
    r4 = lax.broadcasted_iota(jnp.int32, (GLA_HEADS * CHUNK, CHUNK), 0) % CHUNK
    c4 = lax.broadcasted_iota(jnp.int32, (GLA_HEADS * CHUNK, CHUNK), 1)
    bd_r = lax.broadcasted_iota(jnp.int32, (MIX_HW, hk), 0) // GLA_DV
    bd_c = lax.broadcasted_iota(jnp.int32, (MIX_HW, hk), 1) // GLA_DK
    bd_mask = bd_r == bd_c

    for z, (qk_ref, v_ref, s_ref, o_ref) in enumerate(((qk_f, v_f, s_f, of_ref), (qk_b, v_b, s_b, ob_ref))):
        qk = qk_ref[...]
        q = qk[:, :hk] * (GLA_DK ** -0.5)
        k = qk[:, hk:]
        v = v_ref[...].astype(BF16)
        hi, lo = _split_bf16(s_ref[...])
        wh = wup_ref[z, 0]
        logit = _dot(hi, wh) + _dot(lo, wh) + _dot(hi, wup_ref[z, 1]) + bup_ref[z]
        log_a = -_softplus(-logit) / GLA_GATE_NORM
        cum = _dot2_left(tri_ref[z], log_a)
        intra_mask = (c4 <= r4) if z == 0 else (c4 >= r4)
        last = CHUNK - 1 if z == 0 else 0
        st = st_scr[z]
        outs = [None] * nch
        for c in (range(nch) if z == 0 else range(nch - 1, -1, -1)):
            r0 = c * CHUNK
            cum_c = cum[r0:r0 + CHUNK]
            tot = cum_c[last:last + 1, :]
            q_dec = q[r0:r0 + CHUNK] * jnp.exp(cum_c)
            k_c = k[r0:r0 + CHUNK]
            k_inv = (k_c * jnp.exp(-cum_c)).astype(BF16)
            k_dec = (k_c * jnp.exp(tot - cum_c)).astype(BF16)
            q4 = jnp.concatenate([jnp.where(head_of_lane == h, q_dec, 0.0) for h in range(GLA_HEADS)],
                                 axis=0).astype(BF16)
            a4 = lax.dot_general(q4, k_inv, _NT, preferred_element_type=F32)
            a4 = jnp.where(intra_mask, a4, 0.0).astype(BF16)
            v_c = v[r0:r0 + CHUNK]
            o4 = _dot(a4, v_c)
            o_intra = jnp.concatenate(
                [o4[h * CHUNK:(h + 1) * CHUNK, h * GLA_DV:(h + 1) * GLA_DV] for h in range(GLA_HEADS)], axis=1)
            o_inter = lax.dot_general(q_dec.astype(BF16), st.astype(BF16), _NT, preferred_element_type=F32)
            outs[c] = o_intra + o_inter
            kv_t = lax.dot_general(v_c, k_dec, _TN, preferred_element_type=F32)
            st = st * jnp.exp(tot) + jnp.where(bd_mask, kv_t, 0.0)
        o_ref[...] = jnp.concatenate(outs, axis=0)
        st_scr[z] = st

    @pl.when(step == pl.num_programs(1) - 1)
    def _():
        stout_ref[...] = st_scr[...]


def _gla(p, ps, t, bsz, wup, bup, tri, st0):
    tb = SCAN_TB
    nblk = t // tb
    qk_col = _OFF["gla_q"] // (2 * GLA_HEADS * GLA_DK)
    v_col = _OFF["gla_v"] // MIX_HW
    fwd = lambda b, i: b * nblk + i
    bwd = lambda b, i: b * nblk + (nblk - 1 - i)
    hk = GLA_HEADS * GLA_DK
    return pl.pallas_call(
        _gla_kernel,
        out_shape=(jax.ShapeDtypeStruct((bsz * t, MIX_HW), F32), jax.ShapeDtypeStruct((bsz * t, MIX_HW), F32),
                   jax.ShapeDtypeStruct((bsz, 2, MIX_HW, hk), F32)),
        grid=(bsz, nblk),
        in_specs=[pl.BlockSpec((tb, 2 * hk), lambda b, i: (fwd(b, i), qk_col)),
                  pl.BlockSpec((tb, MIX_HW), lambda b, i: (fwd(b, i), v_col)),
                  pl.BlockSpec((tb, SMALL_W), lambda b, i: (fwd(b, i), 0)),
                  pl.BlockSpec((tb, 2 * hk), lambda b, i: (bwd(b, i), qk_col)),
                  pl.BlockSpec((tb, MIX_HW), lambda b, i: (bwd(b, i), v_col)),
                  pl.BlockSpec((tb, SMALL_W), lambda b, i: (bwd(b, i), 0)),
                  pl.BlockSpec((2, 2, SMALL_W, hk), lambda b, i: (0, 0, 0, 0)),
                  pl.BlockSpec((2, 1, hk), lambda b, i: (0, 0, 0)),
                  pl.BlockSpec((2, tb, tb), lambda b, i: (0, 0, 0)),
                  pl.BlockSpec((None, 2, MIX_HW, hk), lambda b, i: (b, 0, 0, 0))],
        out_specs=(pl.BlockSpec((tb, MIX_HW), lambda b, i: (fwd(b, i), 0)),
                   pl.BlockSpec((tb, MIX_HW), lambda b, i: (bwd(b, i), 0)),
                   pl.BlockSpec((None, 2, MIX_HW, hk), lambda b, i: (b, 0, 0, 0))),
        scratch_shapes=[pltpu.VMEM((2, MIX_HW, hk), F32)],
        compiler_params=_cparams(2),
        name="gla_scan",
    )(p, p, ps, p, p, ps, wup, bup, tri, st0)


def _gdn_prep_kernel(x_ref, prev_ref, next_ref, w_ref, o_ref):
    i = pl.program_id(1)
    tb = x_ref.shape[0]
    halo = prev_ref.shape[0]
    prev = jnp.where(i > 0, prev_ref[...], 0.0)
    nxt = jnp.where(i < pl.num_programs(1) - 1, next_ref[...], 0.0)
    ext = jnp.concatenate([prev, x_ref[...], nxt], axis=0)
    w = w_ref[...]
    acc = None
    for j in range(GDN_CONV):
        shift = GDN_CONV // 2 - j
        rolled = ext if shift == 0 else pltpu.roll(ext, shift % (tb + 2 * halo), 0)
        term = rolled[halo:halo + tb] * w[j:j + 1, :]
        acc = term if acc is None else acc + term
    y = acc * _sigmoid(acc)
    qk_w = 2 * GDN_HEADS * GDN_DK
    outs = []
    for h in range(2 * GDN_HEADS):
        yh = y[:, h * GDN_DK:(h + 1) * GDN_DK]
        yh = yh * lax.rsqrt(jnp.sum(yh * yh, axis=-1, keepdims=True) + 1e-6)
        if h < GDN_HEADS:
            yh = yh * (GDN_DK ** -0.5)
        outs.append(yh)
    outs.append(y[:, qk_w:])
    o_ref[...] = jnp.concatenate(outs, axis=1)


def _gdn_prep(p, t, bsz, conv_w):
    tb = SCAN_TB
    halo = 8
    nblk = t // tb
    width = conv_w.shape[1]
    col = _OFF["gdn_q"] // width
    n_halo_blocks = bsz * t // halo
    per = tb // halo
    return pl.pallas_call(
        _gdn_prep_kernel,
        out_shape=jax.ShapeDtypeStruct((bsz * t, width), F32),
        grid=(bsz, nblk),
        in_specs=[pl.BlockSpec((tb, width), lambda b, i: (b * nblk + i, col)),
                  pl.BlockSpec((halo, width), lambda b, i: (jnp.maximum((b * nblk + i) * per - 1, 0), col)),
                  pl.BlockSpec((halo, width),
                               lambda b, i: (jnp.minimum((b * nblk + i + 1) * per, n_halo_blocks - 1), col)),
                  pl.BlockSpec((GDN_CONV, width), lambda b, i: (0, 0))],
        out_specs=pl.BlockSpec((tb, width), lambda b, i: (b * nblk + i, 0)),
        compiler_params=_cparams(2),
        name="gdn_conv",
    )(p, p, p, conv_w)


def _gdn_kernel(x_f, s_f, x_b, s_b, sel_ref, ealog_ref, dtb_ref, tri_ref, s0_ref,
                of_ref, ob_ref, sout_ref, s_scr):
    step = pl.program_id(1)

    @pl.when(step == 0)
    def _():
        s_scr[...] = s0_ref[...]

    tb = x_f.shape[0]
    nch = tb // CHUNK
    i_idx, j_idx, same = _chunk_masks(tb)
    xor = i_idx ^ j_idx
    eye = jnp.where(i_idx == j_idx, 1.0, 0.0)
    hw = GDN_HEADS * GDN_DK

    for z, (x_ref, s_ref, o_ref) in enumerate(((x_f, s_f, of_ref), (x_b, s_b, ob_ref))):
        before = (j_idx <= i_idx) if z == 0 else (j_idx >= i_idx)
        incl = same & before
        strict = incl & (i_idx != j_idx)
        sm = s_ref[...]
        beta_all = _sigmoid(sm)
        g_all = -ealog_ref[...] * _softplus(sm + dtb_ref[...])
        b_sel = _dot2(beta_all, sel_ref[2 * z])
        g_sel = _dot2(g_all, sel_ref[2 * z + 1])
        c_col = _dot2_left(tri_ref[z], g_sel)
        x = x_ref[...]
        last = CHUNK - 1 if z == 0 else 0
        head_outs = []
        for h in range(GDN_HEADS):
            q_h = x[:, h * GDN_DK:(h + 1) * GDN_DK]
            k_h = x[:, hw + h * GDN_DK:hw + (h + 1) * GDN_DK]
            v_h = x[:, 2 * hw + h * GDN_DV:2 * hw + (h + 1) * GDN_DV]
            bc = b_sel[:, h * GDN_DK:(h + 1) * GDN_DK]
            cc = c_col[:, h * GDN_DK:(h + 1) * GDN_DK]
            kb = k_h * bc
            vb = (v_h * bc).astype(BF16)
            k_bf = k_h.astype(BF16)
            c_row = cc.T[0:1, :]
            dm = jnp.concatenate([cc] * (tb // GDN_DK), axis=1) - c_row
            gamma = jnp.where(incl, jnp.exp(jnp.where(incl, dm, 0.0)), 0.0)
            kk = lax.dot_general(kb.astype(BF16), k_bf, _NT, preferred_element_type=F32)
            l_mat = jnp.where(strict, kk * gamma, 0.0)
            t_inv = eye - jnp.where(xor == 1, l_mat, 0.0)
            s = 2
            while s < CHUNK:
                lvl = (xor >= s) & (xor < 2 * s)
                c_off = jnp.where(lvl, l_mat, 0.0).astype(BF16)
                t_bf = t_inv.astype(BF16)
                t_inv = t_inv - _dot(_dot(t_bf, c_off).astype(BF16), t_bf)
                s *= 2
            t_bf = t_inv.astype(BF16)
            u = _dot(t_bf, vb)
            w = _dot(t_bf, (kb * jnp.exp(cc)).astype(BF16))
            qk = lax.dot_general(q_h.astype(BF16), k_bf, _NT, preferred_element_type=F32)
            a_int = jnp.where(incl, qk * gamma, 0.0).astype(BF16)
            state = s_scr[z, h]
            outs = [None] * nch
            for c in (range(nch) if z == 0 else range(nch - 1, -1, -1)):
                r0 = c * CHUNK
                cc_c = cc[r0:r0 + CHUNK]
                g_last = cc_c[last:last + 1, :]
                k_dec = (k_h[r0:r0 + CHUNK] * jnp.exp(g_last - cc_c)).astype(BF16)
                q_dec = (q_h[r0:r0 + CHUNK] * jnp.exp(cc_c)).astype(BF16)
                s_bf = state.astype(BF16)
                v_new = u[r0:r0 + CHUNK] - _dot(w[r0:r0 + CHUNK].astype(BF16), s_bf)
                v_new_bf = v_new.astype(BF16)
                zeros = jnp.zeros((CHUNK, GDN_DV), BF16)
                v_pad = jnp.concatenate([v_new_bf if cc_i == c else zeros for cc_i in range(nch)], axis=0)
                outs[c] = _dot(q_dec, s_bf) + _dot(a_int[r0:r0 + CHUNK], v_pad)
                state = state * jnp.exp(g_last) + lax.dot_general(k_dec, v_new_bf, _TN, preferred_element_type=F32)
            s_scr[z, h] = state
            head_outs.append(jnp.concatenate(outs, axis=0))
        o_ref[...] = jnp.concatenate(head_outs, axis=1)

    @pl.when(step == pl.num_programs(1) - 1)
    def _():
        sout_ref[...] = s_scr[...]


def _gdn(xc, ps, t, bsz, sel, ealog, dtb, tri, s0):
    tb = SCAN_TB
    nblk = t // tb
    width = xc.shape[1]
    fwd = lambda b, i: b * nblk + i
    bwd = lambda b, i: b * nblk + (nblk - 1 - i)
    return pl.pallas_call(
        _gdn_kernel,
        out_shape=(jax.ShapeDtypeStruct((bsz * t, MIX_HW), F32), jax.ShapeDtypeStruct((bsz * t, MIX_HW), F32),
                   jax.ShapeDtypeStruct((bsz, 2, GDN_HEADS, GDN_DK, GDN_DV), F32)),
        grid=(bsz, nblk),
        in_specs=[pl.BlockSpec((tb, width), lambda b, i: (fwd(b, i), 0)),
                  pl.BlockSpec((tb, SMALL_W), lambda b, i: (fwd(b, i), 0)),
                  pl.BlockSpec((tb, width), lambda b, i: (bwd(b, i), 0)),
                  pl.BlockSpec((tb, SMALL_W), lambda b, i: (bwd(b, i), 0)),
                  pl.BlockSpec((4, SMALL_W, MIX_HW), lambda b, i: (0, 0, 0)),
                  pl.BlockSpec((1, SMALL_W), lambda b, i: (0, 0)),
                  pl.BlockSpec((1, SMALL_W), lambda b, i: (0, 0)),
                  pl.BlockSpec((2, tb, tb), lambda b, i: (0, 0, 0)),
                  pl.BlockSpec((None, 2, GDN_HEADS, GDN_DK, GDN_DV), lambda b, i: (b, 0, 0, 0, 0))],
        out_specs=(pl.BlockSpec((tb, MIX_HW), lambda b, i: (fwd(b, i), 0)),
                   pl.BlockSpec((tb, MIX_HW), lambda b, i: (bwd(b, i), 0)),
                   pl.BlockSpec((None, 2, GDN_HEADS, GDN_DK, GDN_DV), lambda b, i: (b, 0, 0, 0, 0))),
        scratch_shapes=[pltpu.VMEM((2, GDN_HEADS, GDN_DK, GDN_DV), F32)],
        compiler_params=_cparams(2),
        name="gdn_scan",
    )(xc, ps, xc, ps, sel, ealog, dtb, tri, s0)


def _layer_norm_rows(z, gain, bias):
    mu = jnp.mean(z, axis=-1, keepdims=True)
    zc = z - mu
    var = jnp.mean(zc * zc, axis=-1, keepdims=True)
    return zc * lax.rsqrt(var + 1e-5) * gain + bias


def _mixer_finish(o, gate, gain, n_heads, dv):
    outs = []
    for h in range(n_heads):
        oh = o[:, h * dv:(h + 1) * dv]
        oh = oh * lax.rsqrt(jnp.mean(oh * oh, axis=-1, keepdims=True) + 1e-6) * gain
        gh = gate[:, h * dv:(h + 1) * dv]
        outs.append((oh * (gh * _sigmoid(gh))).astype(BF16))
    return outs


def _outproj_kernel(glaf_ref, glab_ref, gdnf_ref, gdnb_ref, g_ref, z_ref, att_ref, x_ref, gate_ref, w_ref,
                    ngla_ref, ngdn_ref, gain_ref, bias_ref, o_ref):
    parts = _mixer_finish(glaf_ref[...] + glab_ref[...], g_ref[...], ngla_ref[...], GLA_HEADS, GLA_DV)
    parts += _mixer_finish(gdnf_ref[...] + gdnb_ref[...], z_ref[...], ngdn_ref[...], GDN_HEADS, GDN_DV)
    parts.append(att_ref[...])
    y = _dot(jnp.concatenate(parts, axis=1), w_ref[...])
    z = DEEPNORM_ALPHA * x_ref[...] + gate_ref[...] * y
    o_ref[...] = _layer_norm_rows(z, gain_ref[...], bias_ref[...])


def _outproj(gla_f, gla_b, gdn_f, gdn_b, p, att, x2d, mods, row_of_tile, w_out, n_gla, n_gdn, gain, bias, tm):
    rows, d = x2d.shape
    g_col = _OFF["gla_g"] // MIX_HW
    z_col = _OFF["gdn_z"] // MIX_HW
    mix = lambda: pl.BlockSpec((tm, MIX_HW), lambda i: (i, 0))
    vec = lambda n: pl.BlockSpec((1, n), lambda i: (0, 0))
    return pl.pallas_call(
        _outproj_kernel,
        out_shape=jax.ShapeDtypeStruct((rows, d), F32),
        grid=(rows // tm,),
        in_specs=[mix(), mix(), mix(), mix(),
                  pl.BlockSpec((tm, MIX_HW), lambda i: (i, g_col)),
                  pl.BlockSpec((tm, MIX_HW), lambda i: (i, z_col)),
                  pl.BlockSpec((tm, att.shape[1]), lambda i: (i, 0)),
                  pl.BlockSpec((tm, d), lambda i: (i, 0)),
                  pl.BlockSpec((None, 1, d), lambda i: (row_of_tile(i) * N_MOD + 2, 0, 0)),
                  pl.BlockSpec((w_out.shape[0], d), lambda i: (0, 0)),
                  vec(GLA_DV), vec(GDN_DV), vec(d), vec(d)],
        out_specs=pl.BlockSpec((tm, d), lambda i: (i, 0)),
        compiler_params=_cparams(1),
        name="outproj_ln",
    )(gla_f, gla_b, gdn_f, gdn_b, p, p, att, x2d, mods, w_out, n_gla.reshape(1, -1), n_gdn.reshape(1, -1),
      gain.reshape(1, d), bias.reshape(1, d))


def _router_kernel(x_ref, shift_ref, scale_ref, rh_ref, rl_ref, h_ref, lg_ref):
    h = x_ref[...] * (1.0 + scale_ref[...]) + shift_ref[...]
    hi, lo = _split_bf16(h)
    h_ref[...] = hi
    rh = rh_ref[...]
    lg_ref[...] = _dot(hi, rh) + _dot(lo, rh) + _dot(hi, rl_ref[...])


def _router(x2d, mods, row_of_tile, r_hi, r_lo, tm):
    rows, d = x2d.shape
    return pl.pallas_call(
        _router_kernel,
        out_shape=(jax.ShapeDtypeStruct((rows, d), BF16), jax.ShapeDtypeStruct((rows, 128), F32)),
        grid=(rows // tm,),
        in_specs=[pl.BlockSpec((tm, d), lambda i: (i, 0)),
                  pl.BlockSpec((None, 1, d), lambda i: (row_of_tile(i) * N_MOD + 3, 0, 0)),
                  pl.BlockSpec((None, 1, d), lambda i: (row_of_tile(i) * N_MOD + 4, 0, 0)),
                  pl.BlockSpec((d, 128), lambda i: (0, 0)),
                  pl.BlockSpec((d, 128), lambda i: (0, 0))],
        out_specs=(pl.BlockSpec((tm, d), lambda i: (i, 0)), pl.BlockSpec((tm, 128), lambda i: (i, 0))),
        compiler_params=_cparams(1),
        name="router",
    )(x2d, mods, mods, r_hi, r_lo)


def _ffn_kernel(x_ref, g_ref, w1_ref, w3_ref, w2_ref, o_ref, h_scr, *, nf):
    s = pl.program_id(1)
    tf = w1_ref.shape[1]

    @pl.when(s < nf)
    def _():
        x = x_ref[...]
        a = _dot(x, w1_ref[...].astype(BF16))
        u = _dot(x, w3_ref[...].astype(BF16))
        hmid = (a * _sigmoid(a) * u).astype(BF16)
        for f in range(nf):
            @pl.when(s == f)
            def _():
                h_scr[:, f * tf:(f + 1) * tf] = hmid

    @pl.when(s >= nf)
    def _():
        o_ref[...] = _dot(h_scr[...], w2_ref[...].astype(BF16)) * g_ref[...]


def _ffn(xs, gates, w1, w3, w2, tf=512, td=512):
    n_e, r, d = xs.shape
    ff = w1.shape[2]
    nf = ff // tf
    up = lambda e, s: (e, 0, jnp.minimum(s, nf - 1))
    down = lambda e, s: (e, 0, jnp.maximum(s - nf, 0))
    return pl.pallas_call(
        functools.partial(_ffn_kernel, nf=nf),
        out_shape=jax.ShapeDtypeStruct((n_e, r, d), F32),
        grid=(n_e, nf + d // td),
        in_specs=[pl.BlockSpec((None, r, d), lambda e, s: (e, 0, 0)),
                  pl.BlockSpec((None, r, 1), lambda e, s: (e, 0, 0)),
                  pl.BlockSpec((None, d, tf), up),
                  pl.BlockSpec((None, d, tf), up),
                  pl.BlockSpec((None, ff, td), down)],
        out_specs=pl.BlockSpec((None, r, td), down),
        scratch_shapes=[pltpu.VMEM((r, ff), BF16)],
        compiler_params=_cparams(2),
        name="expert_ffn",
    )(xs, gates, w1, w3, w2)


def _resln_kernel(x_ref, f_ref, gate_ref, gain_ref, bias_ref, o_ref):
    z = DEEPNORM_ALPHA * x_ref[...] + gate_ref[...] * f_ref[...]
    o_ref[...] = _layer_norm_rows(z, gain_ref[...], bias_ref[...])


def _resln(x2d, f2d, f_row0, mods, row_of_tile, gain, bias, tm):
    rows, d = x2d.shape
    f_blk0 = f_row0 // tm
    return pl.pallas_call(
        _resln_kernel,
        out_shape=jax.ShapeDtypeStruct((rows, d), F32),
        grid=(rows // tm,),
        in_specs=[pl.BlockSpec((tm, d), lambda i: (i, 0)),
                  pl.BlockSpec((tm, d), lambda i: (i + f_blk0, 0)),
                  pl.BlockSpec((None, 1, d), lambda i: (row_of_tile(i) * N_MOD + 5, 0, 0)),
                  pl.BlockSpec((1, d), lambda i: (0, 0)),
                  pl.BlockSpec((1, d), lambda i: (0, 0))],
        out_specs=pl.BlockSpec((tm, d), lambda i: (i, 0)),
        compiler_params=_cparams(1),
        name="residual_ln",
    )(x2d, f2d, mods, gain.reshape(1, d), bias.reshape(1, d))


def _rope_tables(n_tokens):
    rows = n_tokens // GRID_W
    row = jnp.broadcast_to(jnp.arange(rows)[:, None], (rows, GRID_W)).reshape(-1).astype(F32)
    col = jnp.broadcast_to(jnp.arange(GRID_W)[None, :], (rows, GRID_W)).reshape(-1).astype(F32)
    half = HEAD_DIM // 2
    inv = ROPE_THETA ** (-jnp.arange(0, half, 2, dtype=F32) / half)
    a_row = row[:, None] * inv
    a_col = col[:, None] * inv
    cos_t = jnp.concatenate([jnp.cos(a_row), jnp.cos(a_row), jnp.cos(a_col), jnp.cos(a_col)], axis=-1)
    sin_t = jnp.concatenate([-jnp.sin(a_row), jnp.sin(a_row), -jnp.sin(a_col), jnp.sin(a_col)], axis=-1)
    return cos_t, sin_t


def _permute_w_in(w):
    main = jnp.concatenate([w[:, _SRC[n][0]:_SRC[n][0] + _SRC[n][1]] for n in _MAIN_ORDER], axis=1)
    small = jnp.concatenate([w[:, _SRC[n][0]:_SRC[n][0] + _SRC[n][1]] for n in _SMALL_ORDER], axis=1)
    small = jnp.pad(small, ((0, 0), (0, SMALL_W - small.shape[1])))
    return main, small


def _scan_tri(tb):
    i = jnp.arange(tb)[:, None]
    j = jnp.arange(tb)[None, :]
    same = (i // CHUNK) == (j // CHUNK)
    return jnp.stack([same & (j <= i), same & (j >= i)]).astype(BF16)


def _gla_params(w_up, b_up):
    hk = GLA_HEADS * GLA_DK
    w = jnp.zeros((2, SMALL_W, hk), F32)
    for z in range(2):
        r0 = _OFF["gla_r"] + z * GLA_RANK
        w = w.at[z, r0:r0 + GLA_RANK].set(w_up[z])
    hi, lo = _split_bf16(w)
    return jnp.stack([hi, lo], axis=1), b_up.reshape(2, 1, hk)


def _gdn_params(a_log, dt_bias):
    sel = jnp.zeros((4, SMALL_W, MIX_HW), F32)
    ealog = jnp.zeros((1, SMALL_W), F32)
    dtb = jnp.zeros((1, SMALL_W), F32)
    for z in range(2):
        for h in range(GDN_HEADS):
            lane_b = _OFF["gdn_b"] + z * GDN_HEADS + h
            lane_a = _OFF["gdn_a"] + z * GDN_HEADS + h
            sel = sel.at[2 * z, lane_b, h * GDN_DK:(h + 1) * GDN_DK].set(1.0)
            sel = sel.at[2 * z + 1, lane_a, h * GDN_DK:(h + 1) * GDN_DK].set(1.0)
            ealog = ealog.at[0, lane_a].set(jnp.exp(a_log[z, h]))
            dtb = dtb.at[0, lane_a].set(dt_bias[z, h])
    return sel.astype(BF16), ealog, dtb


def _moe(x2d_sets, mods, row_fns, r_hi, r_lo, w1, w3, w2, gain, bias, bsz, tms):
    hs, gates, flat_idx, row0s = [], [], [], []
    row0 = 0
    for x2d, row_fn, tm in zip(x2d_sets, row_fns, tms):
        h, lg = _router(x2d, mods, row_fn, r_hi, r_lo, tm)
        t = x2d.shape[0] // bsz
        cap = EC_FACTOR * t // N_EXPERTS
        aff = jax.nn.softmax(lg[:, :N_EXPERTS].reshape(bsz, t, N_EXPERTS), axis=-1).transpose(0, 2, 1)
        gate, idx = lax.top_k(aff, cap)
        idx = idx + (row0 + jnp.arange(bsz) * t)[:, None, None]
        hs.append(h)
        gates.append(gate.transpose(1, 0, 2).reshape(N_EXPERTS, bsz * cap))
        flat_idx.append(idx.transpose(1, 0, 2).reshape(N_EXPERTS, bsz * cap))
        row0s.append(row0)
        row0 += x2d.shape[0]
    h_all = jnp.concatenate(hs, axis=0) if len(hs) > 1 else hs[0]
    fi = jnp.concatenate(flat_idx, axis=1) if len(hs) > 1 else flat_idx[0]
    gs = jnp.concatenate(gates, axis=1) if len(hs) > 1 else gates[0]
    xs = h_all[fi]
    y = _ffn(xs, gs[..., None], w1, w3, w2)
    f_all = jnp.zeros((row0, h_all.shape[1]), F32).at[fi.reshape(-1)].add(y.reshape(-1, y.shape[-1]))
    return [_resln(x2d, f_all, r0, mods, row_fn, gain, bias, tm)
            for x2d, row_fn, tm, r0 in zip(x2d_sets, row_fns, tms, row0s)]


def kernel(x, c, ctx, c_ctx, w_ada, b_ada, w_in, w_out, gla_w_up, gla_b_up, gla_norm, gdn_conv, gdn_a_log,
           gdn_dt_bias, gdn_norm, attn_qk_norm, ln_gain, ln_bias, router, w1, w3, w2):
    bsz, n_lat, d = x.shape
    n_ctx = ctx.shape[1]
    depth = w_ada.shape[0]
    cos_t, sin_t = _rope_tables(n_lat)
    tri = _scan_tri(SCAN_TB)

    cond8 = jnp.zeros((8, d), F32).at[:bsz].set(c).at[bsz].set(c_ctx)
    mods_all = _ada(cond8, w_ada, b_ada)

    tm_lat, tm_ctx = 512, n_ctx
    lat_row = lambda i: i // (n_lat // tm_lat)
    ctx_row = lambda i: bsz + 0 * i
    tm_lat_ln = 256
    lat_row_ln = lambda i: i // (n_lat // tm_lat_ln)

    x_lat = x.reshape(bsz * n_lat, d)
    x_ctx = ctx.reshape(bsz * n_ctx, d)
    for l in range(depth):
        last = l == depth - 1
        mods = mods_all[l].reshape(8 * N_MOD, 1, d)
        w_main, w_small = _permute_w_in(w_in[l])
        w_main = w_main.astype(BF16)
        ws_hi, ws_lo = _split_bf16(w_small)

        p_lat, ps_lat = _inproj(x_lat, mods, lat_row, w_main, ws_hi, ws_lo, tm_lat)
        p_ctx, ps_ctx = _inproj(x_ctx, mods, ctx_row, w_main, ws_hi, ws_lo, tm_ctx)

        wup, bup = _gla_params(gla_w_up[l], gla_b_up[l])
        gla_s0 = jnp.zeros((bsz, 2, MIX_HW, GLA_HEADS * GLA_DK), F32)
        gla_cf, gla_cb, gla_s = _gla(p_ctx, ps_ctx, n_ctx, bsz, wup, bup, tri, gla_s0)
        gla_lf, gla_lb, _ = _gla(p_lat, ps_lat, n_lat, bsz, wup, bup, tri, gla_s)

        sel, ealog, dtb = _gdn_params(gdn_a_log[l], gdn_dt_bias[l])
        gdn_s0 = jnp.zeros((bsz, 2, GDN_HEADS, GDN_DK, GDN_DV), F32)
        xc_ctx = _gdn_prep(p_ctx, n_ctx, bsz, gdn_conv[l])
        xc_lat = _gdn_prep(p_lat, n_lat, bsz, gdn_conv[l])
        gdn_cf, gdn_cb, gdn_s = _gdn(xc_ctx, ps_ctx, n_ctx, bsz, sel, ealog, dtb, tri, gdn_s0)
        gdn_lf, gdn_lb, _ = _gdn(xc_lat, ps_lat, n_lat, bsz, sel, ealog, dtb, tri, gdn_s)

        qn_lat, kv_lat = _qkv_prep(p_lat, attn_qk_norm[l], (cos_t, sin_t), n_lat, 512)
        qn_ctx, kv_ctx = _qkv_prep(p_ctx, attn_qk_norm[l], None, n_ctx, n_ctx)
        att_l = _attention(qn_lat, [(kv_ctx, n_ctx), (kv_lat, n_lat)], bsz, 256)
        w_out_b = w_out[l].astype(BF16)
        x_lat = _outproj(gla_lf, gla_lb, gdn_lf, gdn_lb, p_lat, att_l, x_lat, mods, lat_row_ln, w_out_b,
                         gla_norm[l], gdn_norm[l], ln_gain[l, 0], ln_bias[l, 0], tm_lat_ln)
        r_hi, r_lo = _split_bf16(jnp.pad(router[l], ((0, 0), (0, 128 - N_EXPERTS))))
        if not last:
            att_c = _attention(qn_ctx, [(kv_ctx, n_ctx)], bsz, n_ctx)
            x_ctx = _outproj(gla_cf, gla_cb, gdn_cf, gdn_cb, p_ctx, att_c, x_ctx, mods, ctx_row, w_out_b,
                             gla_norm[l], gdn_norm[l], ln_gain[l, 0], ln_bias[l, 0], n_ctx)
            x_lat, x_ctx = _moe([x_lat, x_ctx], mods, [lat_row_ln, ctx_row], r_hi, r_lo, w1[l], w3[l], w2[l],
                                ln_gain[l, 1], ln_bias[l, 1], bsz, [tm_lat_ln, n_ctx])
        else:
            (x_lat,) = _moe([x_lat], mods, [lat_row_ln], r_hi, r_lo, w1[l], w3[l], w2[l],
                            ln_gain[l, 1], ln_bias[l, 1], bsz, [tm_lat_ln])
    return x_lat.reshape(bsz, n_lat, d)
```

```python
import functools

import jax
import jax.numpy as jnp
from jax import lax
from jax.experimental import pallas as pl
from jax.experimental.pallas import tpu as pltpu

F32 = jnp.float32
BF16 = jnp.bfloat16

D_MODEL = 2048
DEPTH = 2
GRID_W = 64
HEAD_DIM = 128
CHUNK = 64
GLA_HEADS = 4
GLA_DK = 64
GLA_DV = 128
GLA_RANK = 16
GLA_GATE_NORM = 16.0
GDN_HEADS = 4
GDN_DK = 128
GDN_DV = 128
GDN_CONV = 5
ATTN_HEADS = 8
ATTN_KV_HEADS = 2
ATTN_GROUP = ATTN_HEADS // ATTN_KV_HEADS
ROPE_THETA = 10000.0
N_EXPERTS = 16
EC_FACTOR = 2
N_MOD = 6
DEEPNORM_ALPHA = (2 * DEPTH) ** 0.25

_SRC = dict(gla_q=(0, 256), gla_k=(256, 256), gla_v=(512, 512), gla_g=(1024, 512), gla_r=(1536, 32),
            gdn_q=(1568, 512), gdn_k=(2080, 512), gdn_v=(2592, 512), gdn_z=(3104, 512), gdn_b=(3616, 8),
            gdn_a=(3624, 8), att_q=(3632, 1024), att_k=(4656, 256), att_v=(4912, 256))
_MAIN_ORDER = ("att_q", "gla_v", "gdn_q", "gdn_k", "gdn_v", "gla_g", "gdn_z", "gla_q", "gla_k", "att_k", "att_v")
_SMALL_ORDER = ("gla_r", "gdn_b", "gdn_a")
_OFF = {}
_o = 0
for _n in _MAIN_ORDER:
    _OFF[_n] = _o
    _o += _SRC[_n][1]
MAIN_W = _o
_o = 0
for _n in _SMALL_ORDER:
    _OFF[_n] = _o
    _o += _SRC[_n][1]
SMALL_W = 128
MIX_HW = 512
SCAN_TB = 256

VMEM_LIMIT = 56 * 1024 * 1024

_NT = (((1,), (1,)), ((), ()))
_TN = (((0,), (0,)), ((), ()))


def _cparams(n_axes):
    return pltpu.CompilerParams(dimension_semantics=("arbitrary",) * n_axes, vmem_limit_bytes=VMEM_LIMIT)


def _split_bf16(a):
    hi = a.astype(BF16)
    lo = (a - hi.astype(F32)).astype(BF16)
    return hi, lo


def _dot(a, b):
    return jnp.dot(a, b, preferred_element_type=F32)


def _dot2(a, m):
    hi, lo = _split_bf16(a)
    return _dot(hi, m) + _dot(lo, m)


def _dot2_left(m, a):
    hi, lo = _split_bf16(a)
    return _dot(m, hi) + _dot(m, lo)


def _sigmoid(x):
    return 1.0 / (1.0 + jnp.exp(-x))


def _softplus(x):
    return jnp.maximum(x, 0.0) + jnp.log1p(jnp.exp(-jnp.abs(x)))


def _ada_kernel(c_ref, w_ref, b_ref, o_ref):
    c = c_ref[...]
    s = (c * _sigmoid(c)).astype(BF16)
    o_ref[...] = _dot(s, w_ref[...].astype(BF16)) + b_ref[...]


def _ada(cond8, w_ada, b_ada, tn=1024):
    depth, d, n = w_ada.shape
    return pl.pallas_call(
        _ada_kernel,
        out_shape=jax.ShapeDtypeStruct((depth, 8, n), F32),
        grid=(depth, n // tn),
        in_specs=[pl.BlockSpec((8, d), lambda l, j: (0, 0)),
                  pl.BlockSpec((None, d, tn), lambda l, j: (l, 0, j)),
                  pl.BlockSpec((None, 1, tn), lambda l, j: (l, 0, j))],
        out_specs=pl.BlockSpec((None, 8, tn), lambda l, j: (l, 0, j)),
        compiler_params=_cparams(2),
        name="ada",
    )(cond8, w_ada, b_ada.reshape(depth, 1, n))


def _inproj_kernel(x_ref, shift_ref, scale_ref, w_ref, wsh_ref, wsl_ref, o_ref, os_ref, h_ref):
    @pl.when(pl.program_id(1) == 0)
    def _():
        h = x_ref[...] * (1.0 + scale_ref[...]) + shift_ref[...]
        hi, lo = _split_bf16(h)
        h_ref[...] = hi
        wsh = wsh_ref[...]
        os_ref[...] = _dot(hi, wsh) + _dot(lo, wsh) + _dot(hi, wsl_ref[...])

    o_ref[...] = _dot(h_ref[...], w_ref[...])


def _inproj(x2d, mods, row_of_tile, w_main, ws_hi, ws_lo, tm, tn=1024):
    rows, d = x2d.shape
    n = w_main.shape[1]
    return pl.pallas_call(
        _inproj_kernel,
        out_shape=(jax.ShapeDtypeStruct((rows, n), F32), jax.ShapeDtypeStruct((rows, SMALL_W), F32)),
        grid=(rows // tm, n // tn),
        in_specs=[pl.BlockSpec((tm, d), lambda i, j: (i, 0)),
                  pl.BlockSpec((None, 1, d), lambda i, j: (row_of_tile(i) * N_MOD + 0, 0, 0)),
                  pl.BlockSpec((None, 1, d), lambda i, j: (row_of_tile(i) * N_MOD + 1, 0, 0)),
                  pl.BlockSpec((d, tn), lambda i, j: (0, j)),
                  pl.BlockSpec((d, SMALL_W), lambda i, j: (0, 0)),
                  pl.BlockSpec((d, SMALL_W), lambda i, j: (0, 0))],
        out_specs=(pl.BlockSpec((tm, tn), lambda i, j: (i, j)),
                   pl.BlockSpec((tm, SMALL_W), lambda i, j: (i, 0))),
        scratch_shapes=[pltpu.VMEM((tm, d), BF16)],
        compiler_params=_cparams(2),
        name="inproj",
    )(x2d, mods, mods, w_main, ws_hi, ws_lo)


def _rms_rope(x, gain, cos, sin):
    y = x * lax.rsqrt(jnp.mean(x * x, axis=-1, keepdims=True) + 1e-6) * gain
    if cos is not None:
        lane = lax.broadcasted_iota(jnp.int32, y.shape, 1)
        partner = jnp.where((lane % 64) < 32, pltpu.roll(y, 96, 1), pltpu.roll(y, 32, 1))
        y = y * cos + partner * sin
    return y


KV_W = ATTN_KV_HEADS * HEAD_DIM + ATTN_KV_HEADS * 2 * HEAD_DIM
LOG2E = 1.4426950408889634


def _qkv_prep_kernel(*refs, rope):
    if rope:
        q_ref, kv_ref, gain_ref, cos_ref, sin_ref, qo_ref, kvo_ref = refs
        cos, sin = cos_ref[...], sin_ref[...]
    else:
        q_ref, kv_ref, gain_ref, qo_ref, kvo_ref = refs
        cos = sin = None
    q = q_ref[...]
    q_scale = (HEAD_DIM ** -0.5) * LOG2E
    qo_ref[...] = jnp.concatenate(
        [(_rms_rope(q[:, g * HEAD_DIM:(g + 1) * HEAD_DIM], gain_ref[0:1, :], cos, sin) * q_scale).astype(BF16)
         for g in range(ATTN_HEADS)], axis=1)
    kv = kv_ref[...]
    outs = [_rms_rope(kv[:, h * HEAD_DIM:(h + 1) * HEAD_DIM], gain_ref[1:2, :], cos, sin).astype(BF16)
            for h in range(ATTN_KV_HEADS)]
    ones = jnp.ones((kv.shape[0], HEAD_DIM), BF16)
    for h in range(ATTN_KV_HEADS):
        outs += [kv[:, (ATTN_KV_HEADS + h) * HEAD_DIM:(ATTN_KV_HEADS + h + 1) * HEAD_DIM].astype(BF16), ones]
    kvo_ref[...] = jnp.concatenate(outs, axis=1)


def _qkv_prep(p, gain, rope_tabs, n_per_sample, tm):
    rows = p.shape[0]
    qw = ATTN_HEADS * HEAD_DIM
    kvw = 2 * ATTN_KV_HEADS * HEAD_DIM
    per = n_per_sample // tm
    in_specs = [pl.BlockSpec((tm, qw), lambda i: (i, _OFF["att_q"] // qw)),
                pl.BlockSpec((tm, kvw), lambda i: (i, _OFF["att_k"] // kvw)),
                pl.BlockSpec((2, HEAD_DIM), lambda i: (0, 0))]
    args = [p, p, gain]
    if rope_tabs is not None:
        in_specs += [pl.BlockSpec((tm, HEAD_DIM), lambda i: (i % per, 0))] * 2
        args += list(rope_tabs)
    return pl.pallas_call(
        functools.partial(_qkv_prep_kernel, rope=rope_tabs is not None),
        out_shape=(jax.ShapeDtypeStruct((rows, qw), BF16), jax.ShapeDtypeStruct((rows, KV_W), BF16)),
        grid=(rows // tm,),
        in_specs=in_specs,
        out_specs=(pl.BlockSpec((tm, qw), lambda i: (i, 0)), pl.BlockSpec((tm, KV_W), lambda i: (i, 0))),
        compiler_params=_cparams(1),
        name="qkv_prep",
    )(*args)


def _attn_kernel(*refs, n_seg):
    q_ref = refs[0]
    k_refs = refs[1:1 + 2 * n_seg:2]
    v_refs = refs[2:2 + 2 * n_seg:2]
    o_ref = refs[1 + 2 * n_seg]
    outs = []
    for g in range(ATTN_GROUP):
        qg = q_ref[:, g * HEAD_DIM:(g + 1) * HEAD_DIM]
        s = jnp.concatenate([lax.dot_general(qg, k_ref[...], _NT, preferred_element_type=F32) for k_ref in k_refs],
                            axis=1)
        p = jnp.exp2(s - jnp.max(s, axis=-1, keepdims=True)).astype(BF16)
        acc = None
        off = 0
        for v_ref in v_refs:
            n = v_ref.shape[0]
            part = _dot(p[:, off:off + n], v_ref[...])
            acc = part if acc is None else acc + part
            off += n
        outs.append((acc[:, :HEAD_DIM] / acc[:, HEAD_DIM:]).astype(o_ref.dtype))
    o_ref[...] = jnp.concatenate(outs, axis=1)


def _attention(qn, kv_segs, bsz, tq):
    rows = qn.shape[0]
    nq = rows // bsz // tq
    gw = ATTN_GROUP * HEAD_DIM
    in_specs = [pl.BlockSpec((tq, gw), lambda b, h, i: (b * nq + i, h))]
    args = [qn]
    for arr, n in kv_segs:
        in_specs.append(pl.BlockSpec((n, HEAD_DIM), lambda b, h, i: (b, h)))
        in_specs.append(pl.BlockSpec((n, 2 * HEAD_DIM), lambda b, h, i: (b, 1 + h)))
        args += [arr, arr]
    return pl.pallas_call(
        functools.partial(_attn_kernel, n_seg=len(kv_segs)),
        out_shape=jax.ShapeDtypeStruct((rows, ATTN_HEADS * HEAD_DIM), BF16),
        grid=(bsz, ATTN_KV_HEADS, nq),
        in_specs=in_specs,
        out_specs=pl.BlockSpec((tq, gw), lambda b, h, i: (b * nq + i, h)),
        compiler_params=_cparams(3),
        name="attention",
    )(*args)


def _chunk_masks(tb):
    i = lax.broadcasted_iota(jnp.int32, (tb, tb), 0)
    j = lax.broadcasted_iota(jnp.int32, (tb, tb), 1)
    same = (i // CHUNK) == (j // CHUNK)
    return i, j, same


def _gla_kernel(qk_f, v_f, s_f, qk_b, v_b, s_b, wup_ref, bup_ref, tri_ref, st0_ref,
                of_ref, ob_ref, stout_ref, st_scr):
    step = pl.program_id(1)

    @pl.when(step == 0)
    def _():
        st_scr[...] = st0_ref[...]

    tb = qk_f.shape[0]
    nch = tb // CHUNK
    hk = GLA_HEADS * GLA_DK
    lane = lax.broadcasted_iota(jnp.int32, (CHUNK, hk), 1)
    head_of_lane = lane // GLA_DK
    r4 = lax.broadcasted_iota(jnp.int32, (GLA_HEADS * CHUNK, CHUNK), 0) % CHUNK
    c4 = lax.broadcasted_iota(jnp.int32, (GLA_HEADS * CHUNK, CHUNK), 1)
    bd_r = lax.broadcasted_iota(jnp.int32, (MIX_HW, hk), 0) // GLA_DV
    bd_c = lax.broadcasted_iota(jnp.int32, (MIX_HW, hk), 1) // GLA_DK
    bd_mask = bd_r == bd_c

    for z, (qk_ref, v_ref, s_ref, o_ref) in enumerate(((qk_f, v_f, s_f, of_ref), (qk_b, v_b, s_b, ob_ref))):
        qk = qk_ref[...]
        q = qk[:, :hk] * (GLA_DK ** -0.5)
        k = qk[:, hk:]
        v = v_ref[...].astype(BF16)
        hi, lo = _split_bf16(s_ref[...])
        wh = wup_ref[z, 0]
        logit = _dot(hi, wh) + _dot(lo, wh) + _dot(hi, wup_ref[z, 1]) + bup_ref[z]
        log_a = -_softplus(-logit) / GLA_GATE_NORM
        cum = _dot2_left(tri_ref[z], log_a)
        intra_mask = (c4 <= r4) if z == 0 else (c4 >= r4)
        last = CHUNK - 1 if z == 0 else 0
        st = st_scr[z]
        outs = [None] * nch
        for c in (range(nch) if z == 0 else range(nch - 1, -1, -1)):
            r0 = c * CHUNK
            cum_c = cum[r0:r0 + CHUNK]
            tot = cum_c[last:last + 1, :]
            q_dec = q[r0:r0 + CHUNK] * jnp.exp(cum_c)
            k_c = k[r0:r0 + CHUNK]
            k_inv = (k_c * jnp.exp(-cum_c)).astype(BF16)
            k_dec = (k_c * jnp.exp(tot - cum_c)).astype(BF16)
            q4 = jnp.concatenate([jnp.where(head_of_lane == h, q_dec, 0.0) for h in range(GLA_HEADS)],
                                 axis=0).astype(BF16)
            a4 = lax.dot_general(q4, k_inv, _NT, preferred_element_type=F32)
            a4 = jnp.where(intra_mask, a4, 0.0).astype(BF16)
            v_c = v[r0:r0 + CHUNK]
            o4 = _dot(a4, v_c)
            o_intra = jnp.concatenate(
                [o4[h * CHUNK:(h + 1) * CHUNK, h * GLA_DV:(h + 1) * GLA_DV] for h in range(GLA_HEADS)], axis=1)
            o_inter = lax.dot_general(q_dec.astype(BF16), st.astype(BF16), _NT, preferred_element_type=F32)
            outs[c] = o_intra + o_inter
            kv_t = lax.dot_general(v_c, k_dec, _TN, preferred_element_type=F32)
            st = st * jnp.exp(tot) + jnp.where(bd_mask, kv_t, 0.0)
        o_ref[...] = jnp.concatenate(outs, axis=0)
        st_scr[z] = st

    @pl.when(step == pl.num_programs(1) - 1)
    def _():
        stout_ref[...] = st_scr[...]


def _gla(p, ps, t, bsz, wup, bup, tri, st0):
    tb = SCAN_TB
    nblk = t // tb
    qk_col = _OFF["gla_q"] // (2 * GLA_HEADS * GLA_DK)
    v_col = _OFF["gla_v"] // MIX_HW
    fwd = lambda b, i: b * nblk + i
    bwd = lambda b, i: b * nblk + (nblk - 1 - i)
    hk = GLA_HEADS * GLA_DK
    return pl.pallas_call(
        _gla_kernel,
        out_shape=(jax.ShapeDtypeStruct((bsz * t, MIX_HW), F32), jax.ShapeDtypeStruct((bsz * t, MIX_HW), F32),
                   jax.ShapeDtypeStruct((bsz, 2, MIX_HW, hk), F32)),
        grid=(bsz, nblk),
        in_specs=[pl.BlockSpec((tb, 2 * hk), lambda b, i: (fwd(b, i), qk_col)),
                  pl.BlockSpec((tb, MIX_HW), lambda b, i: (fwd(b, i), v_col)),
                  pl.BlockSpec((tb, SMALL_W), lambda b, i: (fwd(b, i), 0)),
                  pl.BlockSpec((tb, 2 * hk), lambda b, i: (bwd(b, i), qk_col)),
                  pl.BlockSpec((tb, MIX_HW), lambda b, i: (bwd(b, i), v_col)),
                  pl.BlockSpec((tb, SMALL_W), lambda b, i: (bwd(b, i), 0)),
                  pl.BlockSpec((2, 2, SMALL_W, hk), lambda b, i: (0, 0, 0, 0)),
                  pl.BlockSpec((2, 1, hk), lambda b, i: (0, 0, 0)),
                  pl.BlockSpec((2, tb, tb), lambda b, i: (0, 0, 0)),
                  pl.BlockSpec((None, 2, MIX_HW, hk), lambda b, i: (b, 0, 0, 0))],
        out_specs=(pl.BlockSpec((tb, MIX_HW), lambda b, i: (fwd(b, i), 0)),
                   pl.BlockSpec((tb, MIX_HW), lambda b, i: (bwd(b, i), 0)),
                   pl.BlockSpec((None, 2, MIX_HW, hk), lambda b, i: (b, 0, 0, 0))),
        scratch_shapes=[pltpu.VMEM((2, MIX_HW, hk), F32)],
        compiler_params=_cparams(2),
        name="gla_scan",
    )(p, p, ps, p, p, ps, wup, bup, tri, st0)


def _gdn_prep_kernel(x_ref, prev_ref, next_ref, w_ref, o_ref):
    i = pl.program_id(1)
    tb = x_ref.shape[0]
    halo = prev_ref.shape[0]
    prev = jnp.where(i > 0, prev_ref[...], 0.0)
    nxt = jnp.where(i < pl.num_programs(1) - 1, next_ref[...], 0.0)
    ext = jnp.concatenate([prev, x_ref[...], nxt], axis=0)
    w = w_ref[...]
    acc = None
    for j in range(GDN_CONV):
        shift = GDN_CONV // 2 - j
        rolled = ext if shift == 0 else pltpu.roll(ext, shift % (tb + 2 * halo), 0)
        term = rolled[halo:halo + tb] * w[j:j + 1, :]
        acc = term if acc is None else acc + term
    y = acc * _sigmoid(acc)
    qk_w = 2 * GDN_HEADS * GDN_DK
    outs = []
    for h in range(2 * GDN_HEADS):
        yh = y[:, h * GDN_DK:(h + 1) * GDN_DK]
        yh = yh * lax.rsqrt(jnp.sum(yh * yh, axis=-1, keepdims=True) + 1e-6)
        if h < GDN_HEADS:
            yh = yh * (GDN_DK ** -0.5)
        outs.append(yh)
    outs.append(y[:, qk_w:])
    o_ref[...] = jnp.concatenate(outs, axis=1)


def _gdn_prep(p, t, bsz, conv_w):
    tb = SCAN_TB
    halo = 8
    nblk = t // tb
    width = conv_w.shape[1]
    col = _OFF["gdn_q"] // width
    n_halo_blocks = bsz * t // halo
    per = tb // halo
    return pl.pallas_call(
        _gdn_prep_kernel,
        out_shape=jax.ShapeDtypeStruct((bsz * t, width), F32),
        grid=(bsz, nblk),
        in_specs=[pl.BlockSpec((tb, width), lambda b, i: (b * nblk + i, col)),
                  pl.BlockSpec((halo, width), lambda b, i: (jnp.maximum((b * nblk + i) * per - 1, 0), col)),
                  pl.BlockSpec((halo, width),
                               lambda b, i: (jnp.minimum((b * nblk + i + 1) * per, n_halo_blocks - 1), col)),
                  pl.BlockSpec((GDN_CONV, width), lambda b, i: (0, 0))],
        out_specs=pl.BlockSpec((tb, width), lambda b, i: (b * nblk + i, 0)),
        compiler_params=_cparams(2),
        name="gdn_conv",
    )(p, p, p, conv_w)


def _gdn_kernel(x_f, s_f, x_b, s_b, sel_ref, ealog_ref, dtb_ref, tri_ref, s0_ref,
                of_ref, ob_ref, sout_ref, s_scr):
    step = pl.program_id(1)

    @pl.when(step == 0)
    def _():
        s_scr[...] = s0_ref[...]

    tb = x_f.shape[0]
    nch = tb // CHUNK
    i_idx, j_idx, same = _chunk_masks(tb)
    xor = i_idx ^ j_idx
    eye = jnp.where(i_idx == j_idx, 1.0, 0.0)
    hw = GDN_HEADS * GDN_DK

    chains = []
    for z, (x_ref, s_ref) in enumerate(((x_f, s_f), (x_b, s_b))):
        before = (j_idx <= i_idx) if z == 0 else (j_idx >= i_idx)
        incl = same & before
        strict = incl & (i_idx != j_idx)
        sm = s_ref[...]
        beta_all = _sigmoid(sm)
        g_all = -ealog_ref[...] * _softplus(sm + dtb_ref[...])
        b_sel = _dot2(beta_all, sel_ref[2 * z])
        g_sel = _dot2(g_all, sel_ref[2 * z + 1])
        c_col = _dot2_left(tri_ref[z], g_sel)
        x = x_ref[...]
        for h in range(GDN_HEADS):
            ch = dict(z=z, h=h, incl=incl)
            ch["q"] = x[:, h * GDN_DK:(h + 1) * GDN_DK]
            ch["k"] = x[:, hw + h * GDN_DK:hw + (h + 1) * GDN_DK]
            bc = b_sel[:, h * GDN_DK:(h + 1) * GDN_DK]
            cc = c_col[:, h * GDN_DK:(h + 1) * GDN_DK]
            ch["cc"] = cc
            kb = ch["k"] * bc
            ch["vb"] = (x[:, 2 * hw + h * GDN_DV:2 * hw + (h + 1) * GDN_DV] * bc).astype(BF16)
            ch["kbg"] = (kb * jnp.exp(cc)).astype(BF16)
            ch["k_bf"] = ch["k"].astype(BF16)
            c_row = cc.T[0:1, :]
            dm = jnp.concatenate([cc] * (tb // GDN_DK), axis=1) - c_row
            ch["gamma"] = jnp.where(incl, jnp.exp(jnp.where(incl, dm, 0.0)), 0.0)
            kk = lax.dot_general(kb.astype(BF16), ch["k_bf"], _NT, preferred_element_type=F32)
            ch["l_mat"] = jnp.where(strict, kk * ch["gamma"], 0.0)
            ch["t_inv"] = eye - jnp.where(xor == 1, ch["l_mat"], 0.0)
            chains.append(ch)

    s = 2
    while s < CHUNK:
        lvl = (xor >= s) & (xor < 2 * s)
        for ch in chains:
            ch["t_bf"] = ch["t_inv"].astype(BF16)
            ch["m1"] = _dot(ch["t_bf"], jnp.where(lvl, ch["l_mat"], 0.0).astype(BF16)).astype(BF16)
        for ch in chains:
            ch["t_inv"] = ch["t_inv"] - _dot(ch["m1"], ch["t_bf"])
        s *= 2

    for ch in chains:
        t_bf = ch["t_inv"].astype(BF16)
        ch["u"] = _dot(t_bf, ch["vb"])
        ch["w"] = _dot(t_bf, ch["kbg"]).astype(BF16)
        qk = lax.dot_general(ch["q"].astype(BF16), ch["k_bf"], _NT, preferred_element_type=F32)
        ch["a_int"] = jnp.where(ch["incl"], qk * ch["gamma"], 0.0).astype(BF16)
        ch["state"] = s_scr[ch["z"], ch["h"]]
        ch["outs"] = [None] * nch

    zeros = jnp.zeros((CHUNK, GDN_DV), BF16)
    for c_step in range(nch):
        for ch in chains:
            c = c_step if ch["z"] == 0 else nch - 1 - c_step
            last = CHUNK - 1 if ch["z"] == 0 else 0
            r0 = c * CHUNK
            cc_c = ch["cc"][r0:r0 + CHUNK]
            g_last = cc_c[last:last + 1, :]
            k_dec = (ch["k"][r0:r0 + CHUNK] * jnp.exp(g_last - cc_c)).astype(BF16)
            q_dec = (ch["q"][r0:r0 + CHUNK] * jnp.exp(cc_c)).astype(BF16)
            s_bf = ch["state"].astype(BF16)
            v_new = ch["u"][r0:r0 + CHUNK] - _dot(ch["w"][r0:r0 + CHUNK], s_bf)
            v_new_bf = v_new.astype(BF16)
            v_pad = jnp.concatenate([v_new_bf if cc_i == c else zeros for cc_i in range(nch)], axis=0)
            ch["outs"][c] = _dot(q_dec, s_bf) + _dot(ch["a_int"][r0:r0 + CHUNK], v_pad)
            ch["state"] = (ch["state"] * jnp.exp(g_last)
                           + lax.dot_general(k_dec, v_new_bf, _TN, preferred_element_type=F32))

    for z, o_ref in enumerate((of_ref, ob_ref)):
        mine = [ch for ch in chains if ch["z"] == z]
        o_ref[...] = jnp.concatenate([jnp.concatenate(ch["outs"], axis=0) for ch in mine], axis=1)
        for ch in mine:
            s_scr[z, ch["h"]] = ch["state"]

    @pl.when(step == pl.num_programs(1) - 1)
    def _():
        sout_ref[...] = s_scr[...]


def _gdn(xc, ps, t, bsz, sel, ealog, dtb, tri, s0):
    tb = SCAN_TB
    nblk = t // tb
    width = xc.shape[1]
    fwd = lambda b, i: b * nblk + i
    bwd = lambda b, i: b * nblk + (nblk - 1 - i)
    return pl.pallas_call(
        _gdn_kernel,
        out_shape=(jax.ShapeDtypeStruct((bsz * t, MIX_HW), F32), jax.ShapeDtypeStruct((bsz * t, MIX_HW), F32),
                   jax.ShapeDtypeStruct((bsz, 2, GDN_HEADS, GDN_DK, GDN_DV), F32)),
        grid=(bsz, nblk),
        in_specs=[pl.BlockSpec((tb, width), lambda b, i: (fwd(b, i), 0)),
                  pl.BlockSpec((tb, SMALL_W), lambda b, i: (fwd(b, i), 0)),
                  pl.BlockSpec((tb, width), lambda b, i: (bwd(b, i), 0)),
                  pl.BlockSpec((tb, SMALL_W), lambda b, i: (bwd(b, i), 0)),
                  pl.BlockSpec((4, SMALL_W, MIX_HW), lambda b, i: (0, 0, 0)),
                  pl.BlockSpec((1, SMALL_W), lambda b, i: (0, 0)),
                  pl.BlockSpec((1, SMALL_W), lambda b, i: (0, 0)),
                  pl.BlockSpec((2, tb, tb), lambda b, i: (0, 0, 0)),
                  pl.BlockSpec((None, 2, GDN_HEADS, GDN_DK, GDN_DV), lambda b, i: (b, 0, 0, 0, 0))],
        out_specs=(pl.BlockSpec((tb, MIX_HW), lambda b, i: (fwd(b, i), 0)),
                   pl.BlockSpec((tb, MIX_HW), lambda b, i: (bwd(b, i), 0)),
                   pl.BlockSpec((None, 2, GDN_HEADS, GDN_DK, GDN_DV), lambda b, i: (b, 0, 0, 0, 0))),
        scratch_shapes=[pltpu.VMEM((2, GDN_HEADS, GDN_DK, GDN_DV), F32)],
        compiler_params=_cparams(2),
        name="gdn_scan",
    )(xc, ps, xc, ps, sel, ealog, dtb, tri, s0)


def _layer_norm_rows(z, gain, bias):
    mu = jnp.mean(z, axis=-1, keepdims=True)
    zc = z - mu
    var = jnp.mean(zc * zc, axis=-1, keepdims=True)
    return zc * lax.rsqrt(var + 1e-5) * gain + bias


def _mixer_finish(o, gate, gain, n_heads, dv):
    outs = []
    for h in range(n_heads):
        oh = o[:, h * dv:(h + 1) * dv]
        oh = oh * lax.rsqrt(jnp.mean(oh * oh, axis=-1, keepdims=True) + 1e-6) * gain
        gh = gate[:, h * dv:(h + 1) * dv]
        outs.append((oh * (gh * _sigmoid(gh))).astype(BF16))
    return outs


def _outproj_kernel(glaf_ref, glab_ref, gdnf_ref, gdnb_ref, g_ref, z_ref, att_ref, x_ref, gate_ref, w_ref,
                    ngla_ref, ngdn_ref, gain_ref, bias_ref, o_ref):
    parts = _mixer_finish(glaf_ref[...] + glab_ref[...], g_ref[...], ngla_ref[...], GLA_HEADS, GLA_DV)
    parts += _mixer_finish(gdnf_ref[...] + gdnb_ref[...], z_ref[...], ngdn_ref[...], GDN_HEADS, GDN_DV)
    parts.append(att_ref[...])
    y = _dot(jnp.concatenate(parts, axis=1), w_ref[...])
    z = DEEPNORM_ALPHA * x_ref[...] + gate_ref[...] * y
    o_ref[...] = _layer_norm_rows(z, gain_ref[...], bias_ref[...])


def _outproj(gla_f, gla_b, gdn_f, gdn_b, p, att, x2d, mods, row_of_tile, w_out, n_gla, n_gdn, gain, bias, tm):
    rows, d = x2d.shape
    g_col = _OFF["gla_g"] // MIX_HW
    z_col = _OFF["gdn_z"] // MIX_HW
    mix = lambda: pl.BlockSpec((tm, MIX_HW), lambda i: (i, 0))
    vec = lambda n: pl.BlockSpec((1, n), lambda i: (0, 0))
    return pl.pallas_call(
        _outproj_kernel,
        out_shape=jax.ShapeDtypeStruct((rows, d), F32),
        grid=(rows // tm,),
        in_specs=[mix(), mix(), mix(), mix(),
                  pl.BlockSpec((tm, MIX_HW), lambda i: (i, g_col)),
                  pl.BlockSpec((tm, MIX_HW), lambda i: (i, z_col)),
                  pl.BlockSpec((tm, att.shape[1]), lambda i: (i, 0)),
                  pl.BlockSpec((tm, d), lambda i: (i, 0)),
                  pl.BlockSpec((None, 1, d), lambda i: (row_of_tile(i) * N_MOD + 2, 0, 0)),
                  pl.BlockSpec((w_out.shape[0], d), lambda i: (0, 0)),
                  vec(GLA_DV), vec(GDN_DV), vec(d), vec(d)],
        out_specs=pl.BlockSpec((tm, d), lambda i: (i, 0)),
        compiler_params=_cparams(1),
        name="outproj_ln",
    )(gla_f, gla_b, gdn_f, gdn_b, p, p, att, x2d, mods, w_out, n_gla.reshape(1, -1), n_gdn.reshape(1, -1),
      gain.reshape(1, d), bias.reshape(1, d))


def _router_kernel(x_ref, shift_ref, scale_ref, rh_ref, rl_ref, h_ref, lg_ref):
    h = x_ref[...] * (1.0 + scale_ref[...]) + shift_ref[...]
    hi, lo = _split_bf16(h)
    h_ref[...] = hi
    rh = rh_ref[...]
    lg_ref[...] = _dot(hi, rh) + _dot(lo, rh) + _dot(hi, rl_ref[...])


def _router(x2d, mods, row_of_tile, r_hi, r_lo, tm):
    rows, d = x2d.shape
    return pl.pallas_call(
        _router_kernel,
        out_shape=(jax.ShapeDtypeStruct((rows, d), BF16), jax.ShapeDtypeStruct((rows, 128), F32)),
        grid=(rows // tm,),
        in_specs=[pl.BlockSpec((tm, d), lambda i: (i, 0)),
                  pl.BlockSpec((None, 1, d), lambda i: (row_of_tile(i) * N_MOD + 3, 0, 0)),
                  pl.BlockSpec((None, 1, d), lambda i: (row_of_tile(i) * N_MOD + 4, 0, 0)),
                  pl.BlockSpec((d, 128), lambda i: (0, 0)),
                  pl.BlockSpec((d, 128), lambda i: (0, 0))],
        out_specs=(pl.BlockSpec((tm, d), lambda i: (i, 0)), pl.BlockSpec((tm, 128), lambda i: (i, 0))),
        compiler_params=_cparams(1),
        name="router",
    )(x2d, mods, mods, r_hi, r_lo)


def _ffn_kernel(x_ref, g_ref, w1_ref, w3_ref, w2_ref, o_ref, h_scr, *, nf):
    s = pl.program_id(1)
    tf = w1_ref.shape[1]

    @pl.when(s < nf)
    def _():
        x = x_ref[...]
        a = _dot(x, w1_ref[...].astype(BF16))
        u = _dot(x, w3_ref[...].astype(BF16))
        hmid = (a * _sigmoid(a) * u).astype(BF16)
        for f in range(nf):
            @pl.when(s == f)
            def _():
                h_scr[:, f * tf:(f + 1) * tf] = hmid

    @pl.when(s >= nf)
    def _():
        o_ref[...] = _dot(h_scr[...], w2_ref[...].astype(BF16)) * g_ref[...]


def _ffn(xs, gates, w1, w3, w2, layer, tf=512, td=512):
    n_e, r, d = xs.shape
    ff = w1.shape[3]
    nf = ff // tf
    up = lambda e, s: (layer, e, 0, jnp.minimum(s, nf - 1))
    down = lambda e, s: (layer, e, 0, jnp.maximum(s - nf, 0))
    return pl.pallas_call(
        functools.partial(_ffn_kernel, nf=nf),
        out_shape=jax.ShapeDtypeStruct((n_e, r, d), F32),
        grid=(n_e, nf + d // td),
        in_specs=[pl.BlockSpec((None, r, d), lambda e, s: (e, 0, 0)),
                  pl.BlockSpec((None, r, 1), lambda e, s: (e, 0, 0)),
                  pl.BlockSpec((None, None, d, tf), up),
                  pl.BlockSpec((None, None, d, tf), up),
                  pl.BlockSpec((None, None, ff, td), down)],
        out_specs=pl.BlockSpec((None, r, td), lambda e, s: (e, 0, jnp.maximum(s - nf, 0))),
        scratch_shapes=[pltpu.VMEM((r, ff), BF16)],
        compiler_params=_cparams(2),
        name="expert_ffn",
    )(xs, gates, w1, w3, w2)


def _resln_kernel(x_ref, f_ref, gate_ref, gain_ref, bias_ref, o_ref):
    z = DEEPNORM_ALPHA * x_ref[...] + gate_ref[...] * f_ref[...]
    o_ref[...] = _layer_norm_rows(z, gain_ref[...], bias_ref[...])


def _resln(x2d, f2d, f_row0, mods, row_of_tile, gain, bias, tm):
    rows, d = x2d.shape
    f_blk0 = f_row0 // tm
    return pl.pallas_call(
        _resln_kernel,
        out_shape=jax.ShapeDtypeStruct((rows, d), F32),
        grid=(rows // tm,),
        in_specs=[pl.BlockSpec((tm, d), lambda i: (i, 0)),
                  pl.BlockSpec((tm, d), lambda i: (i + f_blk0, 0)),
                  pl.BlockSpec((None, 1, d), lambda i: (row_of_tile(i) * N_MOD + 5, 0, 0)),
                  pl.BlockSpec((1, d), lambda i: (0, 0)),
                  pl.BlockSpec((1, d), lambda i: (0, 0))],
        out_specs=pl.BlockSpec((tm, d), lambda i: (i, 0)),
        compiler_params=_cparams(1),
        name="residual_ln",
    )(x2d, f2d, mods, gain.reshape(1, d), bias.reshape(1, d))


def _rope_tables(n_tokens):
    rows = n_tokens // GRID_W
    row = jnp.broadcast_to(jnp.arange(rows)[:, None], (rows, GRID_W)).reshape(-1).astype(F32)
    col = jnp.broadcast_to(jnp.arange(GRID_W)[None, :], (rows, GRID_W)).reshape(-1).astype(F32)
    half = HEAD_DIM // 2
    inv = ROPE_THETA ** (-jnp.arange(0, half, 2, dtype=F32) / half)
    a_row = row[:, None] * inv
    a_col = col[:, None] * inv
    cos_t = jnp.concatenate([jnp.cos(a_row), jnp.cos(a_row), jnp.cos(a_col), jnp.cos(a_col)], axis=-1)
    sin_t = jnp.concatenate([-jnp.sin(a_row), jnp.sin(a_row), -jnp.sin(a_col), jnp.sin(a_col)], axis=-1)
    return cos_t, sin_t


def _permute_w_in(w):
    main = jnp.concatenate([w[:, _SRC[n][0]:_SRC[n][0] + _SRC[n][1]] for n in _MAIN_ORDER], axis=1)
    small = jnp.concatenate([w[:, _SRC[n][0]:_SRC[n][0] + _SRC[n][1]] for n in _SMALL_ORDER], axis=1)
    small = jnp.pad(small, ((0, 0), (0, SMALL_W - small.shape[1])))
    return main, small


def _scan_tri(tb):
    i = jnp.arange(tb)[:, None]
    j = jnp.arange(tb)[None, :]
    same = (i // CHUNK) == (j // CHUNK)
    return jnp.stack([same & (j <= i), same & (j >= i)]).astype(BF16)


def _gla_params(w_up, b_up):
    hk = GLA_HEADS * GLA_DK
    w = jnp.zeros((2, SMALL_W, hk), F32)
    for z in range(2):
        r0 = _OFF["gla_r"] + z * GLA_RANK
        w = w.at[z, r0:r0 + GLA_RANK].set(w_up[z])
    hi, lo = _split_bf16(w)
    return jnp.stack([hi, lo], axis=1), b_up.reshape(2, 1, hk)


def _gdn_params(a_log, dt_bias):
    sel = jnp.zeros((4, SMALL_W, MIX_HW), F32)
    ealog = jnp.zeros((1, SMALL_W), F32)
    dtb = jnp.zeros((1, SMALL_W), F32)
    for z in range(2):
        for h in range(GDN_HEADS):
            lane_b = _OFF["gdn_b"] + z * GDN_HEADS + h
            lane_a = _OFF["gdn_a"] + z * GDN_HEADS + h
            sel = sel.at[2 * z, lane_b, h * GDN_DK:(h + 1) * GDN_DK].set(1.0)
            sel = sel.at[2 * z + 1, lane_a, h * GDN_DK:(h + 1) * GDN_DK].set(1.0)
            ealog = ealog.at[0, lane_a].set(jnp.exp(a_log[z, h]))
            dtb = dtb.at[0, lane_a].set(dt_bias[z, h])
    return sel.astype(BF16), ealog, dtb


def _moe(x2d_sets, mods, row_fns, r_hi, r_lo, w1, w3, w2, layer, gain, bias, bsz, tms):
    hs, gates, flat_idx, row0s = [], [], [], []
    row0 = 0
    for x2d, row_fn, tm in zip(x2d_sets, row_fns, tms):
        h, lg = _router(x2d, mods, row_fn, r_hi, r_lo, tm)
        t = x2d.shape[0] // bsz
        cap = EC_FACTOR * t // N_EXPERTS
        aff = jax.nn.softmax(lg[:, :N_EXPERTS].reshape(bsz, t, N_EXPERTS), axis=-1).transpose(0, 2, 1)
        gate, idx = lax.top_k(aff, cap)
        idx = idx + (row0 + jnp.arange(bsz) * t)[:, None, None]
        hs.append(h)
        gates.append(gate.transpose(1, 0, 2).reshape(N_EXPERTS, bsz * cap))
        flat_idx.append(idx.transpose(1, 0, 2).reshape(N_EXPERTS, bsz * cap))
        row0s.append(row0)
        row0 += x2d.shape[0]
    h_all = jnp.concatenate(hs, axis=0) if len(hs) > 1 else hs[0]
    fi = jnp.concatenate(flat_idx, axis=1) if len(hs) > 1 else flat_idx[0]
    gs = jnp.concatenate(gates, axis=1) if len(hs) > 1 else gates[0]
    xs = h_all[fi]
    y = _ffn(xs, gs[..., None], w1, w3, w2, layer)
    f_all = jnp.zeros((row0, h_all.shape[1]), F32).at[fi.reshape(-1)].add(y.reshape(-1, y.shape[-1]))
    return [_resln(x2d, f_all, r0, mods, row_fn, gain, bias, tm)
            for x2d, row_fn, tm, r0 in zip(x2d_sets, row_fns, tms, row0s)]


def kernel(x, c, ctx, c_ctx, w_ada, b_ada, w_in, w_out, gla_w_up, gla_b_up, gla_norm, gdn_conv, gdn_a_log,
           gdn_dt_bias, gdn_norm, attn_qk_norm, ln_gain, ln_bias, router, w1, w3, w2):
    bsz, n_lat, d = x.shape
    n_ctx = ctx.shape[1]
    depth = w_ada.shape[0]
    cos_t, sin_t = _rope_tables(n_lat)
    tri = _scan_tri(SCAN_TB)

    cond8 = jnp.zeros((8, d), F32).at[:bsz].set(c).at[bsz].set(c_ctx)
    mods_all = _ada(cond8, w_ada, b_ada)

    tm_lat, tm_ctx = 512, n_ctx
    lat_row = lambda i: i // (n_lat // tm_lat)
    ctx_row = lambda i: bsz + 0 * i
    tm_lat_ln = 256
    lat_row_ln = lambda i: i // (n_lat // tm_lat_ln)

    x_lat = x.reshape(bsz * n_lat, d)
    x_ctx = ctx.reshape(bsz * n_ctx, d)
    for l in range(depth):
        last = l == depth - 1
        mods = mods_all[l].reshape(8 * N_MOD, 1, d)
        w_main, w_small = _permute_w_in(w_in[l])
        w_main = w_main.astype(BF16)
        ws_hi, ws_lo = _split_bf16(w_small)

        p_lat, ps_lat = _inproj(x_lat, mods, lat_row, w_main, ws_hi, ws_lo, tm_lat)
        p_ctx, ps_ctx = _inproj(x_ctx, mods, ctx_row, w_main, ws_hi, ws_lo, tm_ctx)

        wup, bup = _gla_params(gla_w_up[l], gla_b_up[l])
        gla_s0 = jnp.zeros((bsz, 2, MIX_HW, GLA_HEADS * GLA_DK), F32)
        gla_cf, gla_cb, gla_s = _gla(p_ctx, ps_ctx, n_ctx, bsz, wup, bup, tri, gla_s0)
        gla_lf, gla_lb, _ = _gla(p_lat, ps_lat, n_lat, bsz, wup, bup, tri, gla_s)

        sel, ealog, dtb = _gdn_params(gdn_a_log[l], gdn_dt_bias[l])
        gdn_s0 = jnp.zeros((bsz, 2, GDN_HEADS, GDN_DK, GDN_DV), F32)
        xc_ctx = _gdn_prep(p_ctx, n_ctx, bsz, gdn_conv[l])
        xc_lat = _gdn_prep(p_lat, n_lat, bsz, gdn_conv[l])
        gdn_cf, gdn_cb, gdn_s = _gdn(xc_ctx, ps_ctx, n_ctx, bsz, sel, ealog, dtb, tri, gdn_s0)
        gdn_lf, gdn_lb, _ = _gdn(xc_lat, ps_lat, n_lat, bsz, sel, ealog, dtb, tri, gdn_s)

        qn_lat, kv_lat = _qkv_prep(p_lat, attn_qk_norm[l], (cos_t, sin_t), n_lat, 512)
        qn_ctx, kv_ctx = _qkv_prep(p_ctx, attn_qk_norm[l], None, n_ctx, n_ctx)
        att_l = _attention(qn_lat, [(kv_ctx, n_ctx), (kv_lat, n_lat)], bsz, 256)
        w_out_b = w_out[l].astype(BF16)
        x_lat = _outproj(gla_lf, gla_lb, gdn_lf, gdn_lb, p_lat, att_l, x_lat, mods, lat_row_ln, w_out_b,
                         gla_norm[l], gdn_norm[l], ln_gain[l, 0], ln_bias[l, 0], tm_lat_ln)
        r_hi, r_lo = _split_bf16(jnp.pad(router[l], ((0, 0), (0, 128 - N_EXPERTS))))
        if not last:
            att_c = _attention(qn_ctx, [(kv_ctx, n_ctx)], bsz, n_ctx)
            x_ctx = _outproj(gla_cf, gla_cb, gdn_cf, gdn_cb, p_ctx, att_c, x_ctx, mods, ctx_row, w_out_b,
                             gla_norm[l], gdn_norm[l], ln_gain[l, 0], ln_bias[l, 0], n_ctx)
            x_lat, x_ctx = _moe([x_lat, x_ctx], mods, [lat_row_ln, ctx_row], r_hi, r_lo, w1, w3, w2, l,
                                ln_gain[l, 1], ln_bias[l, 1], bsz, [tm_lat_ln, n_ctx])
        else:
            (x_lat,) = _moe([x_lat], mods, [lat_row_ln], r_hi, r_lo, w1, w3, w2, l,
                            ln_gain[l, 1], ln_bias[l, 1], bsz, [tm_lat_ln])
    return x_lat.reshape(bsz, n_lat, d)
```

```python
import functools

import jax
import jax.numpy as jnp
from jax import lax
from jax.experimental import pallas as pl
from jax.experimental.pallas import tpu as pltpu

F32 = jnp.float32
BF16 = jnp.bfloat16

D_MODEL = 2048
DEPTH = 2
GRID_W = 64
HEAD_DIM = 128
CHUNK = 64
GLA_HEADS = 4
GLA_DK = 64
GLA_DV = 128
GLA_RANK = 16
GLA_GATE_NORM = 16.0
GDN_HEADS = 4
GDN_DK = 128
GDN_DV = 128
GDN_CONV = 5
ATTN_HEADS = 8
ATTN_KV_HEADS = 2
ATTN_GROUP = ATTN_HEADS // ATTN_KV_HEADS
ROPE_THETA = 10000.0
N_EXPERTS = 16
EC_FACTOR = 2
N_MOD = 6
DEEPNORM_ALPHA = (2 * DEPTH) ** 0.25

_SRC = dict(gla_q=(0, 256), gla_k=(256, 256), gla_v=(512, 512), gla_g=(1024, 512), gla_r=(1536, 32),
            gdn_q=(1568, 512), gdn_k=(2080, 512), gdn_v=(2592, 512), gdn_z=(3104, 512), gdn_b=(3616, 8),
            gdn_a=(3624, 8), att_q=(3632, 1024), att_k=(4656, 256), att_v=(4912, 256))
_MAIN_ORDER = ("att_q", "gla_v", "gdn_q", "gdn_k", "gdn_v", "gla_g", "gdn_z", "gla_q", "gla_k", "att_k", "att_v")
_SMALL_ORDER = ("gla_r", "gdn_b", "gdn_a")
_OFF = {}
_o = 0
for _n in _MAIN_ORDER:
    _OFF[_n] = _o
    _o += _SRC[_n][1]
MAIN_W = _o
_o = 0
for _n in _SMALL_ORDER:
    _OFF[_n] = _o
    _o += _SRC[_n][1]
SMALL_W = 128
MIX_HW = 512
SCAN_TB = 256

VMEM_LIMIT = 56 * 1024 * 1024

_NT = (((1,), (1,)), ((), ()))
_TN = (((0,), (0,)), ((), ()))


def _cparams(n_axes):
    return pltpu.CompilerParams(dimension_semantics=("arbitrary",) * n_axes, vmem_limit_bytes=VMEM_LIMIT)


def _split_bf16(a):
    hi = a.astype(BF16)
    lo = (a - hi.astype(F32)).astype(BF16)
    return hi, lo


def _dot(a, b):
    return jnp.dot(a, b, preferred_element_type=F32)


def _dot2(a, m):
    hi, lo = _split_bf16(a)
    return _dot(hi, m) + _dot(lo, m)


def _dot2_left(m, a):
    hi, lo = _split_bf16(a)
    return _dot(m, hi) + _dot(m, lo)


def _sigmoid(x):
    return 1.0 / (1.0 + jnp.exp(-x))


def _softplus(x):
    return jnp.maximum(x, 0.0) + jnp.log1p(jnp.exp(-jnp.abs(x)))


def _ada_kernel(c_ref, w_ref, b_ref, o_ref):
    c = c_ref[...]
    s = (c * _sigmoid(c)).astype(BF16)
    o_ref[...] = _dot(s, w_ref[...].astype(BF16)) + b_ref[...]


def _ada(cond8, w_ada, b_ada, tn=1024):
    depth, d, n = w_ada.shape
    return pl.pallas_call(
        _ada_kernel,
        out_shape=jax.ShapeDtypeStruct((depth, 8, n), F32),
        grid=(depth, n // tn),
        in_specs=[pl.BlockSpec((8, d), lambda l, j: (0, 0)),
                  pl.BlockSpec((None, d, tn), lambda l, j: (l, 0, j)),
                  pl.BlockSpec((None, 1, tn), lambda l, j: (l, 0, j))],
        out_specs=pl.BlockSpec((None, 8, tn), lambda l, j: (l, 0, j)),
        compiler_params=_cparams(2),
        name="ada",
    )(cond8, w_ada, b_ada.reshape(depth, 1, n))


def _inproj_kernel(x_ref, shift_ref, scale_ref, w_ref, wsh_ref, wsl_ref, o_ref, os_ref, h_ref):
    @pl.when(pl.program_id(1) == 0)
    def _():
        h = x_ref[...] * (1.0 + scale_ref[...]) + shift_ref[...]
        hi, lo = _split_bf16(h)
        h_ref[...] = hi
        wsh = wsh_ref[...]
        os_ref[...] = _dot(hi, wsh) + _dot(lo, wsh) + _dot(hi, wsl_ref[...])

    o_ref[...] = _dot(h_ref[...], w_ref[...])


def _inproj(x2d, mods, row_of_tile, w_main, ws_hi, ws_lo, tm, tn=1024):
    rows, d = x2d.shape
    n = w_main.shape[1]
    return pl.pallas_call(
        _inproj_kernel,
        out_shape=(jax.ShapeDtypeStruct((rows, n), F32), jax.ShapeDtypeStruct((rows, SMALL_W), F32)),
        grid=(rows // tm, n // tn),
        in_specs=[pl.BlockSpec((tm, d), lambda i, j: (i, 0)),
                  pl.BlockSpec((None, 1, d), lambda i, j: (row_of_tile(i) * N_MOD + 0, 0, 0)),
                  pl.BlockSpec((None, 1, d), lambda i, j: (row_of_tile(i) * N_MOD + 1, 0, 0)),
                  pl.BlockSpec((d, tn), lambda i, j: (0, j)),
                  pl.BlockSpec((d, SMALL_W), lambda i, j: (0, 0)),
                  pl.BlockSpec((d, SMALL_W), lambda i, j: (0, 0))],
        out_specs=(pl.BlockSpec((tm, tn), lambda i, j: (i, j)),
                   pl.BlockSpec((tm, SMALL_W), lambda i, j: (i, 0))),
        scratch_shapes=[pltpu.VMEM((tm, d), BF16)],
        compiler_params=_cparams(2),
        name="inproj",
    )(x2d, mods, mods, w_main, ws_hi, ws_lo)


def _rms_rope(x, gain, cos, sin):
    y = x * lax.rsqrt(jnp.mean(x * x, axis=-1, keepdims=True) + 1e-6) * gain
    if cos is not None:
        lane = lax.broadcasted_iota(jnp.int32, y.shape, 1)
        partner = jnp.where((lane % 64) < 32, pltpu.roll(y, 96, 1), pltpu.roll(y, 32, 1))
        y = y * cos + partner * sin
    return y


KV_W = ATTN_KV_HEADS * HEAD_DIM + ATTN_KV_HEADS * 2 * HEAD_DIM
LOG2E = 1.4426950408889634


def _qkv_prep_kernel(*refs, rope):
    if rope:
        q_ref, kv_ref, gain_ref, cos_ref, sin_ref, qo_ref, kvo_ref = refs
        cos, sin = cos_ref[...], sin_ref[...]
    else:
        q_ref, kv_ref, gain_ref, qo_ref, kvo_ref = refs
        cos = sin = None
    q = q_ref[...]
    q_scale = (HEAD_DIM ** -0.5) * LOG2E
    qo_ref[...] = jnp.concatenate(
        [(_rms_rope(q[:, g * HEAD_DIM:(g + 1) * HEAD_DIM], gain_ref[0:1, :], cos, sin) * q_scale).astype(BF16)
         for g in range(ATTN_HEADS)], axis=1)
    kv = kv_ref[...]
    outs = [_rms_rope(kv[:, h * HEAD_DIM:(h + 1) * HEAD_DIM], gain_ref[1:2, :], cos, sin).astype(BF16)
            for h in range(ATTN_KV_HEADS)]
    ones = jnp.ones((kv.shape[0], HEAD_DIM), BF16)
    for h in range(ATTN_KV_HEADS):
        outs += [kv[:, (ATTN_KV_HEADS + h) * HEAD_DIM:(ATTN_KV_HEADS + h + 1) * HEAD_DIM].astype(BF16), ones]
    kvo_ref[...] = jnp.concatenate(outs, axis=1)


def _qkv_prep(p, gain, rope_tabs, n_per_sample, tm):
    rows = p.shape[0]
    qw = ATTN_HEADS * HEAD_DIM
    kvw = 2 * ATTN_KV_HEADS * HEAD_DIM
    per = n_per_sample // tm
    in_specs = [pl.BlockSpec((tm, qw), lambda i: (i, _OFF["att_q"] // qw)),
                pl.BlockSpec((tm, kvw), lambda i: (i, _OFF["att_k"] // kvw)),
                pl.BlockSpec((2, HEAD_DIM), lambda i: (0, 0))]
    args = [p, p, gain]
    if rope_tabs is not None:
        in_specs += [pl.BlockSpec((tm, HEAD_DIM), lambda i: (i % per, 0))] * 2
        args += list(rope_tabs)
    return pl.pallas_call(
        functools.partial(_qkv_prep_kernel, rope=rope_tabs is not None),
        out_shape=(jax.ShapeDtypeStruct((rows, qw), BF16), jax.ShapeDtypeStruct((rows, KV_W), BF16)),
        grid=(rows // tm,),
        in_specs=in_specs,
        out_specs=(pl.BlockSpec((tm, qw), lambda i: (i, 0)), pl.BlockSpec((tm, KV_W), lambda i: (i, 0))),
        compiler_params=_cparams(1),
        name="qkv_prep",
    )(*args)


def _attn_kernel(*refs, n_seg):
    q_ref = refs[0]
    k_refs = refs[1:1 + 2 * n_seg:2]
    v_refs = refs[2:2 + 2 * n_seg:2]
    o_ref = refs[1 + 2 * n_seg]
    outs = []
    for g in range(ATTN_GROUP):
        qg = q_ref[:, g * HEAD_DIM:(g + 1) * HEAD_DIM]
        s = jnp.concatenate([lax.dot_general(qg, k_ref[...], _NT, preferred_element_type=F32) for k_ref in k_refs],
                            axis=1)
        p = jnp.exp2(s - jnp.max(s, axis=-1, keepdims=True)).astype(BF16)
        acc = None
        off = 0
        for v_ref in v_refs:
            n = v_ref.shape[0]
            part = _dot(p[:, off:off + n], v_ref[...])
            acc = part if acc is None else acc + part
            off += n
        outs.append((acc[:, :HEAD_DIM] / acc[:, HEAD_DIM:]).astype(o_ref.dtype))
    o_ref[...] = jnp.concatenate(outs, axis=1)


def _attention(qn, kv_segs, bsz, tq):
    rows = qn.shape[0]
    nq = rows // bsz // tq
    gw = ATTN_GROUP * HEAD_DIM
    in_specs = [pl.BlockSpec((tq, gw), lambda b, h, i: (b * nq + i, h))]
    args = [qn]
    for arr, n in kv_segs:
        in_specs.append(pl.BlockSpec((n, HEAD_DIM), lambda b, h, i: (b, h)))
        in_specs.append(pl.BlockSpec((n, 2 * HEAD_DIM), lambda b, h, i: (b, 1 + h)))
        args += [arr, arr]
    return pl.pallas_call(
        functools.partial(_attn_kernel, n_seg=len(kv_segs)),
        out_shape=jax.ShapeDtypeStruct((rows, ATTN_HEADS * HEAD_DIM), BF16),
        grid=(bsz, ATTN_KV_HEADS, nq),
        in_specs=in_specs,
        out_specs=pl.BlockSpec((tq, gw), lambda b, h, i: (b * nq + i, h)),
        compiler_params=_cparams(3),
        name="attention",
    )(*args)


def _chunk_masks(tb):
    i = lax.broadcasted_iota(jnp.int32, (tb, tb), 0)
    j = lax.broadcasted_iota(jnp.int32, (tb, tb), 1)
    same = (i // CHUNK) == (j // CHUNK)
    return i, j, same


def _gla_kernel(qk_f, v_f, s_f, qk_b, v_b, s_b, wup_ref, bup_ref, tri_ref, st0_ref,
                of_ref, ob_ref, stout_ref, st_scr):
    step = pl.program_id(1)

    @pl.when(step == 0)
    def _():
        st_scr[...] = st0_ref[...]

    tb = qk_f.shape[0]
    nch = tb // CHUNK
    hk = GLA_HEADS * GLA_DK
    lane = lax.broadcasted_iota(jnp.int32, (CHUNK, hk), 1)
    head_of_lane = lane // GLA_DK
    r4 = lax.broadcasted_iota(jnp.int32, (GLA_HEADS * CHUNK, CHUNK), 0) % CHUNK
    c4 = lax.broadcasted_iota(jnp.int32, (GLA_HEADS * CHUNK, CHUNK), 1)
    bd_r = lax.broadcasted_iota(jnp.int32, (MIX_HW, hk), 0) // GLA_DV
    bd_c = lax.broadcasted_iota(jnp.int32, (MIX_HW, hk), 1) // GLA_DK
    bd_mask = bd_r == bd_c

    for z, (qk_ref, v_ref, s_ref, o_ref) in enumerate(((qk_f, v_f, s_f, of_ref), (qk_b, v_b, s_b, ob_ref))):
        qk = qk_ref[...]
        q = qk[:, :hk] * (GLA_DK ** -0.5)
        k = qk[:, hk:]
        v = v_ref[...].astype(BF16)
        hi, lo = _split_bf16(s_ref[...])
        wh = wup_ref[z, 0]
        logit = _dot(hi, wh) + _dot(lo, wh) + _dot(hi, wup_ref[z, 1]) + bup_ref[z]
        log_a = -_softplus(-logit) / GLA_GATE_NORM
        cum = _dot2_left(tri_ref[z], log_a)
        intra_mask = (c4 <= r4) if z == 0 else (c4 >= r4)
        last = CHUNK - 1 if z == 0 else 0
        st = st_scr[z]
        outs = [None] * nch
        for c in (range(nch) if z == 0 else range(nch - 1, -1, -1)):
            r0 = c * CHUNK
            cum_c = cum[r0:r0 + CHUNK]
            tot = cum_c[last:last + 1, :]
            q_dec = q[r0:r0 + CHUNK] * jnp.exp(cum_c)
            k_c = k[r0:r0 + CHUNK]
            k_inv = (k_c * jnp.exp(-cum_c)).astype(BF16)
            k_dec = (k_c * jnp.exp(tot - cum_c)).astype(BF16)
            q4 = jnp.concatenate([jnp.where(head_of_lane == h, q_dec, 0.0) for h in range(GLA_HEADS)],
                                 axis=0).astype(BF16)
            a4 = lax.dot_general(q4, k_inv, _NT, preferred_element_type=F32)
            a4 = jnp.where(intra_mask, a4, 0.0).astype(BF16)
            v_c = v[r0:r0 + CHUNK]
            o4 = _dot(a4, v_c)
            o_intra = jnp.concatenate(
                [o4[h * CHUNK:(h + 1) * CHUNK, h * GLA_DV:(h + 1) * GLA_DV] for h in range(GLA_HEADS)], axis=1)
            o_inter = lax.dot_general(q_dec.astype(BF16), st.astype(BF16), _NT, preferred_element_type=F32)
            outs[c] = o_intra + o_inter
            kv_t = lax.dot_general(v_c, k_dec, _TN, preferred_element_type=F32)
            st = st * jnp.exp(tot) + jnp.where(bd_mask, kv_t, 0.0)
        o_ref[...] = jnp.concatenate(outs, axis=0)
        st_scr[z] = st

    @pl.when(step == pl.num_programs(1) - 1)
    def _():
        stout_ref[...] = st_scr[...]


def _gla(p, ps, t, bsz, wup, bup, tri, st0):
    tb = SCAN_TB
    nblk = t // tb
    qk_col = _OFF["gla_q"] // (2 * GLA_HEADS * GLA_DK)
    v_col = _OFF["gla_v"] // MIX_HW
    fwd = lambda b, i: b * nblk + i
    bwd = lambda b, i: b * nblk + (nblk - 1 - i)
    hk = GLA_HEADS * GLA_DK
    return pl.pallas_call(
        _gla_kernel,
        out_shape=(jax.ShapeDtypeStruct((bsz * t, MIX_HW), F32), jax.ShapeDtypeStruct((bsz * t, MIX_HW), F32),
                   jax.ShapeDtypeStruct((bsz, 2, MIX_HW, hk), F32)),
        grid=(bsz, nblk),
        in_specs=[pl.BlockSpec((tb, 2 * hk), lambda b, i: (fwd(b, i), qk_col)),
                  pl.BlockSpec((tb, MIX_HW), lambda b, i: (fwd(b, i), v_col)),
                  pl.BlockSpec((tb, SMALL_W), lambda b, i: (fwd(b, i), 0)),
                  pl.BlockSpec((tb, 2 * hk), lambda b, i: (bwd(b, i), qk_col)),
                  pl.BlockSpec((tb, MIX_HW), lambda b, i: (bwd(b, i), v_col)),
                  pl.BlockSpec((tb, SMALL_W), lambda b, i: (bwd(b, i), 0)),
                  pl.BlockSpec((2, 2, SMALL_W, hk), lambda b, i: (0, 0, 0, 0)),
                  pl.BlockSpec((2, 1, hk), lambda b, i: (0, 0, 0)),
                  pl.BlockSpec((2, tb, tb), lambda b, i: (0, 0, 0)),
                  pl.BlockSpec((None, 2, MIX_HW, hk), lambda b, i: (b, 0, 0, 0))],
        out_specs=(pl.BlockSpec((tb, MIX_HW), lambda b, i: (fwd(b, i), 0)),
                   pl.BlockSpec((tb, MIX_HW), lambda b, i: (bwd(b, i), 0)),
                   pl.BlockSpec((None, 2, MIX_HW, hk), lambda b, i: (b, 0, 0, 0))),
        scratch_shapes=[pltpu.VMEM((2, MIX_HW, hk), F32)],
        compiler_params=_cparams(2),
        name="gla_scan",
    )(p, p, ps, p, p, ps, wup, bup, tri, st0)


def _gdn_prep_kernel(x_ref, prev_ref, next_ref, w_ref, o_ref):
    i = pl.program_id(1)
    tb = x_ref.shape[0]
    halo = prev_ref.shape[0]
    prev = jnp.where(i > 0, prev_ref[...], 0.0)
    nxt = jnp.where(i < pl.num_programs(1) - 1, next_ref[...], 0.0)
    ext = jnp.concatenate([prev, x_ref[...], nxt], axis=0)
    w = w_ref[...]
    acc = None
    for j in range(GDN_CONV):
        shift = GDN_CONV // 2 - j
        rolled = ext if shift == 0 else pltpu.roll(ext, shift % (tb + 2 * halo), 0)
        term = rolled[halo:halo + tb] * w[j:j + 1, :]
        acc = term if acc is None else acc + term
    y = acc * _sigmoid(acc)
    qk_w = 2 * GDN_HEADS * GDN_DK
    outs = []
    for h in range(2 * GDN_HEADS):
        yh = y[:, h * GDN_DK:(h + 1) * GDN_DK]
        yh = yh * lax.rsqrt(jnp.sum(yh * yh, axis=-1, keepdims=True) + 1e-6)
        if h < GDN_HEADS:
            yh = yh * (GDN_DK ** -0.5)
        outs.append(yh)
    outs.append(y[:, qk_w:])
    o_ref[...] = jnp.concatenate(outs, axis=1)


def _gdn_prep(p, t, bsz, conv_w):
    tb = SCAN_TB
    halo = 8
    nblk = t // tb
    width = conv_w.shape[1]
    col = _OFF["gdn_q"] // width
    n_halo_blocks = bsz * t // halo
    per = tb // halo
    return pl.pallas_call(
        _gdn_prep_kernel,
        out_shape=jax.ShapeDtypeStruct((bsz * t, width), F32),
        grid=(bsz, nblk),
        in_specs=[pl.BlockSpec((tb, width), lambda b, i: (b * nblk + i, col)),
                  pl.BlockSpec((halo, width), lambda b, i: (jnp.maximum((b * nblk + i) * per - 1, 0), col)),
                  pl.BlockSpec((halo, width),
                               lambda b, i: (jnp.minimum((b * nblk + i + 1) * per, n_halo_blocks - 1), col)),
                  pl.BlockSpec((GDN_CONV, width), lambda b, i: (0, 0))],
        out_specs=pl.BlockSpec((tb, width), lambda b, i: (b * nblk + i, 0)),
        compiler_params=_cparams(2),
        name="gdn_conv",
    )(p, p, p, conv_w)


def _gdn_kernel(x_f, s_f, x_b, s_b, sel_ref, ealog_ref, dtb_ref, tri_ref, s0_ref,
                of_ref, ob_ref, sout_ref, s_scr):
    step = pl.program_id(1)

    @pl.when(step == 0)
    def _():
        s_scr[...] = s0_ref[...]

    tb = x_f.shape[0]
    nch = tb // CHUNK
    i_idx, j_idx, same = _chunk_masks(tb)
    xor = i_idx ^ j_idx
    eye = jnp.where(i_idx == j_idx, 1.0, 0.0)
    hw = GDN_HEADS * GDN_DK

    chains = []
    for z, (x_ref, s_ref) in enumerate(((x_f, s_f), (x_b, s_b))):
        before = (j_idx <= i_idx) if z == 0 else (j_idx >= i_idx)
        incl = same & before
        strict = incl & (i_idx != j_idx)
        sm = s_ref[...]
        beta_all = _sigmoid(sm)
        g_all = -ealog_ref[...] * _softplus(sm + dtb_ref[...])
        b_sel = _dot2(beta_all, sel_ref[2 * z])
        g_sel = _dot2(g_all, sel_ref[2 * z + 1])
        c_col = _dot2_left(tri_ref[z], g_sel)
        x = x_ref[...]
        for h in range(GDN_HEADS):
            ch = dict(z=z, h=h, incl=incl)
            ch["q"] = x[:, h * GDN_DK:(h + 1) * GDN_DK]
            ch["k"] = x[:, hw + h * GDN_DK:hw + (h + 1) * GDN_DK]
            bc = b_sel[:, h * GDN_DK:(h + 1) * GDN_DK]
            cc = c_col[:, h * GDN_DK:(h + 1) * GDN_DK]
            ch["cc"] = cc
            kb = ch["k"] * bc
            ch["vb"] = (x[:, 2 * hw + h * GDN_DV:2 * hw + (h + 1) * GDN_DV] * bc).astype(BF16)
            ch["kbg"] = (kb * jnp.exp(cc)).astype(BF16)
            ch["k_bf"] = ch["k"].astype(BF16)
            c_row = cc.T[0:1, :]
            dm = jnp.concatenate([cc] * (tb // GDN_DK), axis=1) - c_row
            ch["gamma"] = jnp.where(incl, jnp.exp(jnp.where(incl, dm, 0.0)), 0.0)
            kk = lax.dot_general(kb.astype(BF16), ch["k_bf"], _NT, preferred_element_type=F32)
            ch["l_mat"] = jnp.where(strict, kk * ch["gamma"], 0.0)
            ch["t_inv"] = eye - jnp.where(xor == 1, ch["l_mat"], 0.0)
            chains.append(ch)

    s = 2
    while s < CHUNK:
        lvl = (xor >= s) & (xor < 2 * s)
        for ch in chains:
            ch["t_bf"] = ch["t_inv"].astype(BF16)
            ch["m1"] = _dot(ch["t_bf"], jnp.where(lvl, ch["l_mat"], 0.0).astype(BF16)).astype(BF16)
        for ch in chains:
            ch["t_inv"] = ch["t_inv"] - _dot(ch["m1"], ch["t_bf"])
        s *= 2

    for ch in chains:
        t_bf = ch["t_inv"].astype(BF16)
        ch["u"] = _dot(t_bf, ch["vb"])
        ch["w"] = _dot(t_bf, ch["kbg"]).astype(BF16)
        qk = lax.dot_general(ch["q"].astype(BF16), ch["k_bf"], _NT, preferred_element_type=F32)
        ch["a_int"] = jnp.where(ch["incl"], qk * ch["gamma"], 0.0).astype(BF16)
        ch["state"] = s_scr[ch["z"], ch["h"]]
        ch["outs"] = [None] * nch

    zeros = jnp.zeros((CHUNK, GDN_DV), BF16)
    for c_step in range(nch):
        for ch in chains:
            c = c_step if ch["z"] == 0 else nch - 1 - c_step
            last = CHUNK - 1 if ch["z"] == 0 else 0
            r0 = c * CHUNK
            cc_c = ch["cc"][r0:r0 + CHUNK]
            g_last = cc_c[last:last + 1, :]
            k_dec = (ch["k"][r0:r0 + CHUNK] * jnp.exp(g_last - cc_c)).astype(BF16)
            q_dec = (ch["q"][r0:r0 + CHUNK] * jnp.exp(cc_c)).astype(BF16)
            s_bf = ch["state"].astype(BF16)
            v_new = ch["u"][r0:r0 + CHUNK] - _dot(ch["w"][r0:r0 + CHUNK], s_bf)
            v_new_bf = v_new.astype(BF16)
            v_pad = jnp.concatenate([v_new_bf if cc_i == c else zeros for cc_i in range(nch)], axis=0)
            ch["outs"][c] = _dot(q_dec, s_bf) + _dot(ch["a_int"][r0:r0 + CHUNK], v_pad)
            ch["state"] = (ch["state"] * jnp.exp(g_last)
                           + lax.dot_general(k_dec, v_new_bf, _TN, preferred_element_type=F32))

    for z, o_ref in enumerate((of_ref, ob_ref)):
        mine = [ch for ch in chains if ch["z"] == z]
        o_ref[...] = jnp.concatenate([jnp.concatenate(ch["outs"], axis=0) for ch in mine], axis=1)
        for ch in mine:
            s_scr[z, ch["h"]] = ch["state"]

    @pl.when(step == pl.num_programs(1) - 1)
    def _():
        sout_ref[...] = s_scr[...]


def _gdn(xc, ps, t, bsz, sel, ealog, dtb, tri, s0):
    tb = SCAN_TB
    nblk = t // tb
    width = xc.shape[1]
    fwd = lambda b, i: b * nblk + i
    bwd = lambda b, i: b * nblk + (nblk - 1 - i)
    return pl.pallas_call(
        _gdn_kernel,
        out_shape=(jax.ShapeDtypeStruct((bsz * t, MIX_HW), F32), jax.ShapeDtypeStruct((bsz * t, MIX_HW), F32),
                   jax.ShapeDtypeStruct((bsz, 2, GDN_HEADS, GDN_DK, GDN_DV), F32)),
        grid=(bsz, nblk),
        in_specs=[pl.BlockSpec((tb, width), lambda b, i: (fwd(b, i), 0)),
                  pl.BlockSpec((tb, SMALL_W), lambda b, i: (fwd(b, i), 0)),
                  pl.BlockSpec((tb, width), lambda b, i: (bwd(b, i), 0)),
                  pl.BlockSpec((tb, SMALL_W), lambda b, i: (bwd(b, i), 0)),
                  pl.BlockSpec((4, SMALL_W, MIX_HW), lambda b, i: (0, 0, 0)),
                  pl.BlockSpec((1, SMALL_W), lambda b, i: (0, 0)),
                  pl.BlockSpec((1, SMALL_W), lambda b, i: (0, 0)),
                  pl.BlockSpec((2, tb, tb), lambda b, i: (0, 0, 0)),
                  pl.BlockSpec((None, 2, GDN_HEADS, GDN_DK, GDN_DV), lambda b, i: (b, 0, 0, 0, 0))],
        out_specs=(pl.BlockSpec((tb, MIX_HW), lambda b, i: (fwd(b, i), 0)),
                   pl.BlockSpec((tb, MIX_HW), lambda b, i: (bwd(b, i), 0)),
                   pl.BlockSpec((None, 2, GDN_HEADS, GDN_DK, GDN_DV), lambda b, i: (b, 0, 0, 0, 0))),
        scratch_shapes=[pltpu.VMEM((2, GDN_HEADS, GDN_DK, GDN_DV), F32)],
        compiler_params=_cparams(2),
        name="gdn_scan",
    )(xc, ps, xc, ps, sel, ealog, dtb, tri, s0)


def _layer_norm_rows(z, gain, bias):
    mu = jnp.mean(z, axis=-1, keepdims=True)
    zc = z - mu
    var = jnp.mean(zc * zc, axis=-1, keepdims=True)
    return zc * lax.rsqrt(var + 1e-5) * gain + bias


def _mixer_finish(o, gate, gain, n_heads, dv):
    outs = []
    for h in range(n_heads):
        oh = o[:, h * dv:(h + 1) * dv]
        oh = oh * lax.rsqrt(jnp.mean(oh * oh, axis=-1, keepdims=True) + 1e-6) * gain
        gh = gate[:, h * dv:(h + 1) * dv]
        outs.append((oh * (gh * _sigmoid(gh))).astype(BF16))
    return outs


def _outproj_kernel(glaf_ref, glab_ref, gdnf_ref, gdnb_ref, g_ref, z_ref, att_ref, x_ref, gate_ref, w_ref,
                    ngla_ref, ngdn_ref, gain_ref, bias_ref, o_ref):
    parts = _mixer_finish(glaf_ref[...] + glab_ref[...], g_ref[...], ngla_ref[...], GLA_HEADS, GLA_DV)
    parts += _mixer_finish(gdnf_ref[...] + gdnb_ref[...], z_ref[...], ngdn_ref[...], GDN_HEADS, GDN_DV)
    parts.append(att_ref[...])
    y = _dot(jnp.concatenate(parts, axis=1), w_ref[...])
    z = DEEPNORM_ALPHA * x_ref[...] + gate_ref[...] * y
    o_ref[...] = _layer_norm_rows(z, gain_ref[...], bias_ref[...])


def _outproj(gla_f, gla_b, gdn_f, gdn_b, p, att, x2d, mods, row_of_tile, w_out, n_gla, n_gdn, gain, bias, tm):
    rows, d = x2d.shape
    g_col = _OFF["gla_g"] // MIX_HW
    z_col = _OFF["gdn_z"] // MIX_HW
    mix = lambda: pl.BlockSpec((tm, MIX_HW), lambda i: (i, 0))
    vec = lambda n: pl.BlockSpec((1, n), lambda i: (0, 0))
    return pl.pallas_call(
        _outproj_kernel,
        out_shape=jax.ShapeDtypeStruct((rows, d), F32),
        grid=(rows // tm,),
        in_specs=[mix(), mix(), mix(), mix(),
                  pl.BlockSpec((tm, MIX_HW), lambda i: (i, g_col)),
                  pl.BlockSpec((tm, MIX_HW), lambda i: (i, z_col)),
                  pl.BlockSpec((tm, att.shape[1]), lambda i: (i, 0)),
                  pl.BlockSpec((tm, d), lambda i: (i, 0)),
                  pl.BlockSpec((None, 1, d), lambda i: (row_of_tile(i) * N_MOD + 2, 0, 0)),
                  pl.BlockSpec((w_out.shape[0], d), lambda i: (0, 0)),
                  vec(GLA_DV), vec(GDN_DV), vec(d), vec(d)],
        out_specs=pl.BlockSpec((tm, d), lambda i: (i, 0)),
        compiler_params=_cparams(1),
        name="outproj_ln",
    )(gla_f, gla_b, gdn_f, gdn_b, p, p, att, x2d, mods, w_out, n_gla.reshape(1, -1), n_gdn.reshape(1, -1),
      gain.reshape(1, d), bias.reshape(1, d))


def _router_kernel(x_ref, shift_ref, scale_ref, rh_ref, rl_ref, h_ref, lg_ref):
    h = x_ref[...] * (1.0 + scale_ref[...]) + shift_ref[...]
    hi, lo = _split_bf16(h)
    h_ref[...] = h
    rh = rh_ref[...]
    lg_ref[...] = _dot(hi, rh) + _dot(lo, rh) + _dot(hi, rl_ref[...])


def _router(x2d, mods, row_of_tile, r_hi, r_lo, tm):
    rows, d = x2d.shape
    return pl.pallas_call(
        _router_kernel,
        out_shape=(jax.ShapeDtypeStruct((rows, d), F32), jax.ShapeDtypeStruct((rows, 128), F32)),
        grid=(rows // tm,),
        in_specs=[pl.BlockSpec((tm, d), lambda i: (i, 0)),
                  pl.BlockSpec((None, 1, d), lambda i: (row_of_tile(i) * N_MOD + 3, 0, 0)),
                  pl.BlockSpec((None, 1, d), lambda i: (row_of_tile(i) * N_MOD + 4, 0, 0)),
                  pl.BlockSpec((d, 128), lambda i: (0, 0)),
                  pl.BlockSpec((d, 128), lambda i: (0, 0))],
        out_specs=(pl.BlockSpec((tm, d), lambda i: (i, 0)), pl.BlockSpec((tm, 128), lambda i: (i, 0))),
        compiler_params=_cparams(1),
        name="router",
    )(x2d, mods, mods, r_hi, r_lo)


GATHER_STEPS = 4


def _row_copy(h_hbm, land, sem, src_row, dst_row):
    return pltpu.make_async_copy(h_hbm.at[pl.ds(src_row, 1), :], land.at[pl.ds(dst_row, 1), :], sem)


def _ffn_kernel(idx_ref, h_hbm, g_ref, w1_ref, w3_ref, w2_ref, o_ref, land, x_scr, h_scr, sem, *, nf):
    e = pl.program_id(0)
    s = pl.program_id(1)
    tf = w1_ref.shape[1]
    n_rows = land.shape[0]
    per = n_rows // GATHER_STEPS

    @pl.when(s == 0)
    def _():
        @pl.when(e == 0)
        def _():
            def issue(r, carry):
                _row_copy(h_hbm, land, sem, idx_ref[0, r], r).start()
                return carry
            lax.fori_loop(0, n_rows, issue, 0)

        def wait(r, carry):
            _row_copy(h_hbm, land, sem, 0, r).wait()
            return carry
        lax.fori_loop(0, n_rows, wait, 0)
        x_scr[...] = land[...].astype(BF16)

    @pl.when((s >= 1) & (s <= GATHER_STEPS) & (e + 1 < pl.num_programs(0)))
    def _():
        base = (s - 1) * per
        for i in range(per):
            _row_copy(h_hbm, land, sem, idx_ref[e + 1, base + i], base + i).start()

    @pl.when(s < nf)
    def _():
        x = x_scr[...]
        a = _dot(x, w1_ref[...].astype(BF16))
        u = _dot(x, w3_ref[...].astype(BF16))
        hmid = (a * _sigmoid(a) * u).astype(BF16)
        for f in range(nf):
            @pl.when(s == f)
            def _():
                h_scr[:, f * tf:(f + 1) * tf] = hmid

    @pl.when(s >= nf)
    def _():
        o_ref[...] = _dot(h_scr[...], w2_ref[...].astype(BF16)) * g_ref[...]


def _ffn(h_all, rows_idx, gates, w1, w3, w2, layer, tf=512, td=256):
    n_e, r = rows_idx.shape
    d = h_all.shape[1]
    ff = w1.shape[3]
    nf = ff // tf
    assert r % GATHER_STEPS == 0 and nf + d // td > GATHER_STEPS
    up = lambda e, s, idx: (layer, e, 0, jnp.minimum(s, nf - 1))
    down = lambda e, s, idx: (layer, e, 0, jnp.maximum(s - nf, 0))
    grid_spec = pltpu.PrefetchScalarGridSpec(
        num_scalar_prefetch=1,
        grid=(n_e, nf + d // td),
        in_specs=[pl.BlockSpec(memory_space=pl.ANY),
                  pl.BlockSpec((None, r, 1), lambda e, s, idx: (e, 0, 0)),
                  pl.BlockSpec((None, None, d, tf), up),
                  pl.BlockSpec((None, None, d, tf), up),
                  pl.BlockSpec((None, None, ff, td), down)],
        out_specs=pl.BlockSpec((None, r, td), lambda e, s, idx: (e, 0, jnp.maximum(s - nf, 0))),
        scratch_shapes=[pltpu.VMEM((r, d), F32), pltpu.VMEM((r, d), BF16), pltpu.VMEM((r, ff), BF16),
                        pltpu.SemaphoreType.DMA(())],
    )
    return pl.pallas_call(
        functools.partial(_ffn_kernel, nf=nf),
        out_shape=jax.ShapeDtypeStruct((n_e, r, d), F32),
        grid_spec=grid_spec,
        compiler_params=_cparams(2),
        name="expert_ffn",
    )(rows_idx, h_all, gates, w1, w3, w2)


def _resln_kernel(x_ref, f_ref, gate_ref, gain_ref, bias_ref, o_ref):
    z = DEEPNORM_ALPHA * x_ref[...] + gate_ref[...] * f_ref[...]
    o_ref[...] = _layer_norm_rows(z, gain_ref[...], bias_ref[...])


def _resln(x2d, f2d, f_row0, mods, row_of_tile, gain, bias, tm):
    rows, d = x2d.shape
    f_blk0 = f_row0 // tm
    return pl.pallas_call(
        _resln_kernel,
        out_shape=jax.ShapeDtypeStruct((rows, d), F32),
        grid=(rows // tm,),
        in_specs=[pl.BlockSpec((tm, d), lambda i: (i, 0)),
                  pl.BlockSpec((tm, d), lambda i: (i + f_blk0, 0)),
                  pl.BlockSpec((None, 1, d), lambda i: (row_of_tile(i) * N_MOD + 5, 0, 0)),
                  pl.BlockSpec((1, d), lambda i: (0, 0)),
                  pl.BlockSpec((1, d), lambda i: (0, 0))],
        out_specs=pl.BlockSpec((tm, d), lambda i: (i, 0)),
        compiler_params=_cparams(1),
        name="residual_ln",
    )(x2d, f2d, mods, gain.reshape(1, d), bias.reshape(1, d))


def _rope_tables(n_tokens):
    rows = n_tokens // GRID_W
    row = jnp.broadcast_to(jnp.arange(rows)[:, None], (rows, GRID_W)).reshape(-1).astype(F32)
    col = jnp.broadcast_to(jnp.arange(GRID_W)[None, :], (rows, GRID_W)).reshape(-1).astype(F32)
    half = HEAD_DIM // 2
    inv = ROPE_THETA ** (-jnp.arange(0, half, 2, dtype=F32) / half)
    a_row = row[:, None] * inv
    a_col = col[:, None] * inv
    cos_t = jnp.concatenate([jnp.cos(a_row), jnp.cos(a_row), jnp.cos(a_col), jnp.cos(a_col)], axis=-1)
    sin_t = jnp.concatenate([-jnp.sin(a_row), jnp.sin(a_row), -jnp.sin(a_col), jnp.sin(a_col)], axis=-1)
    return cos_t, sin_t


def _permute_w_in(w):
    main = jnp.concatenate([w[:, _SRC[n][0]:_SRC[n][0] + _SRC[n][1]] for n in _MAIN_ORDER], axis=1)
    small = jnp.concatenate([w[:, _SRC[n][0]:_SRC[n][0] + _SRC[n][1]] for n in _SMALL_ORDER], axis=1)
    small = jnp.pad(small, ((0, 0), (0, SMALL_W - small.shape[1])))
    return main, small


def _scan_tri(tb):
    i = jnp.arange(tb)[:, None]
    j = jnp.arange(tb)[None, :]
    same = (i // CHUNK) == (j // CHUNK)
    return jnp.stack([same & (j <= i), same & (j >= i)]).astype(BF16)


def _gla_params(w_up, b_up):
    hk = GLA_HEADS * GLA_DK
    w = jnp.zeros((2, SMALL_W, hk), F32)
    for z in range(2):
        r0 = _OFF["gla_r"] + z * GLA_RANK
        w = w.at[z, r0:r0 + GLA_RANK].set(w_up[z])
    hi, lo = _split_bf16(w)
    return jnp.stack([hi, lo], axis=1), b_up.reshape(2, 1, hk)


def _gdn_params(a_log, dt_bias):
    sel = jnp.zeros((4, SMALL_W, MIX_HW), F32)
    ealog = jnp.zeros((1, SMALL_W), F32)
    dtb = jnp.zeros((1, SMALL_W), F32)
    for z in range(2):
        for h in range(GDN_HEADS):
            lane_b = _OFF["gdn_b"] + z * GDN_HEADS + h
            lane_a = _OFF["gdn_a"] + z * GDN_HEADS + h
            sel = sel.at[2 * z, lane_b, h * GDN_DK:(h + 1) * GDN_DK].set(1.0)
            sel = sel.at[2 * z + 1, lane_a, h * GDN_DK:(h + 1) * GDN_DK].set(1.0)
            ealog = ealog.at[0, lane_a].set(jnp.exp(a_log[z, h]))
            dtb = dtb.at[0, lane_a].set(dt_bias[z, h])
    return sel.astype(BF16), ealog, dtb


def _moe(x2d_sets, mods, row_fns, r_hi, r_lo, w1, w3, w2, layer, gain, bias, bsz, tms):
    hs, gates, flat_idx, row0s = [], [], [], []
    row0 = 0
    for x2d, row_fn, tm in zip(x2d_sets, row_fns, tms):
        h, lg = _router(x2d, mods, row_fn, r_hi, r_lo, tm)
        t = x2d.shape[0] // bsz
        cap = EC_FACTOR * t // N_EXPERTS
        aff = jax.nn.softmax(lg[:, :N_EXPERTS].reshape(bsz, t, N_EXPERTS), axis=-1).transpose(0, 2, 1)
        gate, idx = lax.top_k(aff, cap)
        idx = idx + (row0 + jnp.arange(bsz) * t)[:, None, None]
        hs.append(h)
        gates.append(gate.transpose(1, 0, 2).reshape(N_EXPERTS, bsz * cap))
        flat_idx.append(idx.transpose(1, 0, 2).reshape(N_EXPERTS, bsz * cap))
        row0s.append(row0)
        row0 += x2d.shape[0]
    h_all = jnp.concatenate(hs, axis=0) if len(hs) > 1 else hs[0]
    fi = jnp.concatenate(flat_idx, axis=1) if len(hs) > 1 else flat_idx[0]
    gs = jnp.concatenate(gates, axis=1) if len(hs) > 1 else gates[0]
    y = _ffn(h_all, fi.astype(jnp.int32), gs[..., None], w1, w3, w2, layer)
    f_all = jnp.zeros((row0, h_all.shape[1]), F32).at[fi.reshape(-1)].add(y.reshape(-1, y.shape[-1]))
    return [_resln(x2d, f_all, r0, mods, row_fn, gain, bias, tm)
            for x2d, row_fn, tm, r0 in zip(x2d_sets, row_fns, tms, row0s)]


def kernel(x, c, ctx, c_ctx, w_ada, b_ada, w_in, w_out, gla_w_up, gla_b_up, gla_norm, gdn_conv, gdn_a_log,
           gdn_dt_bias, gdn_norm, attn_qk_norm, ln_gain, ln_bias, router, w1, w3, w2):
    bsz, n_lat, d = x.shape
    n_ctx = ctx.shape[1]
    depth = w_ada.shape[0]
    cos_t, sin_t = _rope_tables(n_lat)
    tri = _scan_tri(SCAN_TB)

    cond8 = jnp.zeros((8, d), F32).at[:bsz].set(c).at[bsz].set(c_ctx)
    mods_all = _ada(cond8, w_ada, b_ada)

    tm_lat, tm_ctx = 512, n_ctx
    lat_row = lambda i: i // (n_lat // tm_lat)
    ctx_row = lambda i: bsz + 0 * i
    tm_lat_ln = 256
    lat_row_ln = lambda i: i // (n_lat // tm_lat_ln)

    x_lat = x.reshape(bsz * n_lat, d)
    x_ctx = ctx.reshape(bsz * n_ctx, d)
    for l in range(depth):
        last = l == depth - 1
        mods = mods_all[l].reshape(8 * N_MOD, 1, d)
        w_main, w_small = _permute_w_in(w_in[l])
        w_main = w_main.astype(BF16)
        ws_hi, ws_lo = _split_bf16(w_small)

        p_lat, ps_lat = _inproj(x_lat, mods, lat_row, w_main, ws_hi, ws_lo, tm_lat)
        p_ctx, ps_ctx = _inproj(x_ctx, mods, ctx_row, w_main, ws_hi, ws_lo, tm_ctx)

        wup, bup = _gla_params(gla_w_up[l], gla_b_up[l])
        gla_s0 = jnp.zeros((bsz, 2, MIX_HW, GLA_HEADS * GLA_DK), F32)
        gla_cf, gla_cb, gla_s = _gla(p_ctx, ps_ctx, n_ctx, bsz, wup, bup, tri, gla_s0)
        gla_lf, gla_lb, _ = _gla(p_lat, ps_lat, n_lat, bsz, wup, bup, tri, gla_s)

        sel, ealog, dtb = _gdn_params(gdn_a_log[l], gdn_dt_bias[l])
        gdn_s0 = jnp.zeros((bsz, 2, GDN_HEADS, GDN_DK, GDN_DV), F32)
        xc_ctx = _gdn_prep(p_ctx, n_ctx, bsz, gdn_conv[l])
        xc_lat = _gdn_prep(p_lat, n_lat, bsz, gdn_conv[l])
        gdn_cf, gdn_cb, gdn_s = _gdn(xc_ctx, ps_ctx, n_ctx, bsz, sel, ealog, dtb, tri, gdn_s0)
        gdn_lf, gdn_lb, _ = _gdn(xc_lat, ps_lat, n_lat, bsz, sel, ealog, dtb, tri, gdn_s)

        qn_lat, kv_lat = _qkv_prep(p_lat, attn_qk_norm[l], (cos_t, sin_t), n_lat, 512)
        qn_ctx, kv_ctx = _qkv_prep(p_ctx, attn_qk_norm[l], None, n_ctx, n_ctx)
        att_l = _attention(qn_lat, [(kv_ctx, n_ctx), (kv_lat, n_lat)], bsz, 256)
        w_out_b = w_out[l].astype(BF16)
        x_lat = _outproj(gla_lf, gla_lb, gdn_lf, gdn_lb, p_lat, att_l, x_lat, mods, lat_row_ln, w_out_b,
                         gla_norm[l], gdn_norm[l], ln_gain[l, 0], ln_bias[l, 0], tm_lat_ln)
        r_hi, r_lo = _split_bf16(jnp.pad(router[l], ((0, 0), (0, 128 - N_EXPERTS))))
        if not last:
            att_c = _attention(qn_ctx, [(kv_ctx, n_ctx)], bsz, n_ctx)
            x_ctx = _outproj(gla_cf, gla_cb, gdn_cf, gdn_cb, p_ctx, att_c, x_ctx, mods, ctx_row, w_out_b,
                             gla_norm[l], gdn_norm[l], ln_gain[l, 0], ln_bias[l, 0], n_ctx)
            x_lat, x_ctx = _moe([x_lat, x_ctx], mods, [lat_row_ln, ctx_row], r_hi, r_lo, w1, w3, w2, l,
                                ln_gain[l, 1], ln_bias[l, 1], bsz, [tm_lat_ln, n_ctx])
        else:
            (x_lat,) = _moe([x_lat], mods, [lat_row_ln], r_hi, r_lo, w1, w3, w2, l,
                            ln_gain[l, 1], ln_bias[l, 1], bsz, [tm_lat_ln])
    return x_lat.reshape(bsz, n_lat, d)
```

```python
import functools

import jax
import jax.numpy as jnp
from jax import lax
from jax.experimental import pallas as pl
from jax.experimental.pallas import tpu as pltpu

F32 = jnp.float32
BF16 = jnp.bfloat16

D_MODEL = 2048
DEPTH = 2
GRID_W = 64
HEAD_DIM = 128
CHUNK = 64
GLA_HEADS = 4
GLA_DK = 64
GLA_DV = 128
GLA_RANK = 16
GLA_GATE_NORM = 16.0
GDN_HEADS = 4
GDN_DK = 128
GDN_DV = 128
GDN_CONV = 5
ATTN_HEADS = 8
ATTN_KV_HEADS = 2
ATTN_GROUP = ATTN_HEADS // ATTN_KV_HEADS
ROPE_THETA = 10000.0
N_EXPERTS = 16
EC_FACTOR = 2
N_MOD = 6
DEEPNORM_ALPHA = (2 * DEPTH) ** 0.25

_SRC = dict(gla_q=(0, 256), gla_k=(256, 256), gla_v=(512, 512), gla_g=(1024, 512), gla_r=(1536, 32),
            gdn_q=(1568, 512), gdn_k=(2080, 512), gdn_v=(2592, 512), gdn_z=(3104, 512), gdn_b=(3616, 8),
            gdn_a=(3624, 8), att_q=(3632, 1024), att_k=(4656, 256), att_v=(4912, 256))
_MAIN_ORDER = ("att_q", "gla_v", "gdn_q", "gdn_k", "gdn_v", "gla_g", "gdn_z", "gla_q", "gla_k", "att_k", "att_v")
_SMALL_ORDER = ("gla_r", "gdn_b", "gdn_a")
_OFF = {}
_o = 0
for _n in _MAIN_ORDER:
    _OFF[_n] = _o
    _o += _SRC[_n][1]
MAIN_W = _o
_o = 0
for _n in _SMALL_ORDER:
    _OFF[_n] = _o
    _o += _SRC[_n][1]
SMALL_W = 128
MIX_HW = 512
SCAN_TB = 256

VMEM_LIMIT = 56 * 1024 * 1024

_NT = (((1,), (1,)), ((), ()))
_TN = (((0,), (0,)), ((), ()))


def _cparams(n_axes):
    return pltpu.CompilerParams(dimension_semantics=("arbitrary",) * n_axes, vmem_limit_bytes=VMEM_LIMIT)


def _split_bf16(a):
    hi = a.astype(BF16)
    lo = (a - hi.astype(F32)).astype(BF16)
    return hi, lo


def _dot(a, b):
    return jnp.dot(a, b, preferred_element_type=F32)


def _dot2(a, m):
    hi, lo = _split_bf16(a)
    return _dot(hi, m) + _dot(lo, m)


def _dot2_left(m, a):
    hi, lo = _split_bf16(a)
    return _dot(m, hi) + _dot(m, lo)


def _sigmoid(x):
    return 1.0 / (1.0 + jnp.exp(-x))


def _softplus(x):
    return jnp.maximum(x, 0.0) + jnp.log1p(jnp.exp(-jnp.abs(x)))


def _ada_kernel(c_ref, w_ref, b_ref, o_ref):
    c = c_ref[...]
    s = (c * _sigmoid(c)).astype(BF16)
    o_ref[...] = _dot(s, w_ref[...].astype(BF16)) + b_ref[...]


def _ada(cond8, w_ada, b_ada, tn=1024):
    depth, d, n = w_ada.shape
    return pl.pallas_call(
        _ada_kernel,
        out_shape=jax.ShapeDtypeStruct((depth, 8, n), F32),
        grid=(depth, n // tn),
        in_specs=[pl.BlockSpec((8, d), lambda l, j: (0, 0)),
                  pl.BlockSpec((None, d, tn), lambda l, j: (l, 0, j)),
                  pl.BlockSpec((None, 1, tn), lambda l, j: (l, 0, j))],
        out_specs=pl.BlockSpec((None, 8, tn), lambda l, j: (l, 0, j)),
        compiler_params=_cparams(2),
        name="ada",
    )(cond8, w_ada, b_ada.reshape(depth, 1, n))


def _inproj_kernel(x_ref, shift_ref, scale_ref, w_ref, wsh_ref, wsl_ref, o_ref, os_ref, h_ref):
    @pl.when(pl.program_id(1) == 0)
    def _():
        h = x_ref[...] * (1.0 + scale_ref[...]) + shift_ref[...]
        hi, lo = _split_bf16(h)
        h_ref[...] = hi
        wsh = wsh_ref[...]
        os_ref[...] = _dot(hi, wsh) + _dot(lo, wsh) + _dot(hi, wsl_ref[...])

    o_ref[...] = _dot(h_ref[...], w_ref[...])


def _inproj(x2d, mods, row_of_tile, w_main, ws_hi, ws_lo, tm, tn=1024):
    rows, d = x2d.shape
    n = w_main.shape[1]
    return pl.pallas_call(
        _inproj_kernel,
        out_shape=(jax.ShapeDtypeStruct((rows, n), F32), jax.ShapeDtypeStruct((rows, SMALL_W), F32)),
        grid=(rows // tm, n // tn),
        in_specs=[pl.BlockSpec((tm, d), lambda i, j: (i, 0)),
                  pl.BlockSpec((None, 1, d), lambda i, j: (row_of_tile(i) * N_MOD + 0, 0, 0)),
                  pl.BlockSpec((None, 1, d), lambda i, j: (row_of_tile(i) * N_MOD + 1, 0, 0)),
                  pl.BlockSpec((d, tn), lambda i, j: (0, j)),
                  pl.BlockSpec((d, SMALL_W), lambda i, j: (0, 0)),
                  pl.BlockSpec((d, SMALL_W), lambda i, j: (0, 0))],
        out_specs=(pl.BlockSpec((tm, tn), lambda i, j: (i, j)),
                   pl.BlockSpec((tm, SMALL_W), lambda i, j: (i, 0))),
        scratch_shapes=[pltpu.VMEM((tm, d), BF16)],
        compiler_params=_cparams(2),
        name="inproj",
    )(x2d, mods, mods, w_main, ws_hi, ws_lo)


def _rms_rope(x, gain, cos, sin):
    y = x * lax.rsqrt(jnp.mean(x * x, axis=-1, keepdims=True) + 1e-6) * gain
    if cos is not None:
        lane = lax.broadcasted_iota(jnp.int32, y.shape, 1)
        partner = jnp.where((lane % 64) < 32, pltpu.roll(y, 96, 1), pltpu.roll(y, 32, 1))
        y = y * cos + partner * sin
    return y


KV_W = ATTN_KV_HEADS * HEAD_DIM + ATTN_KV_HEADS * 2 * HEAD_DIM
LOG2E = 1.4426950408889634


def _qkv_prep_kernel(*refs, rope):
    if rope:
        q_ref, kv_ref, gain_ref, cos_ref, sin_ref, qo_ref, kvo_ref = refs
        cos, sin = cos_ref[...], sin_ref[...]
    else:
        q_ref, kv_ref, gain_ref, qo_ref, kvo_ref = refs
        cos = sin = None
    q = q_ref[...]
    q_scale = (HEAD_DIM ** -0.5) * LOG2E
    qo_ref[...] = jnp.concatenate(
        [(_rms_rope(q[:, g * HEAD_DIM:(g + 1) * HEAD_DIM], gain_ref[0:1, :], cos, sin) * q_scale).astype(BF16)
         for g in range(ATTN_HEADS)], axis=1)
    kv = kv_ref[...]
    outs = [_rms_rope(kv[:, h * HEAD_DIM:(h + 1) * HEAD_DIM], gain_ref[1:2, :], cos, sin).astype(BF16)
            for h in range(ATTN_KV_HEADS)]
    ones = jnp.ones((kv.shape[0], HEAD_DIM), BF16)
    for h in range(ATTN_KV_HEADS):
        outs += [kv[:, (ATTN_KV_HEADS + h) * HEAD_DIM:(ATTN_KV_HEADS + h + 1) * HEAD_DIM].astype(BF16), ones]
    kvo_ref[...] = jnp.concatenate(outs, axis=1)


def _qkv_prep(p, gain, rope_tabs, n_per_sample, tm):
    rows = p.shape[0]
    qw = ATTN_HEADS * HEAD_DIM
    kvw = 2 * ATTN_KV_HEADS * HEAD_DIM
    per = n_per_sample // tm
    in_specs = [pl.BlockSpec((tm, qw), lambda i: (i, _OFF["att_q"] // qw)),
                pl.BlockSpec((tm, kvw), lambda i: (i, _OFF["att_k"] // kvw)),
                pl.BlockSpec((2, HEAD_DIM), lambda i: (0, 0))]
    args = [p, p, gain]
    if rope_tabs is not None:
        in_specs += [pl.BlockSpec((tm, HEAD_DIM), lambda i: (i % per, 0))] * 2
        args += list(rope_tabs)
    return pl.pallas_call(
        functools.partial(_qkv_prep_kernel, rope=rope_tabs is not None),
        out_shape=(jax.ShapeDtypeStruct((rows, qw), BF16), jax.ShapeDtypeStruct((rows, KV_W), BF16)),
        grid=(rows // tm,),
        in_specs=in_specs,
        out_specs=(pl.BlockSpec((tm, qw), lambda i: (i, 0)), pl.BlockSpec((tm, KV_W), lambda i: (i, 0))),
        compiler_params=_cparams(1),
        name="qkv_prep",
    )(*args)


def _attn_kernel(*refs, n_seg):
    q_ref = refs[0]
    k_refs = refs[1:1 + 2 * n_seg:2]
    v_refs = refs[2:2 + 2 * n_seg:2]
    o_ref = refs[1 + 2 * n_seg]
    outs = []
    for g in range(ATTN_GROUP):
        qg = q_ref[:, g * HEAD_DIM:(g + 1) * HEAD_DIM]
        s = jnp.concatenate([lax.dot_general(qg, k_ref[...], _NT, preferred_element_type=F32) for k_ref in k_refs],
                            axis=1)
        p = jnp.exp2(s - jnp.max(s, axis=-1, keepdims=True)).astype(BF16)
        acc = None
        off = 0
        for v_ref in v_refs:
            n = v_ref.shape[0]
            part = _dot(p[:, off:off + n], v_ref[...])
            acc = part if acc is None else acc + part
            off += n
        outs.append((acc[:, :HEAD_DIM] / acc[:, HEAD_DIM:]).astype(o_ref.dtype))
    o_ref[...] = jnp.concatenate(outs, axis=1)


def _attention(qn, kv_segs, bsz, tq):
    rows = qn.shape[0]
    nq = rows // bsz // tq
    gw = ATTN_GROUP * HEAD_DIM
    in_specs = [pl.BlockSpec((tq, gw), lambda b, h, i: (b * nq + i, h))]
    args = [qn]
    for arr, n in kv_segs:
        in_specs.append(pl.BlockSpec((n, HEAD_DIM), lambda b, h, i: (b, h)))
        in_specs.append(pl.BlockSpec((n, 2 * HEAD_DIM), lambda b, h, i: (b, 1 + h)))
        args += [arr, arr]
    return pl.pallas_call(
        functools.partial(_attn_kernel, n_seg=len(kv_segs)),
        out_shape=jax.ShapeDtypeStruct((rows, ATTN_HEADS * HEAD_DIM), BF16),
        grid=(bsz, ATTN_KV_HEADS, nq),
        in_specs=in_specs,
        out_specs=pl.BlockSpec((tq, gw), lambda b, h, i: (b * nq + i, h)),
        compiler_params=_cparams(3),
        name="attention",
    )(*args)


def _chunk_masks(tb):
    i = lax.broadcasted_iota(jnp.int32, (tb, tb), 0)
    j = lax.broadcasted_iota(jnp.int32, (tb, tb), 1)
    same = (i // CHUNK) == (j // CHUNK)
    return i, j, same


def _gla_kernel(qk_f, v_f, s_f, qk_b, v_b, s_b, wup_ref, bup_ref, tri_ref, st0_ref,
                of_ref, ob_ref, stout_ref, st_scr):
    step = pl.program_id(1)

    @pl.when(step == 0)
    def _():
        st_scr[...] = st0_ref[...]

    tb = qk_f.shape[0]
    nch = tb // CHUNK
    hk = GLA_HEADS * GLA_DK
    lane = lax.broadcasted_iota(jnp.int32, (CHUNK, hk), 1)
    head_of_lane = lane // GLA_DK
    r4 = lax.broadcasted_iota(jnp.int32, (GLA_HEADS * CHUNK, CHUNK), 0) % CHUNK
    c4 = lax.broadcasted_iota(jnp.int32, (GLA_HEADS * CHUNK, CHUNK), 1)
    bd_r = lax.broadcasted_iota(jnp.int32, (MIX_HW, hk), 0) // GLA_DV
    bd_c = lax.broadcasted_iota(jnp.int32, (MIX_HW, hk), 1) // GLA_DK
    bd_mask = bd_r == bd_c

    for z, (qk_ref, v_ref, s_ref, o_ref) in enumerate(((qk_f, v_f, s_f, of_ref), (qk_b, v_b, s_b, ob_ref))):
        qk = qk_ref[...]
        q = qk[:, :hk] * (GLA_DK ** -0.5)
        k = qk[:, hk:]
        v = v_ref[...].astype(BF16)
        hi, lo = _split_bf16(s_ref[...])
        wh = wup_ref[z, 0]
        logit = _dot(hi, wh) + _dot(lo, wh) + _dot(hi, wup_ref[z, 1]) + bup_ref[z]
        log_a = -_softplus(-logit) / GLA_GATE_NORM
        cum = _dot2_left(tri_ref[z], log_a)
        intra_mask = (c4 <= r4) if z == 0 else (c4 >= r4)
        last = CHUNK - 1 if z == 0 else 0
        st = st_scr[z]
        outs = [None] * nch
        for c in (range(nch) if z == 0 else range(nch - 1, -1, -1)):
            r0 = c * CHUNK
            cum_c = cum[r0:r0 + CHUNK]
            tot = cum_c[last:last + 1, :]
            q_dec = q[r0:r0 + CHUNK] * jnp.exp(cum_c)
            k_c = k[r0:r0 + CHUNK]
            k_inv = (k_c * jnp.exp(-cum_c)).astype(BF16)
            k_dec = (k_c * jnp.exp(tot - cum_c)).astype(BF16)
            q4 = jnp.concatenate([jnp.where(head_of_lane == h, q_dec, 0.0) for h in range(GLA_HEADS)],
                                 axis=0).astype(BF16)
            a4 = lax.dot_general(q4, k_inv, _NT, preferred_element_type=F32)
            a4 = jnp.where(intra_mask, a4, 0.0).astype(BF16)
            v_c = v[r0:r0 + CHUNK]
            o4 = _dot(a4, v_c)
            o_intra = jnp.concatenate(
                [o4[h * CHUNK:(h + 1) * CHUNK, h * GLA_DV:(h + 1) * GLA_DV] for h in range(GLA_HEADS)], axis=1)
            o_inter = lax.dot_general(q_dec.astype(BF16), st.astype(BF16), _NT, preferred_element_type=F32)
            outs[c] = o_intra + o_inter
            kv_t = lax.dot_general(v_c, k_dec, _TN, preferred_element_type=F32)
            st = st * jnp.exp(tot) + jnp.where(bd_mask, kv_t, 0.0)
        o_ref[...] = jnp.concatenate(outs, axis=0)
        st_scr[z] = st

    @pl.when(step == pl.num_programs(1) - 1)
    def _():
        stout_ref[...] = st_scr[...]


def _gla(p, ps, t, bsz, wup, bup, tri, st0):
    tb = SCAN_TB
    nblk = t // tb
    qk_col = _OFF["gla_q"] // (2 * GLA_HEADS * GLA_DK)
    v_col = _OFF["gla_v"] // MIX_HW
    fwd = lambda b, i: b * nblk + i
    bwd = lambda b, i: b * nblk + (nblk - 1 - i)
    hk = GLA_HEADS * GLA_DK
    return pl.pallas_call(
        _gla_kernel,
        out_shape=(jax.ShapeDtypeStruct((bsz * t, MIX_HW), F32), jax.ShapeDtypeStruct((bsz * t, MIX_HW), F32),
                   jax.ShapeDtypeStruct((bsz, 2, MIX_HW, hk), F32)),
        grid=(bsz, nblk),
        in_specs=[pl.BlockSpec((tb, 2 * hk), lambda b, i: (fwd(b, i), qk_col)),
                  pl.BlockSpec((tb, MIX_HW), lambda b, i: (fwd(b, i), v_col)),
                  pl.BlockSpec((tb, SMALL_W), lambda b, i: (fwd(b, i), 0)),
                  pl.BlockSpec((tb, 2 * hk), lambda b, i: (bwd(b, i), qk_col)),
                  pl.BlockSpec((tb, MIX_HW), lambda b, i: (bwd(b, i), v_col)),
                  pl.BlockSpec((tb, SMALL_W), lambda b, i: (bwd(b, i), 0)),
                  pl.BlockSpec((2, 2, SMALL_W, hk), lambda b, i: (0, 0, 0, 0)),
                  pl.BlockSpec((2, 1, hk), lambda b, i: (0, 0, 0)),
                  pl.BlockSpec((2, tb, tb), lambda b, i: (0, 0, 0)),
                  pl.BlockSpec((None, 2, MIX_HW, hk), lambda b, i: (b, 0, 0, 0))],
        out_specs=(pl.BlockSpec((tb, MIX_HW), lambda b, i: (fwd(b, i), 0)),
                   pl.BlockSpec((tb, MIX_HW), lambda b, i: (bwd(b, i), 0)),
                   pl.BlockSpec((None, 2, MIX_HW, hk), lambda b, i: (b, 0, 0, 0))),
        scratch_shapes=[pltpu.VMEM((2, MIX_HW, hk), F32)],
        compiler_params=_cparams(2),
        name="gla_scan",
    )(p, p, ps, p, p, ps, wup, bup, tri, st0)


def _gdn_prep_kernel(x_ref, prev_ref, next_ref, w_ref, o_ref):
    i = pl.program_id(1)
    tb = x_ref.shape[0]
    halo = prev_ref.shape[0]
    prev = jnp.where(i > 0, prev_ref[...], 0.0)
    nxt = jnp.where(i < pl.num_programs(1) - 1, next_ref[...], 0.0)
    ext = jnp.concatenate([prev, x_ref[...], nxt], axis=0)
    w = w_ref[...]
    acc = None
    for j in range(GDN_CONV):
        shift = GDN_CONV // 2 - j
        rolled = ext if shift == 0 else pltpu.roll(ext, shift % (tb + 2 * halo), 0)
        term = rolled[halo:halo + tb] * w[j:j + 1, :]
        acc = term if acc is None else acc + term
    y = acc * _sigmoid(acc)
    qk_w = 2 * GDN_HEADS * GDN_DK
    outs = []
    for h in range(2 * GDN_HEADS):
        yh = y[:, h * GDN_DK:(h + 1) * GDN_DK]
        yh = yh * lax.rsqrt(jnp.sum(yh * yh, axis=-1, keepdims=True) + 1e-6)
        if h < GDN_HEADS:
            yh = yh * (GDN_DK ** -0.5)
        outs.append(yh)
    outs.append(y[:, qk_w:])
    o_ref[...] = jnp.concatenate(outs, axis=1)


def _gdn_prep(p, t, bsz, conv_w):
    tb = SCAN_TB
    halo = 8
    nblk = t // tb
    width = conv_w.shape[1]
    col = _OFF["gdn_q"] // width
    n_halo_blocks = bsz * t // halo
    per = tb // halo
    return pl.pallas_call(
        _gdn_prep_kernel,
        out_shape=jax.ShapeDtypeStruct((bsz * t, width), F32),
        grid=(bsz, nblk),
        in_specs=[pl.BlockSpec((tb, width), lambda b, i: (b * nblk + i, col)),
                  pl.BlockSpec((halo, width), lambda b, i: (jnp.maximum((b * nblk + i) * per - 1, 0), col)),
                  pl.BlockSpec((halo, width),
                               lambda b, i: (jnp.minimum((b * nblk + i + 1) * per, n_halo_blocks - 1), col)),
                  pl.BlockSpec((GDN_CONV, width), lambda b, i: (0, 0))],
        out_specs=pl.BlockSpec((tb, width), lambda b, i: (b * nblk + i, 0)),
        compiler_params=_cparams(2),
        name="gdn_conv",
    )(p, p, p, conv_w)


def _gdn_kernel(x_f, s_f, x_b, s_b, sel_ref, ealog_ref, dtb_ref, tri_ref, s0_ref,
                of_ref, ob_ref, sout_ref, s_scr):
    step = pl.program_id(1)

    @pl.when(step == 0)
    def _():
        s_scr[...] = s0_ref[...]

    tb = x_f.shape[0]
    nch = tb // CHUNK
    i_idx, j_idx, same = _chunk_masks(tb)
    xor = i_idx ^ j_idx
    eye = jnp.where(i_idx == j_idx, 1.0, 0.0)
    hw = GDN_HEADS * GDN_DK

    chains = []
    for z, (x_ref, s_ref) in enumerate(((x_f, s_f), (x_b, s_b))):
        before = (j_idx <= i_idx) if z == 0 else (j_idx >= i_idx)
        incl = same & before
        strict = incl & (i_idx != j_idx)
        sm = s_ref[...]
        beta_all = _sigmoid(sm)
        g_all = -ealog_ref[...] * _softplus(sm + dtb_ref[...])
        b_sel = _dot2(beta_all, sel_ref[2 * z])
        g_sel = _dot2(g_all, sel_ref[2 * z + 1])
        c_col = _dot2_left(tri_ref[z], g_sel)
        x = x_ref[...]
        for h in range(GDN_HEADS):
            ch = dict(z=z, h=h, incl=incl)
            ch["q"] = x[:, h * GDN_DK:(h + 1) * GDN_DK]
            ch["k"] = x[:, hw + h * GDN_DK:hw + (h + 1) * GDN_DK]
            bc = b_sel[:, h * GDN_DK:(h + 1) * GDN_DK]
            cc = c_col[:, h * GDN_DK:(h + 1) * GDN_DK]
            ch["cc"] = cc
            kb = ch["k"] * bc
            ch["vb"] = (x[:, 2 * hw + h * GDN_DV:2 * hw + (h + 1) * GDN_DV] * bc).astype(BF16)
            ch["kbg"] = (kb * jnp.exp(cc)).astype(BF16)
            ch["k_bf"] = ch["k"].astype(BF16)
            c_row = cc.T[0:1, :]
            dm = jnp.concatenate([cc] * (tb // GDN_DK), axis=1) - c_row
            ch["gamma"] = jnp.where(incl, jnp.exp(jnp.where(incl, dm, 0.0)), 0.0)
            kk = lax.dot_general(kb.astype(BF16), ch["k_bf"], _NT, preferred_element_type=F32)
            ch["l_mat"] = jnp.where(strict, kk * ch["gamma"], 0.0)
            ch["t_inv"] = eye - jnp.where(xor == 1, ch["l_mat"], 0.0)
            chains.append(ch)

    s = 2
    while s < CHUNK:
        lvl = (xor >= s) & (xor < 2 * s)
        for ch in chains:
            ch["t_bf"] = ch["t_inv"].astype(BF16)
            ch["m1"] = _dot(ch["t_bf"], jnp.where(lvl, ch["l_mat"], 0.0).astype(BF16)).astype(BF16)
        for ch in chains:
            ch["t_inv"] = ch["t_inv"] - _dot(ch["m1"], ch["t_bf"])
        s *= 2

    for ch in chains:
        t_bf = ch["t_inv"].astype(BF16)
        ch["u"] = _dot(t_bf, ch["vb"])
        ch["w"] = _dot(t_bf, ch["kbg"]).astype(BF16)
        qk = lax.dot_general(ch["q"].astype(BF16), ch["k_bf"], _NT, preferred_element_type=F32)
        ch["a_int"] = jnp.where(ch["incl"], qk * ch["gamma"], 0.0).astype(BF16)
        ch["state"] = s_scr[ch["z"], ch["h"]]
        ch["outs"] = [None] * nch

    zeros = jnp.zeros((CHUNK, GDN_DV), BF16)
    for c_step in range(nch):
        for ch in chains:
            c = c_step if ch["z"] == 0 else nch - 1 - c_step
            last = CHUNK - 1 if ch["z"] == 0 else 0
            r0 = c * CHUNK
            cc_c = ch["cc"][r0:r0 + CHUNK]
            g_last = cc_c[last:last + 1, :]
            k_dec = (ch["k"][r0:r0 + CHUNK] * jnp.exp(g_last - cc_c)).astype(BF16)
            q_dec = (ch["q"][r0:r0 + CHUNK] * jnp.exp(cc_c)).astype(BF16)
            s_bf = ch["state"].astype(BF16)
            v_new = ch["u"][r0:r0 + CHUNK] - _dot(ch["w"][r0:r0 + CHUNK], s_bf)
            v_new_bf = v_new.astype(BF16)
            v_pad = jnp.concatenate([v_new_bf if cc_i == c else zeros for cc_i in range(nch)], axis=0)
            ch["outs"][c] = _dot(q_dec, s_bf) + _dot(ch["a_int"][r0:r0 + CHUNK], v_pad)
            ch["state"] = (ch["state"] * jnp.exp(g_last)
                           + lax.dot_general(k_dec, v_new_bf, _TN, preferred_element_type=F32))

    for z, o_ref in enumerate((of_ref, ob_ref)):
        mine = [ch for ch in chains if ch["z"] == z]
        o_ref[...] = jnp.concatenate([jnp.concatenate(ch["outs"], axis=0) for ch in mine], axis=1)
        for ch in mine:
            s_scr[z, ch["h"]] = ch["state"]

    @pl.when(step == pl.num_programs(1) - 1)
    def _():
        sout_ref[...] = s_scr[...]


def _gdn(xc, ps, t, bsz, sel, ealog, dtb, tri, s0):
    tb = SCAN_TB
    nblk = t // tb
    width = xc.shape[1]
    fwd = lambda b, i: b * nblk + i
    bwd = lambda b, i: b * nblk + (nblk - 1 - i)
    return pl.pallas_call(
        _gdn_kernel,
        out_shape=(jax.ShapeDtypeStruct((bsz * t, MIX_HW), F32), jax.ShapeDtypeStruct((bsz * t, MIX_HW), F32),
                   jax.ShapeDtypeStruct((bsz, 2, GDN_HEADS, GDN_DK, GDN_DV), F32)),
        grid=(bsz, nblk),
        in_specs=[pl.BlockSpec((tb, width), lambda b, i: (fwd(b, i), 0)),
                  pl.BlockSpec((tb, SMALL_W), lambda b, i: (fwd(b, i), 0)),
                  pl.BlockSpec((tb, width), lambda b, i: (bwd(b, i), 0)),
                  pl.BlockSpec((tb, SMALL_W), lambda b, i: (bwd(b, i), 0)),
                  pl.BlockSpec((4, SMALL_W, MIX_HW), lambda b, i: (0, 0, 0)),
                  pl.BlockSpec((1, SMALL_W), lambda b, i: (0, 0)),
                  pl.BlockSpec((1, SMALL_W), lambda b, i: (0, 0)),
                  pl.BlockSpec((2, tb, tb), lambda b, i: (0, 0, 0)),
                  pl.BlockSpec((None, 2, GDN_HEADS, GDN_DK, GDN_DV), lambda b, i: (b, 0, 0, 0, 0))],
        out_specs=(pl.BlockSpec((tb, MIX_HW), lambda b, i: (fwd(b, i), 0)),
                   pl.BlockSpec((tb, MIX_HW), lambda b, i: (bwd(b, i), 0)),
                   pl.BlockSpec((None, 2, GDN_HEADS, GDN_DK, GDN_DV), lambda b, i: (b, 0, 0, 0, 0))),
        scratch_shapes=[pltpu.VMEM((2, GDN_HEADS, GDN_DK, GDN_DV), F32)],
        compiler_params=_cparams(2),
        name="gdn_scan",
    )(xc, ps, xc, ps, sel, ealog, dtb, tri, s0)


def _layer_norm_rows(z, gain, bias):
    mu = jnp.mean(z, axis=-1, keepdims=True)
    zc = z - mu
    var = jnp.mean(zc * zc, axis=-1, keepdims=True)
    return zc * lax.rsqrt(var + 1e-5) * gain + bias


def _mixer_finish(o, gate, gain, n_heads, dv):
    outs = []
    for h in range(n_heads):
        oh = o[:, h * dv:(h + 1) * dv]
        oh = oh * lax.rsqrt(jnp.mean(oh * oh, axis=-1, keepdims=True) + 1e-6) * gain
        gh = gate[:, h * dv:(h + 1) * dv]
        outs.append((oh * (gh * _sigmoid(gh))).astype(BF16))
    return outs


def _outproj_kernel(glaf_ref, glab_ref, gdnf_ref, gdnb_ref, g_ref, z_ref, att_ref, x_ref, gate_ref, w_ref,
                    ngla_ref, ngdn_ref, gain_ref, bias_ref, o_ref):
    parts = _mixer_finish(glaf_ref[...] + glab_ref[...], g_ref[...], ngla_ref[...], GLA_HEADS, GLA_DV)
    parts += _mixer_finish(gdnf_ref[...] + gdnb_ref[...], z_ref[...], ngdn_ref[...], GDN_HEADS, GDN_DV)
    parts.append(att_ref[...])
    y = _dot(jnp.concatenate(parts, axis=1), w_ref[...])
    z = DEEPNORM_ALPHA * x_ref[...] + gate_ref[...] * y
    o_ref[...] = _layer_norm_rows(z, gain_ref[...], bias_ref[...])


def _outproj(gla_f, gla_b, gdn_f, gdn_b, p, att, x2d, mods, row_of_tile, w_out, n_gla, n_gdn, gain, bias, tm):
    rows, d = x2d.shape
    g_col = _OFF["gla_g"] // MIX_HW
    z_col = _OFF["gdn_z"] // MIX_HW
    mix = lambda: pl.BlockSpec((tm, MIX_HW), lambda i: (i, 0))
    vec = lambda n: pl.BlockSpec((1, n), lambda i: (0, 0))
    return pl.pallas_call(
        _outproj_kernel,
        out_shape=jax.ShapeDtypeStruct((rows, d), F32),
        grid=(rows // tm,),
        in_specs=[mix(), mix(), mix(), mix(),
                  pl.BlockSpec((tm, MIX_HW), lambda i: (i, g_col)),
                  pl.BlockSpec((tm, MIX_HW), lambda i: (i, z_col)),
                  pl.BlockSpec((tm, att.shape[1]), lambda i: (i, 0)),
                  pl.BlockSpec((tm, d), lambda i: (i, 0)),
                  pl.BlockSpec((None, 1, d), lambda i: (row_of_tile(i) * N_MOD + 2, 0, 0)),
                  pl.BlockSpec((w_out.shape[0], d), lambda i: (0, 0)),
                  vec(GLA_DV), vec(GDN_DV), vec(d), vec(d)],
        out_specs=pl.BlockSpec((tm, d), lambda i: (i, 0)),
        compiler_params=_cparams(1),
        name="outproj_ln",
    )(gla_f, gla_b, gdn_f, gdn_b, p, p, att, x2d, mods, w_out, n_gla.reshape(1, -1), n_gdn.reshape(1, -1),
      gain.reshape(1, d), bias.reshape(1, d))


def _router_kernel(x_ref, shift_ref, scale_ref, rh_ref, rl_ref, h_ref, lg_ref):
    h = x_ref[...] * (1.0 + scale_ref[...]) + shift_ref[...]
    hi, lo = _split_bf16(h)
    h_ref[...] = h
    rh = rh_ref[...]
    lg_ref[...] = _dot(hi, rh) + _dot(lo, rh) + _dot(hi, rl_ref[...])


def _router(x2d, mods, row_of_tile, r_hi, r_lo, tm):
    rows, d = x2d.shape
    return pl.pallas_call(
        _router_kernel,
        out_shape=(jax.ShapeDtypeStruct((rows, d), F32), jax.ShapeDtypeStruct((rows, 128), F32)),
        grid=(rows // tm,),
        in_specs=[pl.BlockSpec((tm, d), lambda i: (i, 0)),
                  pl.BlockSpec((None, 1, d), lambda i: (row_of_tile(i) * N_MOD + 3, 0, 0)),
                  pl.BlockSpec((None, 1, d), lambda i: (row_of_tile(i) * N_MOD + 4, 0, 0)),
                  pl.BlockSpec((d, 128), lambda i: (0, 0)),
                  pl.BlockSpec((d, 128), lambda i: (0, 0))],
        out_specs=(pl.BlockSpec((tm, d), lambda i: (i, 0)), pl.BlockSpec((tm, 128), lambda i: (i, 0))),
        compiler_params=_cparams(1),
        name="router",
    )(x2d, mods, mods, r_hi, r_lo)


def _row_copy(h_hbm, land, sem, src_row, dst_row):
    return pltpu.make_async_copy(h_hbm.at[pl.ds(src_row, 1), :], land.at[pl.ds(dst_row, 1), :], sem)


def _ffn_kernel(idx_ref, h_hbm, g_ref, w1_ref, w3_ref, w2_ref, o_ref, land, x_scr, h_scr, sem, *, nf):
    e = pl.program_id(0)
    s = pl.program_id(1)
    n_e = pl.num_programs(0)
    tf = w1_ref.shape[1]
    n_rows = land.shape[0]
    per = n_rows // nf

    def wait_all_rows():
        pltpu.make_async_copy(h_hbm.at[pl.ds(0, n_rows), :], land, sem).wait()

    @pl.when(s == 0)
    def _():
        @pl.when(e == 0)
        def _():
            def issue(r, carry):
                _row_copy(h_hbm, land, sem, idx_ref[0, r], r).start()
                return carry
            lax.fori_loop(0, n_rows, issue, 0)

        wait_all_rows()
        x_scr[...] = land[...].astype(BF16)

    @pl.when(s < nf)
    def _():
        nxt = lax.rem(e + 1, n_e)
        base = s * per
        for i in range(per):
            _row_copy(h_hbm, land, sem, idx_ref[nxt, base + i], base + i).start()
        x = x_scr[...]
        a = _dot(x, w1_ref[...].astype(BF16))
        u = _dot(x, w3_ref[...].astype(BF16))
        hmid = (a * _sigmoid(a) * u).astype(BF16)
        for f in range(nf):
            @pl.when(s == f)
            def _():
                h_scr[:, f * tf:(f + 1) * tf] = hmid

    @pl.when(s >= nf)
    def _():
        o_ref[...] = _dot(h_scr[...], w2_ref[...].astype(BF16)) * g_ref[...]

    @pl.when((e == n_e - 1) & (s == pl.num_programs(1) - 1))
    def _():
        wait_all_rows()


def _ffn(h_all, rows_idx, gates, w1, w3, w2, layer, tf=512, td=256):
    n_e, r = rows_idx.shape
    d = h_all.shape[1]
    ff = w1.shape[3]
    nf = ff // tf
    assert r % nf == 0
    up = lambda e, s, idx: (layer, e, 0, jnp.minimum(s, nf - 1))
    down = lambda e, s, idx: (layer, e, 0, jnp.maximum(s - nf, 0))
    grid_spec = pltpu.PrefetchScalarGridSpec(
        num_scalar_prefetch=1,
        grid=(n_e, nf + d // td),
        in_specs=[pl.BlockSpec(memory_space=pl.ANY),
                  pl.BlockSpec((None, r, 1), lambda e, s, idx: (e, 0, 0)),
                  pl.BlockSpec((None, None, d, tf), up),
                  pl.BlockSpec((None, None, d, tf), up),
                  pl.BlockSpec((None, None, ff, td), down)],
        out_specs=pl.BlockSpec((None, r, td), lambda e, s, idx: (e, 0, jnp.maximum(s - nf, 0))),
        scratch_shapes=[pltpu.VMEM((r, d), F32), pltpu.VMEM((r, d), BF16), pltpu.VMEM((r, ff), BF16),
                        pltpu.SemaphoreType.DMA(())],
    )
    return pl.pallas_call(
        functools.partial(_ffn_kernel, nf=nf),
        out_shape=jax.ShapeDtypeStruct((n_e, r, d), F32),
        grid_spec=grid_spec,
        compiler_params=_cparams(2),
        name="expert_ffn",
    )(rows_idx, h_all, gates, w1, w3, w2)


def _resln_kernel(x_ref, f_ref, gate_ref, gain_ref, bias_ref, o_ref):
    z = DEEPNORM_ALPHA * x_ref[...] + gate_ref[...] * f_ref[...]
    o_ref[...] = _layer_norm_rows(z, gain_ref[...], bias_ref[...])


def _resln(x2d, f2d, f_row0, mods, row_of_tile, gain, bias, tm):
    rows, d = x2d.shape
    f_blk0 = f_row0 // tm
    return pl.pallas_call(
        _resln_kernel,
        out_shape=jax.ShapeDtypeStruct((rows, d), F32),
        grid=(rows // tm,),
        in_specs=[pl.BlockSpec((tm, d), lambda i: (i, 0)),
                  pl.BlockSpec((tm, d), lambda i: (i + f_blk0, 0)),
                  pl.BlockSpec((None, 1, d), lambda i: (row_of_tile(i) * N_MOD + 5, 0, 0)),
                  pl.BlockSpec((1, d), lambda i: (0, 0)),
                  pl.BlockSpec((1, d), lambda i: (0, 0))],
        out_specs=pl.BlockSpec((tm, d), lambda i: (i, 0)),
        compiler_params=_cparams(1),
        name="residual_ln",
    )(x2d, f2d, mods, gain.reshape(1, d), bias.reshape(1, d))


def _rope_tables(n_tokens):
    rows = n_tokens // GRID_W
    row = jnp.broadcast_to(jnp.arange(rows)[:, None], (rows, GRID_W)).reshape(-1).astype(F32)
    col = jnp.broadcast_to(jnp.arange(GRID_W)[None, :], (rows, GRID_W)).reshape(-1).astype(F32)
    half = HEAD_DIM // 2
    inv = ROPE_THETA ** (-jnp.arange(0, half, 2, dtype=F32) / half)
    a_row = row[:, None] * inv
    a_col = col[:, None] * inv
    cos_t = jnp.concatenate([jnp.cos(a_row), jnp.cos(a_row), jnp.cos(a_col), jnp.cos(a_col)], axis=-1)
    sin_t = jnp.concatenate([-jnp.sin(a_row), jnp.sin(a_row), -jnp.sin(a_col), jnp.sin(a_col)], axis=-1)
    return cos_t, sin_t


def _permute_w_in(w):
    main = jnp.concatenate([w[:, _SRC[n][0]:_SRC[n][0] + _SRC[n][1]] for n in _MAIN_ORDER], axis=1)
    small = jnp.concatenate([w[:, _SRC[n][0]:_SRC[n][0] + _SRC[n][1]] for n in _SMALL_ORDER], axis=1)
    small = jnp.pad(small, ((0, 0), (0, SMALL_W - small.shape[1])))
    return main, small


def _scan_tri(tb):
    i = jnp.arange(tb)[:, None]
    j = jnp.arange(tb)[None, :]
    same = (i // CHUNK) == (j // CHUNK)
    return jnp.stack([same & (j <= i), same & (j >= i)]).astype(BF16)


def _gla_params(w_up, b_up):
    hk = GLA_HEADS * GLA_DK
    w = jnp.zeros((2, SMALL_W, hk), F32)
    for z in range(2):
        r0 = _OFF["gla_r"] + z * GLA_RANK
        w = w.at[z, r0:r0 + GLA_RANK].set(w_up[z])
    hi, lo = _split_bf16(w)
    return jnp.stack([hi, lo], axis=1), b_up.reshape(2, 1, hk)


def _gdn_params(a_log, dt_bias):
    sel = jnp.zeros((4, SMALL_W, MIX_HW), F32)
    ealog = jnp.zeros((1, SMALL_W), F32)
    dtb = jnp.zeros((1, SMALL_W), F32)
    for z in range(2):
        for h in range(GDN_HEADS):
            lane_b = _OFF["gdn_b"] + z * GDN_HEADS + h
            lane_a = _OFF["gdn_a"] + z * GDN_HEADS + h
            sel = sel.at[2 * z, lane_b, h * GDN_DK:(h + 1) * GDN_DK].set(1.0)
            sel = sel.at[2 * z + 1, lane_a, h * GDN_DK:(h + 1) * GDN_DK].set(1.0)
            ealog = ealog.at[0, lane_a].set(jnp.exp(a_log[z, h]))
            dtb = dtb.at[0, lane_a].set(dt_bias[z, h])
    return sel.astype(BF16), ealog, dtb


def _moe(x2d_sets, mods, row_fns, r_hi, r_lo, w1, w3, w2, layer, gain, bias, bsz, tms):
    hs, gates, flat_idx, row0s = [], [], [], []
    row0 = 0
    for x2d, row_fn, tm in zip(x2d_sets, row_fns, tms):
        h, lg = _router(x2d, mods, row_fn, r_hi, r_lo, tm)
        t = x2d.shape[0] // bsz
        cap = EC_FACTOR * t // N_EXPERTS
        aff = jax.nn.softmax(lg[:, :N_EXPERTS].reshape(bsz, t, N_EXPERTS), axis=-1).transpose(0, 2, 1)
        gate, idx = lax.top_k(aff, cap)
        idx = idx + (row0 + jnp.arange(bsz) * t)[:, None, None]
        hs.append(h)
        gates.append(gate.transpose(1, 0, 2).reshape(N_EXPERTS, bsz * cap))
        flat_idx.append(idx.transpose(1, 0, 2).reshape(N_EXPERTS, bsz * cap))
        row0s.append(row0)
        row0 += x2d.shape[0]
    h_all = jnp.concatenate(hs, axis=0) if len(hs) > 1 else hs[0]
    fi = jnp.concatenate(flat_idx, axis=1) if len(hs) > 1 else flat_idx[0]
    gs = jnp.concatenate(gates, axis=1) if len(hs) > 1 else gates[0]
    y = _ffn(h_all, fi.astype(jnp.int32), gs[..., None], w1, w3, w2, layer)
    f_all = jnp.zeros((row0, h_all.shape[1]), F32).at[fi.reshape(-1)].add(y.reshape(-1, y.shape[-1]))
    return [_resln(x2d, f_all, r0, mods, row_fn, gain, bias, tm)
            for x2d, row_fn, tm, r0 in zip(x2d_sets, row_fns, tms, row0s)]


def kernel(x, c, ctx, c_ctx, w_ada, b_ada, w_in, w_out, gla_w_up, gla_b_up, gla_norm, gdn_conv, gdn_a_log,
           gdn_dt_bias, gdn_norm, attn_qk_norm, ln_gain, ln_bias, router, w1, w3, w2):
    bsz, n_lat, d = x.shape
    n_ctx = ctx.shape[1]
    depth = w_ada.shape[0]
    cos_t, sin_t = _rope_tables(n_lat)
    tri = _scan_tri(SCAN_TB)

    cond8 = jnp.zeros((8, d), F32).at[:bsz].set(c).at[bsz].set(c_ctx)
    mods_all = _ada(cond8, w_ada, b_ada)

    tm_lat, tm_ctx = 512, n_ctx
    lat_row = lambda i: i // (n_lat // tm_lat)
    ctx_row = lambda i: bsz + 0 * i
    tm_lat_ln = 256
    lat_row_ln = lambda i: i // (n_lat // tm_lat_ln)

    x_lat = x.reshape(bsz * n_lat, d)
    x_ctx = ctx.reshape(bsz * n_ctx, d)
    for l in range(depth):
        last = l == depth - 1
        mods = mods_all[l].reshape(8 * N_MOD, 1, d)
        w_main, w_small = _permute_w_in(w_in[l])
        w_main = w_main.astype(BF16)
        ws_hi, ws_lo = _split_bf16(w_small)

        p_lat, ps_lat = _inproj(x_lat, mods, lat_row, w_main, ws_hi, ws_lo, tm_lat)
        p_ctx, ps_ctx = _inproj(x_ctx, mods, ctx_row, w_main, ws_hi, ws_lo, tm_ctx)

        wup, bup = _gla_params(gla_w_up[l], gla_b_up[l])
        gla_s0 = jnp.zeros((bsz, 2, MIX_HW, GLA_HEADS * GLA_DK), F32)
        gla_cf, gla_cb, gla_s = _gla(p_ctx, ps_ctx, n_ctx, bsz, wup, bup, tri, gla_s0)
        gla_lf, gla_lb, _ = _gla(p_lat, ps_lat, n_lat, bsz, wup, bup, tri, gla_s)

        sel, ealog, dtb = _gdn_params(gdn_a_log[l], gdn_dt_bias[l])
        gdn_s0 = jnp.zeros((bsz, 2, GDN_HEADS, GDN_DK, GDN_DV), F32)
        xc_ctx = _gdn_prep(p_ctx, n_ctx, bsz, gdn_conv[l])
        xc_lat = _gdn_prep(p_lat, n_lat, bsz, gdn_conv[l])
        gdn_cf, gdn_cb, gdn_s = _gdn(xc_ctx, ps_ctx, n_ctx, bsz, sel, ealog, dtb, tri, gdn_s0)
        gdn_lf, gdn_lb, _ = _gdn(xc_lat, ps_lat, n_lat, bsz, sel, ealog, dtb, tri, gdn_s)

        qn_lat, kv_lat = _qkv_prep(p_lat, attn_qk_norm[l], (cos_t, sin_t), n_lat, 512)
        qn_ctx, kv_ctx = _qkv_prep(p_ctx, attn_qk_norm[l], None, n_ctx, n_ctx)
        att_l = _attention(qn_lat, [(kv_ctx, n_ctx), (kv_lat, n_lat)], bsz, 256)
        w_out_b = w_out[l].astype(BF16)
        x_lat = _outproj(gla_lf, gla_lb, gdn_lf, gdn_lb, p_lat, att_l, x_lat, mods, lat_row_ln, w_out_b,
                         gla_norm[l], gdn_norm[l], ln_gain[l, 0], ln_bias[l, 0], tm_lat_ln)
        r_hi, r_lo = _split_bf16(jnp.pad(router[l], ((0, 0), (0, 128 - N_EXPERTS))))
        if not last:
            att_c = _attention(qn_ctx, [(kv_ctx, n_ctx)], bsz, n_ctx)
            x_ctx = _outproj(gla_cf, gla_cb, gdn_cf, gdn_cb, p_ctx, att_c, x_ctx, mods, ctx_row, w_out_b,
                             gla_norm[l], gdn_norm[l], ln_gain[l, 0], ln_bias[l, 0], n_ctx)
            x_lat, x_ctx = _moe([x_lat, x_ctx], mods, [lat_row_ln, ctx_row], r_hi, r_lo, w1, w3, w2, l,
                                ln_gain[l, 1], ln_bias[l, 1], bsz, [tm_lat_ln, n_ctx])
        else:
            (x_lat,) = _moe([x_lat], mods, [lat_row_ln], r_hi, r_lo, w1, w3, w2, l,
                            ln_gain[l, 1], ln_bias[l, 1], bsz, [tm_lat_ln])
    return x_lat.reshape(bsz, n_lat, d)
```

```python
import functools

import jax
import jax.numpy as jnp
from jax import lax
from jax.experimental import pallas as pl
from jax.experimental.pallas import tpu as pltpu

F32 = jnp.float32
BF16 = jnp.bfloat16

D_MODEL = 2048
DEPTH = 2
GRID_W = 64
HEAD_DIM = 128
CHUNK = 64
GLA_HEADS = 4
GLA_DK = 64
GLA_DV = 128
GLA_RANK = 16
GLA_GATE_NORM = 16.0
GDN_HEADS = 4
GDN_DK = 128
GDN_DV = 128
GDN_CONV = 5
ATTN_HEADS = 8
ATTN_KV_HEADS = 2
ATTN_GROUP = ATTN_HEADS // ATTN_KV_HEADS
ROPE_THETA = 10000.0
N_EXPERTS = 16
EC_FACTOR = 2
N_MOD = 6
DEEPNORM_ALPHA = (2 * DEPTH) ** 0.25

_SRC = dict(gla_q=(0, 256), gla_k=(256, 256), gla_v=(512, 512), gla_g=(1024, 512), gla_r=(1536, 32),
            gdn_q=(1568, 512), gdn_k=(2080, 512), gdn_v=(2592, 512), gdn_z=(3104, 512), gdn_b=(3616, 8),
            gdn_a=(3624, 8), att_q=(3632, 1024), att_k=(4656, 256), att_v=(4912, 256))
_MAIN_ORDER = ("att_q", "gla_v", "gdn_q", "gdn_k", "gdn_v", "gla_g", "gdn_z", "gla_q", "gla_k", "att_k", "att_v")
_SMALL_ORDER = ("gla_r", "gdn_b", "gdn_a")
_OFF = {}
_o = 0
for _n in _MAIN_ORDER:
    _OFF[_n] = _o
    _o += _SRC[_n][1]
MAIN_W = _o
_o = 0
for _n in _SMALL_ORDER:
    _OFF[_n] = _o
    _o += _SRC[_n][1]
SMALL_W = 128
MIX_HW = 512
SCAN_TB = 256

VMEM_LIMIT = 56 * 1024 * 1024

_NT = (((1,), (1,)), ((), ()))
_TN = (((0,), (0,)), ((), ()))


def _cparams(n_axes):
    return pltpu.CompilerParams(dimension_semantics=("arbitrary",) * n_axes, vmem_limit_bytes=VMEM_LIMIT)


def _split_bf16(a):
    hi = a.astype(BF16)
    lo = (a - hi.astype(F32)).astype(BF16)
    return hi, lo


def _dot(a, b):
    return jnp.dot(a, b, preferred_element_type=F32)


def _dot2(a, m):
    hi, lo = _split_bf16(a)
    return _dot(hi, m) + _dot(lo, m)


def _dot2_left(m, a):
    hi, lo = _split_bf16(a)
    return _dot(m, hi) + _dot(m, lo)


def _sigmoid(x):
    return 1.0 / (1.0 + jnp.exp(-x))


def _softplus(x):
    return jnp.maximum(x, 0.0) + jnp.log1p(jnp.exp(-jnp.abs(x)))


def _ada_kernel(c_ref, w_ref, b_ref, o_ref):
    c = c_ref[...]
    s = (c * _sigmoid(c)).astype(BF16)
    o_ref[...] = _dot(s, w_ref[...].astype(BF16)) + b_ref[...]


def _ada(cond8, w_ada, b_ada, tn=1024):
    depth, d, n = w_ada.shape
    return pl.pallas_call(
        _ada_kernel,
        out_shape=jax.ShapeDtypeStruct((depth, 8, n), F32),
        grid=(depth, n // tn),
        in_specs=[pl.BlockSpec((8, d), lambda l, j: (0, 0)),
                  pl.BlockSpec((None, d, tn), lambda l, j: (l, 0, j)),
                  pl.BlockSpec((None, 1, tn), lambda l, j: (l, 0, j))],
        out_specs=pl.BlockSpec((None, 8, tn), lambda l, j: (l, 0, j)),
        compiler_params=_cparams(2),
        name="ada",
    )(cond8, w_ada, b_ada.reshape(depth, 1, n))


def _inproj_kernel(x_ref, shift_ref, scale_ref, w_ref, wsh_ref, wsl_ref, o_ref, os_ref, h_ref):
    @pl.when(pl.program_id(1) == 0)
    def _():
        h = x_ref[...] * (1.0 + scale_ref[...]) + shift_ref[...]
        hi, lo = _split_bf16(h)
        h_ref[...] = hi
        wsh = wsh_ref[...]
        os_ref[...] = _dot(hi, wsh) + _dot(lo, wsh) + _dot(hi, wsl_ref[...])

    o_ref[...] = _dot(h_ref[...], w_ref[...])


def _inproj(x2d, mods, row_of_tile, w_main, ws_hi, ws_lo, tm, tn=1024):
    rows, d = x2d.shape
    n = w_main.shape[1]
    return pl.pallas_call(
        _inproj_kernel,
        out_shape=(jax.ShapeDtypeStruct((rows, n), F32), jax.ShapeDtypeStruct((rows, SMALL_W), F32)),
        grid=(rows // tm, n // tn),
        in_specs=[pl.BlockSpec((tm, d), lambda i, j: (i, 0)),
                  pl.BlockSpec((None, 1, d), lambda i, j: (row_of_tile(i) * N_MOD + 0, 0, 0)),
                  pl.BlockSpec((None, 1, d), lambda i, j: (row_of_tile(i) * N_MOD + 1, 0, 0)),
                  pl.BlockSpec((d, tn), lambda i, j: (0, j)),
                  pl.BlockSpec((d, SMALL_W), lambda i, j: (0, 0)),
                  pl.BlockSpec((d, SMALL_W), lambda i, j: (0, 0))],
        out_specs=(pl.BlockSpec((tm, tn), lambda i, j: (i, j)),
                   pl.BlockSpec((tm, SMALL_W), lambda i, j: (i, 0))),
        scratch_shapes=[pltpu.VMEM((tm, d), BF16)],
        compiler_params=_cparams(2),
        name="inproj",
    )(x2d, mods, mods, w_main, ws_hi, ws_lo)


def _rms_rope(x, gain, cos, sin):
    y = x * lax.rsqrt(jnp.mean(x * x, axis=-1, keepdims=True) + 1e-6) * gain
    if cos is not None:
        lane = lax.broadcasted_iota(jnp.int32, y.shape, 1)
        partner = jnp.where((lane % 64) < 32, pltpu.roll(y, 96, 1), pltpu.roll(y, 32, 1))
        y = y * cos + partner * sin
    return y


KV_W = ATTN_KV_HEADS * HEAD_DIM + ATTN_KV_HEADS * 2 * HEAD_DIM
LOG2E = 1.4426950408889634


def _qkv_prep_kernel(*refs, rope):
    if rope:
        q_ref, kv_ref, gain_ref, cos_ref, sin_ref, qo_ref, kvo_ref = refs
        cos, sin = cos_ref[...], sin_ref[...]
    else:
        q_ref, kv_ref, gain_ref, qo_ref, kvo_ref = refs
        cos = sin = None
    q = q_ref[...]
    q_scale = (HEAD_DIM ** -0.5) * LOG2E
    qo_ref[...] = jnp.concatenate(
        [(_rms_rope(q[:, g * HEAD_DIM:(g + 1) * HEAD_DIM], gain_ref[0:1, :], cos, sin) * q_scale).astype(BF16)
         for g in range(ATTN_HEADS)], axis=1)
    kv = kv_ref[...]
    outs = [_rms_rope(kv[:, h * HEAD_DIM:(h + 1) * HEAD_DIM], gain_ref[1:2, :], cos, sin).astype(BF16)
            for h in range(ATTN_KV_HEADS)]
    ones = jnp.ones((kv.shape[0], HEAD_DIM), BF16)
    for h in range(ATTN_KV_HEADS):
        outs += [kv[:, (ATTN_KV_HEADS + h) * HEAD_DIM:(ATTN_KV_HEADS + h + 1) * HEAD_DIM].astype(BF16), ones]
    kvo_ref[...] = jnp.concatenate(outs, axis=1)


def _qkv_prep(p, gain, rope_tabs, n_per_sample, tm):
    rows = p.shape[0]
    qw = ATTN_HEADS * HEAD_DIM
    kvw = 2 * ATTN_KV_HEADS * HEAD_DIM
    per = n_per_sample // tm
    in_specs = [pl.BlockSpec((tm, qw), lambda i: (i, _OFF["att_q"] // qw)),
                pl.BlockSpec((tm, kvw), lambda i: (i, _OFF["att_k"] // kvw)),
                pl.BlockSpec((2, HEAD_DIM), lambda i: (0, 0))]
    args = [p, p, gain]
    if rope_tabs is not None:
        in_specs += [pl.BlockSpec((tm, HEAD_DIM), lambda i: (i % per, 0))] * 2
        args += list(rope_tabs)
    return pl.pallas_call(
        functools.partial(_qkv_prep_kernel, rope=rope_tabs is not None),
        out_shape=(jax.ShapeDtypeStruct((rows, qw), BF16), jax.ShapeDtypeStruct((rows, KV_W), BF16)),
        grid=(rows // tm,),
        in_specs=in_specs,
        out_specs=(pl.BlockSpec((tm, qw), lambda i: (i, 0)), pl.BlockSpec((tm, KV_W), lambda i: (i, 0))),
        compiler_params=_cparams(1),
        name="qkv_prep",
    )(*args)


def _attn_kernel(*refs, n_seg):
    q_ref = refs[0]
    k_refs = refs[1:1 + 2 * n_seg:2]
    v_refs = refs[2:2 + 2 * n_seg:2]
    o_ref = refs[1 + 2 * n_seg]
    outs = []
    for g in range(ATTN_GROUP):
        qg = q_ref[:, g * HEAD_DIM:(g + 1) * HEAD_DIM]
        s = jnp.concatenate([lax.dot_general(qg, k_ref[...], _NT, preferred_element_type=F32) for k_ref in k_refs],
                            axis=1)
        p = jnp.exp2(s - jnp.max(s, axis=-1, keepdims=True)).astype(BF16)
        acc = None
        off = 0
        for v_ref in v_refs:
            n = v_ref.shape[0]
            part = _dot(p[:, off:off + n], v_ref[...])
            acc = part if acc is None else acc + part
            off += n
        outs.append((acc[:, :HEAD_DIM] / acc[:, HEAD_DIM:]).astype(o_ref.dtype))
    o_ref[...] = jnp.concatenate(outs, axis=1)


def _attention(qn, kv_segs, bsz, tq):
    rows = qn.shape[0]
    nq = rows // bsz // tq
    gw = ATTN_GROUP * HEAD_DIM
    in_specs = [pl.BlockSpec((tq, gw), lambda b, h, i: (b * nq + i, h))]
    args = [qn]
    for arr, n in kv_segs:
        in_specs.append(pl.BlockSpec((n, HEAD_DIM), lambda b, h, i: (b, h)))
        in_specs.append(pl.BlockSpec((n, 2 * HEAD_DIM), lambda b, h, i: (b, 1 + h)))
        args += [arr, arr]
    return pl.pallas_call(
        functools.partial(_attn_kernel, n_seg=len(kv_segs)),
        out_shape=jax.ShapeDtypeStruct((rows, ATTN_HEADS * HEAD_DIM), BF16),
        grid=(bsz, ATTN_KV_HEADS, nq),
        in_specs=in_specs,
        out_specs=pl.BlockSpec((tq, gw), lambda b, h, i: (b * nq + i, h)),
        compiler_params=_cparams(3),
        name="attention",
    )(*args)


def _chunk_masks(tb):
    i = lax.broadcasted_iota(jnp.int32, (tb, tb), 0)
    j = lax.broadcasted_iota(jnp.int32, (tb, tb), 1)
    same = (i // CHUNK) == (j // CHUNK)
    return i, j, same


def _gla_kernel(qk_f, v_f, s_f, qk_b, v_b, s_b, wup_ref, bup_ref, tri_ref, st0_ref,
                of_ref, ob_ref, stout_ref, st_scr):
    step = pl.program_id(1)

    @pl.when(step == 0)
    def _():
        st_scr[...] = st0_ref[...]

    tb = qk_f.shape[0]
    nch = tb // CHUNK
    hk = GLA_HEADS * GLA_DK
    lane = lax.broadcasted_iota(jnp.int32, (CHUNK, hk), 1)
    head_of_lane = lane // GLA_DK
    r4 = lax.broadcasted_iota(jnp.int32, (GLA_HEADS * CHUNK, CHUNK), 0) % CHUNK
    c4 = lax.broadcasted_iota(jnp.int32, (GLA_HEADS * CHUNK, CHUNK), 1)
    bd_r = lax.broadcasted_iota(jnp.int32, (MIX_HW, hk), 0) // GLA_DV
    bd_c = lax.broadcasted_iota(jnp.int32, (MIX_HW, hk), 1) // GLA_DK
    bd_mask = bd_r == bd_c

    dirs = []
    for z, (qk_ref, v_ref, s_ref) in enumerate(((qk_f, v_f, s_f), (qk_b, v_b, s_b))):
        qk = qk_ref[...]
        hi, lo = _split_bf16(s_ref[...])
        wh = wup_ref[z, 0]
        logit = _dot(hi, wh) + _dot(lo, wh) + _dot(hi, wup_ref[z, 1]) + bup_ref[z]
        log_a = -_softplus(-logit) / GLA_GATE_NORM
        dirs.append(dict(
            z=z, q=qk[:, :hk] * (GLA_DK ** -0.5), k=qk[:, hk:], v=v_ref[...].astype(BF16),
            cum=_dot2_left(tri_ref[z], log_a),
            mask=(c4 <= r4) if z == 0 else (c4 >= r4), st=st_scr[z], outs=[None] * nch))

    for c_step in range(nch):
        for dr in dirs:
            z = dr["z"]
            c = c_step if z == 0 else nch - 1 - c_step
            last = CHUNK - 1 if z == 0 else 0
            r0 = c * CHUNK
            cum_c = dr["cum"][r0:r0 + CHUNK]
            tot = cum_c[last:last + 1, :]
            q_dec = dr["q"][r0:r0 + CHUNK] * jnp.exp(cum_c)
            k_c = dr["k"][r0:r0 + CHUNK]
            k_inv = (k_c * jnp.exp(-cum_c)).astype(BF16)
            k_dec = (k_c * jnp.exp(tot - cum_c)).astype(BF16)
            q4 = jnp.concatenate([jnp.where(head_of_lane == h, q_dec, 0.0) for h in range(GLA_HEADS)],
                                 axis=0).astype(BF16)
            a4 = lax.dot_general(q4, k_inv, _NT, preferred_element_type=F32)
            a4 = jnp.where(dr["mask"], a4, 0.0).astype(BF16)
            v_c = dr["v"][r0:r0 + CHUNK]
            o4 = _dot(a4, v_c)
            o_intra = jnp.concatenate(
                [o4[h * CHUNK:(h + 1) * CHUNK, h * GLA_DV:(h + 1) * GLA_DV] for h in range(GLA_HEADS)], axis=1)
            o_inter = lax.dot_general(q_dec.astype(BF16), dr["st"].astype(BF16), _NT, preferred_element_type=F32)
            dr["outs"][c] = o_intra + o_inter
            kv_t = lax.dot_general(v_c, k_dec, _TN, preferred_element_type=F32)
            dr["st"] = dr["st"] * jnp.exp(tot) + jnp.where(bd_mask, kv_t, 0.0)

    for dr, o_ref in zip(dirs, (of_ref, ob_ref)):
        o_ref[...] = jnp.concatenate(dr["outs"], axis=0)
        st_scr[dr["z"]] = dr["st"]

    @pl.when(step == pl.num_programs(1) - 1)
    def _():
        stout_ref[...] = st_scr[...]


def _gla(p, ps, t, bsz, wup, bup, tri, st0):
    tb = SCAN_TB
    nblk = t // tb
    qk_col = _OFF["gla_q"] // (2 * GLA_HEADS * GLA_DK)
    v_col = _OFF["gla_v"] // MIX_HW
    fwd = lambda b, i: b * nblk + i
    bwd = lambda b, i: b * nblk + (nblk - 1 - i)
    hk = GLA_HEADS * GLA_DK
    return pl.pallas_call(
        _gla_kernel,
        out_shape=(jax.ShapeDtypeStruct((bsz * t, MIX_HW), F32), jax.ShapeDtypeStruct((bsz * t, MIX_HW), F32),
                   jax.ShapeDtypeStruct((bsz, 2, MIX_HW, hk), F32)),
        grid=(bsz, nblk),
        in_specs=[pl.BlockSpec((tb, 2 * hk), lambda b, i: (fwd(b, i), qk_col)),
                  pl.BlockSpec((tb, MIX_HW), lambda b, i: (fwd(b, i), v_col)),
                  pl.BlockSpec((tb, SMALL_W), lambda b, i: (fwd(b, i), 0)),
                  pl.BlockSpec((tb, 2 * hk), lambda b, i: (bwd(b, i), qk_col)),
                  pl.BlockSpec((tb, MIX_HW), lambda b, i: (bwd(b, i), v_col)),
                  pl.BlockSpec((tb, SMALL_W), lambda b, i: (bwd(b, i), 0)),
                  pl.BlockSpec((2, 2, SMALL_W, hk), lambda b, i: (0, 0, 0, 0)),
                  pl.BlockSpec((2, 1, hk), lambda b, i: (0, 0, 0)),
                  pl.BlockSpec((2, tb, tb), lambda b, i: (0, 0, 0)),
                  pl.BlockSpec((None, 2, MIX_HW, hk), lambda b, i: (b, 0, 0, 0))],
        out_specs=(pl.BlockSpec((tb, MIX_HW), lambda b, i: (fwd(b, i), 0)),
                   pl.BlockSpec((tb, MIX_HW), lambda b, i: (bwd(b, i), 0)),
                   pl.BlockSpec((None, 2, MIX_HW, hk), lambda b, i: (b, 0, 0, 0))),
        scratch_shapes=[pltpu.VMEM((2, MIX_HW, hk), F32)],
        compiler_params=_cparams(2),
        name="gla_scan",
    )(p, p, ps, p, p, ps, wup, bup, tri, st0)


def _gdn_prep_kernel(x_ref, prev_ref, next_ref, w_ref, o_ref):
    i = pl.program_id(1)
    tb = x_ref.shape[0]
    halo = prev_ref.shape[0]
    prev = jnp.where(i > 0, prev_ref[...], 0.0)
    nxt = jnp.where(i < pl.num_programs(1) - 1, next_ref[...], 0.0)
    ext = jnp.concatenate([prev, x_ref[...], nxt], axis=0)
    w = w_ref[...]
    acc = None
    for j in range(GDN_CONV):
        shift = GDN_CONV // 2 - j
        rolled = ext if shift == 0 else pltpu.roll(ext, shift % (tb + 2 * halo), 0)
        term = rolled[halo:halo + tb] * w[j:j + 1, :]
        acc = term if acc is None else acc + term
    y = acc * _sigmoid(acc)
    qk_w = 2 * GDN_HEADS * GDN_DK
    outs = []
    for h in range(2 * GDN_HEADS):
        yh = y[:, h * GDN_DK:(h + 1) * GDN_DK]
        yh = yh * lax.rsqrt(jnp.sum(yh * yh, axis=-1, keepdims=True) + 1e-6)
        if h < GDN_HEADS:
            yh = yh * (GDN_DK ** -0.5)
        outs.append(yh)
    outs.append(y[:, qk_w:])
    o_ref[...] = jnp.concatenate(outs, axis=1)


def _gdn_prep(p, t, bsz, conv_w):
    tb = SCAN_TB
    halo = 8
    nblk = t // tb
    width = conv_w.shape[1]
    col = _OFF["gdn_q"] // width
    n_halo_blocks = bsz * t // halo
    per = tb // halo
    return pl.pallas_call(
        _gdn_prep_kernel,
        out_shape=jax.ShapeDtypeStruct((bsz * t, width), F32),
        grid=(bsz, nblk),
        in_specs=[pl.BlockSpec((tb, width), lambda b, i: (b * nblk + i, col)),
                  pl.BlockSpec((halo, width), lambda b, i: (jnp.maximum((b * nblk + i) * per - 1, 0), col)),
                  pl.BlockSpec((halo, width),
                               lambda b, i: (jnp.minimum((b * nblk + i + 1) * per, n_halo_blocks - 1), col)),
                  pl.BlockSpec((GDN_CONV, width), lambda b, i: (0, 0))],
        out_specs=pl.BlockSpec((tb, width), lambda b, i: (b * nblk + i, 0)),
        compiler_params=_cparams(2),
        name="gdn_conv",
    )(p, p, p, conv_w)


def _gdn_kernel(x_f, s_f, x_b, s_b, sel_ref, ealog_ref, dtb_ref, tri_ref, s0_ref,
                of_ref, ob_ref, sout_ref, s_scr):
    step = pl.program_id(1)

    @pl.when(step == 0)
    def _():
        s_scr[...] = s0_ref[...]

    tb = x_f.shape[0]
    nch = tb // CHUNK
    i_idx, j_idx, same = _chunk_masks(tb)
    xor = i_idx ^ j_idx
    eye = jnp.where(i_idx == j_idx, 1.0, 0.0)
    hw = GDN_HEADS * GDN_DK

    chains = []
    for z, (x_ref, s_ref) in enumerate(((x_f, s_f), (x_b, s_b))):
        before = (j_idx <= i_idx) if z == 0 else (j_idx >= i_idx)
        incl = same & before
        strict = incl & (i_idx != j_idx)
        sm = s_ref[...]
        beta_all = _sigmoid(sm)
        g_all = -ealog_ref[...] * _softplus(sm + dtb_ref[...])
        b_sel = _dot2(beta_all, sel_ref[2 * z])
        g_sel = _dot2(g_all, sel_ref[2 * z + 1])
        c_col = _dot2_left(tri_ref[z], g_sel)
        x = x_ref[...]
        for h in range(GDN_HEADS):
            ch = dict(z=z, h=h, incl=incl)
            ch["q"] = x[:, h * GDN_DK:(h + 1) * GDN_DK]
            ch["k"] = x[:, hw + h * GDN_DK:hw + (h + 1) * GDN_DK]
            bc = b_sel[:, h * GDN_DK:(h + 1) * GDN_DK]
            cc = c_col[:, h * GDN_DK:(h + 1) * GDN_DK]
            ch["cc"] = cc
            kb = ch["k"] * bc
            ch["vb"] = (x[:, 2 * hw + h * GDN_DV:2 * hw + (h + 1) * GDN_DV] * bc).astype(BF16)
            ch["kbg"] = (kb * jnp.exp(cc)).astype(BF16)
            ch["k_bf"] = ch["k"].astype(BF16)
            c_row = cc.T[0:1, :]
            dm = jnp.concatenate([cc] * (tb // GDN_DK), axis=1) - c_row
            ch["gamma"] = jnp.where(incl, jnp.exp(jnp.where(incl, dm, 0.0)), 0.0)
            kk = lax.dot_general(kb.astype(BF16), ch["k_bf"], _NT, preferred_element_type=F32)
            ch["l_mat"] = jnp.where(strict, kk * ch["gamma"], 0.0)
            ch["t_inv"] = eye - jnp.where(xor == 1, ch["l_mat"], 0.0)
            chains.append(ch)

    s = 2
    while s < CHUNK:
        lvl = (xor >= s) & (xor < 2 * s)
        for ch in chains:
            ch["t_bf"] = ch["t_inv"].astype(BF16)
            ch["m1"] = _dot(ch["t_bf"], jnp.where(lvl, ch["l_mat"], 0.0).astype(BF16)).astype(BF16)
        for ch in chains:
            ch["t_inv"] = ch["t_inv"] - _dot(ch["m1"], ch["t_bf"])
        s *= 2

    for ch in chains:
        t_bf = ch["t_inv"].astype(BF16)
        ch["u"] = _dot(t_bf, ch["vb"])
        ch["w"] = _dot(t_bf, ch["kbg"]).astype(BF16)
        qk = lax.dot_general(ch["q"].astype(BF16), ch["k_bf"], _NT, preferred_element_type=F32)
        ch["a_int"] = jnp.where(ch["incl"], qk * ch["gamma"], 0.0).astype(BF16)
        ch["state"] = s_scr[ch["z"], ch["h"]]
        ch["outs"] = [None] * nch

    zeros = jnp.zeros((CHUNK, GDN_DV), BF16)
    for c_step in range(nch):
        for ch in chains:
            c = c_step if ch["z"] == 0 else nch - 1 - c_step
            last = CHUNK - 1 if ch["z"] == 0 else 0
            r0 = c * CHUNK
            cc_c = ch["cc"][r0:r0 + CHUNK]
            g_last = cc_c[last:last + 1, :]
            k_dec = (ch["k"][r0:r0 + CHUNK] * jnp.exp(g_last - cc_c)).astype(BF16)
            q_dec = (ch["q"][r0:r0 + CHUNK] * jnp.exp(cc_c)).astype(BF16)
            s_bf = ch["state"].astype(BF16)
            v_new = ch["u"][r0:r0 + CHUNK] - _dot(ch["w"][r0:r0 + CHUNK], s_bf)
            v_new_bf = v_new.astype(BF16)
            v_pad = jnp.concatenate([v_new_bf if cc_i == c else zeros for cc_i in range(nch)], axis=0)
            ch["outs"][c] = _dot(q_dec, s_bf) + _dot(ch["a_int"][r0:r0 + CHUNK], v_pad)
            ch["state"] = (ch["state"] * jnp.exp(g_last)
                           + lax.dot_general(k_dec, v_new_bf, _TN, preferred_element_type=F32))

    for z, o_ref in enumerate((of_ref, ob_ref)):
        mine = [ch for ch in chains if ch["z"] == z]
        o_ref[...] = jnp.concatenate([jnp.concatenate(ch["outs"], axis=0) for ch in mine], axis=1)
        for ch in mine:
            s_scr[z, ch["h"]] = ch["state"]

    @pl.when(step == pl.num_programs(1) - 1)
    def _():
        sout_ref[...] = s_scr[...]


def _gdn(xc, ps, t, bsz, sel, ealog, dtb, tri, s0):
    tb = SCAN_TB
    nblk = t // tb
    width = xc.shape[1]
    fwd = lambda b, i: b * nblk + i
    bwd = lambda b, i: b * nblk + (nblk - 1 - i)
    return pl.pallas_call(
        _gdn_kernel,
        out_shape=(jax.ShapeDtypeStruct((bsz * t, MIX_HW), F32), jax.ShapeDtypeStruct((bsz * t, MIX_HW), F32),
                   jax.ShapeDtypeStruct((bsz, 2, GDN_HEADS, GDN_DK, GDN_DV), F32)),
        grid=(bsz, nblk),
        in_specs=[pl.BlockSpec((tb, width), lambda b, i: (fwd(b, i), 0)),
                  pl.BlockSpec((tb, SMALL_W), lambda b, i: (fwd(b, i), 0)),
                  pl.BlockSpec((tb, width), lambda b, i: (bwd(b, i), 0)),
                  pl.BlockSpec((tb, SMALL_W), lambda b, i: (bwd(b, i), 0)),
                  pl.BlockSpec((4, SMALL_W, MIX_HW), lambda b, i: (0, 0, 0)),
                  pl.BlockSpec((1, SMALL_W), lambda b, i: (0, 0)),
                  pl.BlockSpec((1, SMALL_W), lambda b, i: (0, 0)),
                  pl.BlockSpec((2, tb, tb), lambda b, i: (0, 0, 0)),
                  pl.BlockSpec((None, 2, GDN_HEADS, GDN_DK, GDN_DV), lambda b, i: (b, 0, 0, 0, 0))],
        out_specs=(pl.BlockSpec((tb, MIX_HW), lambda b, i: (fwd(b, i), 0)),
                   pl.BlockSpec((tb, MIX_HW), lambda b, i: (bwd(b, i), 0)),
                   pl.BlockSpec((None, 2, GDN_HEADS, GDN_DK, GDN_DV), lambda b, i: (b, 0, 0, 0, 0))),
        scratch_shapes=[pltpu.VMEM((2, GDN_HEADS, GDN_DK, GDN_DV), F32)],
        compiler_params=_cparams(2),
        name="gdn_scan",
    )(xc, ps, xc, ps, sel, ealog, dtb, tri, s0)


def _layer_norm_rows(z, gain, bias):
    mu = jnp.mean(z, axis=-1, keepdims=True)
    zc = z - mu
    var = jnp.mean(zc * zc, axis=-1, keepdims=True)
    return zc * lax.rsqrt(var + 1e-5) * gain + bias


def _mixer_finish(o, gate, gain, n_heads, dv):
    outs = []
    for h in range(n_heads):
        oh = o[:, h * dv:(h + 1) * dv]
        oh = oh * lax.rsqrt(jnp.mean(oh * oh, axis=-1, keepdims=True) + 1e-6) * gain
        gh = gate[:, h * dv:(h + 1) * dv]
        outs.append((oh * (gh * _sigmoid(gh))).astype(BF16))
    return outs


def _outproj_kernel(glaf_ref, glab_ref, gdnf_ref, gdnb_ref, g_ref, z_ref, att_ref, x_ref, gate_ref, w_ref,
                    ngla_ref, ngdn_ref, gain_ref, bias_ref, o_ref):
    parts = _mixer_finish(glaf_ref[...] + glab_ref[...], g_ref[...], ngla_ref[...], GLA_HEADS, GLA_DV)
    parts += _mixer_finish(gdnf_ref[...] + gdnb_ref[...], z_ref[...], ngdn_ref[...], GDN_HEADS, GDN_DV)
    parts.append(att_ref[...])
    y = _dot(jnp.concatenate(parts, axis=1), w_ref[...])
    z = DEEPNORM_ALPHA * x_ref[...] + gate_ref[...] * y
    o_ref[...] = _layer_norm_rows(z, gain_ref[...], bias_ref[...])


def _outproj(gla_f, gla_b, gdn_f, gdn_b, p, att, x2d, mods, row_of_tile, w_out, n_gla, n_gdn, gain, bias, tm):
    rows, d = x2d.shape
    g_col = _OFF["gla_g"] // MIX_HW
    z_col = _OFF["gdn_z"] // MIX_HW
    mix = lambda: pl.BlockSpec((tm, MIX_HW), lambda i: (i, 0))
    vec = lambda n: pl.BlockSpec((1, n), lambda i: (0, 0))
    return pl.pallas_call(
        _outproj_kernel,
        out_shape=jax.ShapeDtypeStruct((rows, d), F32),
        grid=(rows // tm,),
        in_specs=[mix(), mix(), mix(), mix(),
                  pl.BlockSpec((tm, MIX_HW), lambda i: (i, g_col)),
                  pl.BlockSpec((tm, MIX_HW), lambda i: (i, z_col)),
                  pl.BlockSpec((tm, att.shape[1]), lambda i: (i, 0)),
                  pl.BlockSpec((tm, d), lambda i: (i, 0)),
                  pl.BlockSpec((None, 1, d), lambda i: (row_of_tile(i) * N_MOD + 2, 0, 0)),
                  pl.BlockSpec((w_out.shape[0], d), lambda i: (0, 0)),
                  vec(GLA_DV), vec(GDN_DV), vec(d), vec(d)],
        out_specs=pl.BlockSpec((tm, d), lambda i: (i, 0)),
        compiler_params=_cparams(1),
        name="outproj_ln",
    )(gla_f, gla_b, gdn_f, gdn_b, p, p, att, x2d, mods, w_out, n_gla.reshape(1, -1), n_gdn.reshape(1, -1),
      gain.reshape(1, d), bias.reshape(1, d))


def _router_kernel(x_ref, shift_ref, scale_ref, rh_ref, rl_ref, h_ref, lg_ref):
    h = x_ref[...] * (1.0 + scale_ref[...]) + shift_ref[...]
    hi, lo = _split_bf16(h)
    h_ref[...] = h
    rh = rh_ref[...]
    lg_ref[...] = _dot(hi, rh) + _dot(lo, rh) + _dot(hi, rl_ref[...])


def _router(x2d, mods, row_of_tile, r_hi, r_lo, tm):
    rows, d = x2d.shape
    return pl.pallas_call(
        _router_kernel,
        out_shape=(jax.ShapeDtypeStruct((rows, d), F32), jax.ShapeDtypeStruct((rows, 128), F32)),
        grid=(rows // tm,),
        in_specs=[pl.BlockSpec((tm, d), lambda i: (i, 0)),
                  pl.BlockSpec((None, 1, d), lambda i: (row_of_tile(i) * N_MOD + 3, 0, 0)),
                  pl.BlockSpec((None, 1, d), lambda i: (row_of_tile(i) * N_MOD + 4, 0, 0)),
                  pl.BlockSpec((d, 128), lambda i: (0, 0)),
                  pl.BlockSpec((d, 128), lambda i: (0, 0))],
        out_specs=(pl.BlockSpec((tm, d), lambda i: (i, 0)), pl.BlockSpec((tm, 128), lambda i: (i, 0))),
        compiler_params=_cparams(1),
        name="router",
    )(x2d, mods, mods, r_hi, r_lo)


def _row_copy(h_hbm, land, sem, src_row, dst_row):
    return pltpu.make_async_copy(h_hbm.at[pl.ds(src_row, 1), :], land.at[pl.ds(dst_row, 1), :], sem)


def _ffn_kernel(idx_ref, h_hbm, g_ref, w1_ref, w3_ref, w2_ref, o_ref, land, x_scr, h_scr, sem, *, nf):
    e = pl.program_id(0)
    s = pl.program_id(1)
    n_e = pl.num_programs(0)
    tf = w1_ref.shape[1]
    n_rows = land.shape[0]
    per = n_rows // nf

    def wait_all_rows():
        pltpu.make_async_copy(h_hbm.at[pl.ds(0, n_rows), :], land, sem).wait()

    @pl.when(s == 0)
    def _():
        @pl.when(e == 0)
        def _():
            def issue(r, carry):
                _row_copy(h_hbm, land, sem, idx_ref[0, r], r).start()
                return carry
            lax.fori_loop(0, n_rows, issue, 0)

        wait_all_rows()
        x_scr[...] = land[...].astype(BF16)

    @pl.when(s < nf)
    def _():
        nxt = lax.rem(e + 1, n_e)
        base = s * per
        for i in range(per):
            _row_copy(h_hbm, land, sem, idx_ref[nxt, base + i], base + i).start()
        x = x_scr[...]
        a = _dot(x, w1_ref[...].astype(BF16))
        u = _dot(x, w3_ref[...].astype(BF16))
        hmid = (a * _sigmoid(a) * u).astype(BF16)
        for f in range(nf):
            @pl.when(s == f)
            def _():
                h_scr[:, f * tf:(f + 1) * tf] = hmid

    @pl.when(s >= nf)
    def _():
        o_ref[...] = _dot(h_scr[...], w2_ref[...].astype(BF16)) * g_ref[...]

    @pl.when((e == n_e - 1) & (s == pl.num_programs(1) - 1))
    def _():
        wait_all_rows()


def _ffn(h_all, rows_idx, gates, w1, w3, w2, layer, tf=512, td=256):
    n_e, r = rows_idx.shape
    d = h_all.shape[1]
    ff = w1.shape[3]
    nf = ff // tf
    assert r % nf == 0
    up = lambda e, s, idx: (layer, e, 0, jnp.minimum(s, nf - 1))
    down = lambda e, s, idx: (layer, e, 0, jnp.maximum(s - nf, 0))
    grid_spec = pltpu.PrefetchScalarGridSpec(
        num_scalar_prefetch=1,
        grid=(n_e, nf + d // td),
        in_specs=[pl.BlockSpec(memory_space=pl.ANY),
                  pl.BlockSpec((None, r, 1), lambda e, s, idx: (e, 0, 0)),
                  pl.BlockSpec((None, None, d, tf), up),
                  pl.BlockSpec((None, None, d, tf), up),
                  pl.BlockSpec((None, None, ff, td), down)],
        out_specs=pl.BlockSpec((None, r, td), lambda e, s, idx: (e, 0, jnp.maximum(s - nf, 0))),
        scratch_shapes=[pltpu.VMEM((r, d), F32), pltpu.VMEM((r, d), BF16), pltpu.VMEM((r, ff), BF16),
                        pltpu.SemaphoreType.DMA(())],
    )
    return pl.pallas_call(
        functools.partial(_ffn_kernel, nf=nf),
        out_shape=jax.ShapeDtypeStruct((n_e, r, d), F32),
        grid_spec=grid_spec,
        compiler_params=_cparams(2),
        name="expert_ffn",
    )(rows_idx, h_all, gates, w1, w3, w2)


def _resln_kernel(x_ref, f_ref, gate_ref, gain_ref, bias_ref, o_ref):
    z = DEEPNORM_ALPHA * x_ref[...] + gate_ref[...] * f_ref[...]
    o_ref[...] = _layer_norm_rows(z, gain_ref[...], bias_ref[...])


def _resln(x2d, f2d, f_row0, mods, row_of_tile, gain, bias, tm):
    rows, d = x2d.shape
    f_blk0 = f_row0 // tm
    return pl.pallas_call(
        _resln_kernel,
        out_shape=jax.ShapeDtypeStruct((rows, d), F32),
        grid=(rows // tm,),
        in_specs=[pl.BlockSpec((tm, d), lambda i: (i, 0)),
                  pl.BlockSpec((tm, d), lambda i: (i + f_blk0, 0)),
                  pl.BlockSpec((None, 1, d), lambda i: (row_of_tile(i) * N_MOD + 5, 0, 0)),
                  pl.BlockSpec((1, d), lambda i: (0, 0)),
                  pl.BlockSpec((1, d), lambda i: (0, 0))],
        out_specs=pl.BlockSpec((tm, d), lambda i: (i, 0)),
        compiler_params=_cparams(1),
        name="residual_ln",
    )(x2d, f2d, mods, gain.reshape(1, d), bias.reshape(1, d))


def _rope_tables(n_tokens):
    rows = n_tokens // GRID_W
    row = jnp.broadcast_to(jnp.arange(rows)[:, None], (rows, GRID_W)).reshape(-1).astype(F32)
    col = jnp.broadcast_to(jnp.arange(GRID_W)[None, :], (rows, GRID_W)).reshape(-1).astype(F32)
    half = HEAD_DIM // 2
    inv = ROPE_THETA ** (-jnp.arange(0, half, 2, dtype=F32) / half)
    a_row = row[:, None] * inv
    a_col = col[:, None] * inv
    cos_t = jnp.concatenate([jnp.cos(a_row), jnp.cos(a_row), jnp.cos(a_col), jnp.cos(a_col)], axis=-1)
    sin_t = jnp.concatenate([-jnp.sin(a_row), jnp.sin(a_row), -jnp.sin(a_col), jnp.sin(a_col)], axis=-1)
    return cos_t, sin_t


def _permute_w_in(w):
    main = jnp.concatenate([w[:, _SRC[n][0]:_SRC[n][0] + _SRC[n][1]] for n in _MAIN_ORDER], axis=1)
    small = jnp.concatenate([w[:, _SRC[n][0]:_SRC[n][0] + _SRC[n][1]] for n in _SMALL_ORDER], axis=1)
    small = jnp.pad(small, ((0, 0), (0, SMALL_W - small.shape[1])))
    return main, small


def _scan_tri(tb):
    i = jnp.arange(tb)[:, None]
    j = jnp.arange(tb)[None, :]
    same = (i // CHUNK) == (j // CHUNK)
    return jnp.stack([same & (j <= i), same & (j >= i)]).astype(BF16)


def _gla_params(w_up, b_up):
    hk = GLA_HEADS * GLA_DK
    w = jnp.zeros((2, SMALL_W, hk), F32)
    for z in range(2):
        r0 = _OFF["gla_r"] + z * GLA_RANK
        w = w.at[z, r0:r0 + GLA_RANK].set(w_up[z])
    hi, lo = _split_bf16(w)
    return jnp.stack([hi, lo], axis=1), b_up.reshape(2, 1, hk)


def _gdn_params(a_log, dt_bias):
    sel = jnp.zeros((4, SMALL_W, MIX_HW), F32)
    ealog = jnp.zeros((1, SMALL_W), F32)
    dtb = jnp.zeros((1, SMALL_W), F32)
    for z in range(2):
        for h in range(GDN_HEADS):
            lane_b = _OFF["gdn_b"] + z * GDN_HEADS + h
            lane_a = _OFF["gdn_a"] + z * GDN_HEADS + h
            sel = sel.at[2 * z, lane_b, h * GDN_DK:(h + 1) * GDN_DK].set(1.0)
            sel = sel.at[2 * z + 1, lane_a, h * GDN_DK:(h + 1) * GDN_DK].set(1.0)
            ealog = ealog.at[0, lane_a].set(jnp.exp(a_log[z, h]))
            dtb = dtb.at[0, lane_a].set(dt_bias[z, h])
    return sel.astype(BF16), ealog, dtb


def _moe(x2d_sets, mods, row_fns, r_hi, r_lo, w1, w3, w2, layer, gain, bias, bsz, tms):
    hs, gates, flat_idx, row0s = [], [], [], []
    row0 = 0
    for x2d, row_fn, tm in zip(x2d_sets, row_fns, tms):
        h, lg = _router(x2d, mods, row_fn, r_hi, r_lo, tm)
        t = x2d.shape[0] // bsz
        cap = EC_FACTOR * t // N_EXPERTS
        aff = jax.nn.softmax(lg[:, :N_EXPERTS].reshape(bsz, t, N_EXPERTS), axis=-1).transpose(0, 2, 1)
        gate, idx = lax.top_k(aff, cap)
        idx = idx + (row0 + jnp.arange(bsz) * t)[:, None, None]
        hs.append(h)
        gates.append(gate.transpose(1, 0, 2).reshape(N_EXPERTS, bsz * cap))
        flat_idx.append(idx.transpose(1, 0, 2).reshape(N_EXPERTS, bsz * cap))
        row0s.append(row0)
        row0 += x2d.shape[0]
    h_all = jnp.concatenate(hs, axis=0) if len(hs) > 1 else hs[0]
    fi = jnp.concatenate(flat_idx, axis=1) if len(hs) > 1 else flat_idx[0]
    gs = jnp.concatenate(gates, axis=1) if len(hs) > 1 else gates[0]
    y = _ffn(h_all, fi.astype(jnp.int32), gs[..., None], w1, w3, w2, layer)
    f_all = jnp.zeros((row0, h_all.shape[1]), F32).at[fi.reshape(-1)].add(y.reshape(-1, y.shape[-1]))
    return [_resln(x2d, f_all, r0, mods, row_fn, gain, bias, tm)
            for x2d, row_fn, tm, r0 in zip(x2d_sets, row_fns, tms, row0s)]


def kernel(x, c, ctx, c_ctx, w_ada, b_ada, w_in, w_out, gla_w_up, gla_b_up, gla_norm, gdn_conv, gdn_a_log,
           gdn_dt_bias, gdn_norm, attn_qk_norm, ln_gain, ln_bias, router, w1, w3, w2):
    bsz, n_lat, d = x.shape
    n_ctx = ctx.shape[1]
    depth = w_ada.shape[0]
    cos_t, sin_t = _rope_tables(n_lat)
    tri = _scan_tri(SCAN_TB)

    cond8 = jnp.zeros((8, d), F32).at[:bsz].set(c).at[bsz].set(c_ctx)
    mods_all = _ada(cond8, w_ada, b_ada)

    tm_lat, tm_ctx = 1024, n_ctx
    lat_row = lambda i: i // (n_lat // tm_lat)
    ctx_row = lambda i: bsz + 0 * i
    tm_lat_ln = 256
    lat_row_ln = lambda i: i // (n_lat // tm_lat_ln)

    x_lat = x.reshape(bsz * n_lat, d)
    x_ctx = ctx.reshape(bsz * n_ctx, d)
    for l in range(depth):
        last = l == depth - 1
        mods = mods_all[l].reshape(8 * N_MOD, 1, d)
        w_main, w_small = _permute_w_in(w_in[l])
        w_main = w_main.astype(BF16)
        ws_hi, ws_lo = _split_bf16(w_small)

        p_lat, ps_lat = _inproj(x_lat, mods, lat_row, w_main, ws_hi, ws_lo, tm_lat, tn=512)
        p_ctx, ps_ctx = _inproj(x_ctx, mods, ctx_row, w_main, ws_hi, ws_lo, tm_ctx)

        wup, bup = _gla_params(gla_w_up[l], gla_b_up[l])
        gla_s0 = jnp.zeros((bsz, 2, MIX_HW, GLA_HEADS * GLA_DK), F32)
        gla_cf, gla_cb, gla_s = _gla(p_ctx, ps_ctx, n_ctx, bsz, wup, bup, tri, gla_s0)
        gla_lf, gla_lb, _ = _gla(p_lat, ps_lat, n_lat, bsz, wup, bup, tri, gla_s)

        sel, ealog, dtb = _gdn_params(gdn_a_log[l], gdn_dt_bias[l])
        gdn_s0 = jnp.zeros((bsz, 2, GDN_HEADS, GDN_DK, GDN_DV), F32)
        xc_ctx = _gdn_prep(p_ctx, n_ctx, bsz, gdn_conv[l])
        xc_lat = _gdn_prep(p_lat, n_lat, bsz, gdn_conv[l])
        gdn_cf, gdn_cb, gdn_s = _gdn(xc_ctx, ps_ctx, n_ctx, bsz, sel, ealog, dtb, tri, gdn_s0)
        gdn_lf, gdn_lb, _ = _gdn(xc_lat, ps_lat, n_lat, bsz, sel, ealog, dtb, tri, gdn_s)

        qn_lat, kv_lat = _qkv_prep(p_lat, attn_qk_norm[l], (cos_t, sin_t), n_lat, 512)
        qn_ctx, kv_ctx = _qkv_prep(p_ctx, attn_qk_norm[l], None, n_ctx, n_ctx)
        att_l = _attention(qn_lat, [(kv_ctx, n_ctx), (kv_lat, n_lat)], bsz, 256)
        w_out_b = w_out[l].astype(BF16)
        x_lat = _outproj(gla_lf, gla_lb, gdn_lf, gdn_lb, p_lat, att_l, x_lat, mods, lat_row_ln, w_out_b,
                         gla_norm[l], gdn_norm[l], ln_gain[l, 0], ln_bias[l, 0], tm_lat_ln)
        r_hi, r_lo = _split_bf16(jnp.pad(router[l], ((0, 0), (0, 128 - N_EXPERTS))))
        if not last:
            att_c = _attention(qn_ctx, [(kv_ctx, n_ctx)], bsz, n_ctx)
            x_ctx = _outproj(gla_cf, gla_cb, gdn_cf, gdn_cb, p_ctx, att_c, x_ctx, mods, ctx_row, w_out_b,
                             gla_norm[l], gdn_norm[l], ln_gain[l, 0], ln_bias[l, 0], n_ctx)
            x_lat, x_ctx = _moe([x_lat, x_ctx], mods, [lat_row_ln, ctx_row], r_hi, r_lo, w1, w3, w2, l,
                                ln_gain[l, 1], ln_bias[l, 1], bsz, [tm_lat_ln, n_ctx])
        else:
            (x_lat,) = _moe([x_lat], mods, [lat_row_ln], r_hi, r_lo, w1, w3, w2, l,
                            ln_gain[l, 1], ln_bias[l, 1], bsz, [tm_lat_ln])
    return x_lat.reshape(bsz, n_lat, d)
```

```python
import functools

import jax
import jax.numpy as jnp
from jax import lax
from jax.experimental import pallas as pl
from jax.experimental.pallas import tpu as pltpu

F32 = jnp.float32
BF16 = jnp.bfloat16

D_MODEL = 2048
DEPTH = 2
GRID_W = 64
HEAD_DIM = 128
CHUNK = 64
GLA_HEADS = 4
GLA_DK = 64
GLA_DV = 128
GLA_RANK = 16
GLA_GATE_NORM = 16.0
GDN_HEADS = 4
GDN_DK = 128
GDN_DV = 128
GDN_CONV = 5
ATTN_HEADS = 8
ATTN_KV_HEADS = 2
ATTN_GROUP = ATTN_HEADS // ATTN_KV_HEADS
ROPE_THETA = 10000.0
N_EXPERTS = 16
EC_FACTOR = 2
N_MOD = 6
DEEPNORM_ALPHA = (2 * DEPTH) ** 0.25

_SRC = dict(gla_q=(0, 256), gla_k=(256, 256), gla_v=(512, 512), gla_g=(1024, 512), gla_r=(1536, 32),
            gdn_q=(1568, 512), gdn_k=(2080, 512), gdn_v=(2592, 512), gdn_z=(3104, 512), gdn_b=(3616, 8),
            gdn_a=(3624, 8), att_q=(3632, 1024), att_k=(4656, 256), att_v=(4912, 256))
_MAIN_ORDER = ("att_q", "gla_v", "gdn_q", "gdn_k", "gdn_v", "gla_g", "gdn_z", "gla_q", "gla_k", "att_k", "att_v")
_SMALL_ORDER = ("gla_r", "gdn_b", "gdn_a")
_OFF = {}
_o = 0
for _n in _MAIN_ORDER:
    _OFF[_n] = _o
    _o += _SRC[_n][1]
MAIN_W = _o
_o = 0
for _n in _SMALL_ORDER:
    _OFF[_n] = _o
    _o += _SRC[_n][1]
SMALL_W = 128
MIX_HW = 512
SCAN_TB = 256

VMEM_LIMIT = 56 * 1024 * 1024

_NT = (((1,), (1,)), ((), ()))
_TN = (((0,), (0,)), ((), ()))


def _cparams(n_axes):
    return pltpu.CompilerParams(dimension_semantics=("arbitrary",) * n_axes, vmem_limit_bytes=VMEM_LIMIT)


def _split_bf16(a):
    hi = a.astype(BF16)
    lo = (a - hi.astype(F32)).astype(BF16)
    return hi, lo


def _dot(a, b):
    return jnp.dot(a, b, preferred_element_type=F32)


def _dot2(a, m):
    hi, lo = _split_bf16(a)
    return _dot(hi, m) + _dot(lo, m)


def _dot2_left(m, a):
    hi, lo = _split_bf16(a)
    return _dot(m, hi) + _dot(m, lo)


def _sigmoid(x):
    return 1.0 / (1.0 + jnp.exp(-x))


def _softplus(x):
    return jnp.maximum(x, 0.0) + jnp.log1p(jnp.exp(-jnp.abs(x)))


def _ada_kernel(c_ref, w_ref, b_ref, o_ref):
    c = c_ref[...]
    s = (c * _sigmoid(c)).astype(BF16)
    o_ref[...] = _dot(s, w_ref[...].astype(BF16)) + b_ref[...]


def _ada(cond8, w_ada, b_ada, tn=1024):
    depth, d, n = w_ada.shape
    return pl.pallas_call(
        _ada_kernel,
        out_shape=jax.ShapeDtypeStruct((depth, 8, n), F32),
        grid=(depth, n // tn),
        in_specs=[pl.BlockSpec((8, d), lambda l, j: (0, 0)),
                  pl.BlockSpec((None, d, tn), lambda l, j: (l, 0, j)),
                  pl.BlockSpec((None, 1, tn), lambda l, j: (l, 0, j))],
        out_specs=pl.BlockSpec((None, 8, tn), lambda l, j: (l, 0, j)),
        compiler_params=_cparams(2),
        name="ada",
    )(cond8, w_ada, b_ada.reshape(depth, 1, n))


def _inproj_kernel(x_ref, shift_ref, scale_ref, w_ref, wsh_ref, wsl_ref, o_ref, os_ref, h_ref):
    @pl.when(pl.program_id(1) == 0)
    def _():
        h = x_ref[...] * (1.0 + scale_ref[...]) + shift_ref[...]
        hi, lo = _split_bf16(h)
        h_ref[...] = hi
        wsh = wsh_ref[...]
        os_ref[...] = _dot(hi, wsh) + _dot(lo, wsh) + _dot(hi, wsl_ref[...])

    o_ref[...] = _dot(h_ref[...], w_ref[...])


def _inproj(x2d, mods, row_of_tile, w_main, ws_hi, ws_lo, tm, tn=1024):
    rows, d = x2d.shape
    n = w_main.shape[1]
    return pl.pallas_call(
        _inproj_kernel,
        out_shape=(jax.ShapeDtypeStruct((rows, n), F32), jax.ShapeDtypeStruct((rows, SMALL_W), F32)),
        grid=(rows // tm, n // tn),
        in_specs=[pl.BlockSpec((tm, d), lambda i, j: (i, 0)),
                  pl.BlockSpec((None, 1, d), lambda i, j: (row_of_tile(i) * N_MOD + 0, 0, 0)),
                  pl.BlockSpec((None, 1, d), lambda i, j: (row_of_tile(i) * N_MOD + 1, 0, 0)),
                  pl.BlockSpec((d, tn), lambda i, j: (0, j)),
                  pl.BlockSpec((d, SMALL_W), lambda i, j: (0, 0)),
                  pl.BlockSpec((d, SMALL_W), lambda i, j: (0, 0))],
        out_specs=(pl.BlockSpec((tm, tn), lambda i, j: (i, j)),
                   pl.BlockSpec((tm, SMALL_W), lambda i, j: (i, 0))),
        scratch_shapes=[pltpu.VMEM((tm, d), BF16)],
        compiler_params=_cparams(2),
        name="inproj",
    )(x2d, mods, mods, w_main, ws_hi, ws_lo)


def _rms_rope(x, gain, cos, sin):
    y = x * lax.rsqrt(jnp.mean(x * x, axis=-1, keepdims=True) + 1e-6) * gain
    if cos is not None:
        lane = lax.broadcasted_iota(jnp.int32, y.shape, 1)
        partner = jnp.where((lane % 64) < 32, pltpu.roll(y, 96, 1), pltpu.roll(y, 32, 1))
        y = y * cos + partner * sin
    return y


KV_W = ATTN_KV_HEADS * HEAD_DIM + ATTN_KV_HEADS * 2 * HEAD_DIM
LOG2E = 1.4426950408889634


def _qkv_prep_kernel(*refs, rope):
    if rope:
        q_ref, kv_ref, gain_ref, cos_ref, sin_ref, qo_ref, kvo_ref = refs
        cos, sin = cos_ref[...], sin_ref[...]
    else:
        q_ref, kv_ref, gain_ref, qo_ref, kvo_ref = refs
        cos = sin = None
    q = q_ref[...]
    q_scale = (HEAD_DIM ** -0.5) * LOG2E
    qo_ref[...] = jnp.concatenate(
        [(_rms_rope(q[:, g * HEAD_DIM:(g + 1) * HEAD_DIM], gain_ref[0:1, :], cos, sin) * q_scale).astype(BF16)
         for g in range(ATTN_HEADS)], axis=1)
    kv = kv_ref[...]
    outs = [_rms_rope(kv[:, h * HEAD_DIM:(h + 1) * HEAD_DIM], gain_ref[1:2, :], cos, sin).astype(BF16)
            for h in range(ATTN_KV_HEADS)]
    ones = jnp.ones((kv.shape[0], HEAD_DIM), BF16)
    for h in range(ATTN_KV_HEADS):
        outs += [kv[:, (ATTN_KV_HEADS + h) * HEAD_DIM:(ATTN_KV_HEADS + h + 1) * HEAD_DIM].astype(BF16), ones]
    kvo_ref[...] = jnp.concatenate(outs, axis=1)


def _qkv_prep(p, gain, rope_tabs, n_per_sample, tm):
    rows = p.shape[0]
    qw = ATTN_HEADS * HEAD_DIM
    kvw = 2 * ATTN_KV_HEADS * HEAD_DIM
    per = n_per_sample // tm
    in_specs = [pl.BlockSpec((tm, qw), lambda i: (i, _OFF["att_q"] // qw)),
                pl.BlockSpec((tm, kvw), lambda i: (i, _OFF["att_k"] // kvw)),
                pl.BlockSpec((2, HEAD_DIM), lambda i: (0, 0))]
    args = [p, p, gain]
    if rope_tabs is not None:
        in_specs += [pl.BlockSpec((tm, HEAD_DIM), lambda i: (i % per, 0))] * 2
        args += list(rope_tabs)
    return pl.pallas_call(
        functools.partial(_qkv_prep_kernel, rope=rope_tabs is not None),
        out_shape=(jax.ShapeDtypeStruct((rows, qw), BF16), jax.ShapeDtypeStruct((rows, KV_W), BF16)),
        grid=(rows // tm,),
        in_specs=in_specs,
        out_specs=(pl.BlockSpec((tm, qw), lambda i: (i, 0)), pl.BlockSpec((tm, KV_W), lambda i: (i, 0))),
        compiler_params=_cparams(1),
        name="qkv_prep",
    )(*args)


def _attn_kernel(*refs, n_seg):
    q_ref = refs[0]
    k_refs = refs[1:1 + 2 * n_seg:2]
    v_refs = refs[2:2 + 2 * n_seg:2]
    o_ref = refs[1 + 2 * n_seg]
    outs = []
    for g in range(ATTN_GROUP):
        qg = q_ref[:, g * HEAD_DIM:(g + 1) * HEAD_DIM]
        s = jnp.concatenate([lax.dot_general(qg, k_ref[...], _NT, preferred_element_type=F32) for k_ref in k_refs],
                            axis=1)
        p = jnp.exp2(s - jnp.max(s, axis=-1, keepdims=True)).astype(BF16)
        acc = None
        off = 0
        for v_ref in v_refs:
            n = v_ref.shape[0]
            part = _dot(p[:, off:off + n], v_ref[...])
            acc = part if acc is None else acc + part
            off += n
        outs.append((acc[:, :HEAD_DIM] / acc[:, HEAD_DIM:]).astype(o_ref.dtype))
    o_ref[...] = jnp.concatenate(outs, axis=1)


def _attention(qn, kv_segs, bsz, tq):
    rows = qn.shape[0]
    nq = rows // bsz // tq
    gw = ATTN_GROUP * HEAD_DIM
    in_specs = [pl.BlockSpec((tq, gw), lambda b, h, i: (b * nq + i, h))]
    args = [qn]
    for arr, n in kv_segs:
        in_specs.append(pl.BlockSpec((n, HEAD_DIM), lambda b, h, i: (b, h)))
        in_specs.append(pl.BlockSpec((n, 2 * HEAD_DIM), lambda b, h, i: (b, 1 + h)))
        args += [arr, arr]
    return pl.pallas_call(
        functools.partial(_attn_kernel, n_seg=len(kv_segs)),
        out_shape=jax.ShapeDtypeStruct((rows, ATTN_HEADS * HEAD_DIM), BF16),
        grid=(bsz, ATTN_KV_HEADS, nq),
        in_specs=in_specs,
        out_specs=pl.BlockSpec((tq, gw), lambda b, h, i: (b * nq + i, h)),
        compiler_params=_cparams(3),
        name="attention",
    )(*args)


def _chunk_masks(tb):
    i = lax.broadcasted_iota(jnp.int32, (tb, tb), 0)
    j = lax.broadcasted_iota(jnp.int32, (tb, tb), 1)
    same = (i // CHUNK) == (j // CHUNK)
    return i, j, same


def _gla_kernel(qk_f, v_f, s_f, qk_b, v_b, s_b, wup_ref, bup_ref, tri_ref, st0_ref,
                of_ref, ob_ref, stout_ref, st_scr):
    step = pl.program_id(1)

    @pl.when(step == 0)
    def _():
        st_scr[...] = st0_ref[...]

    tb = qk_f.shape[0]
    nch = tb // CHUNK
    hk = GLA_HEADS * GLA_DK
    lane = lax.broadcasted_iota(jnp.int32, (CHUNK, hk), 1)
    head_of_lane = lane // GLA_DK
    r4 = lax.broadcasted_iota(jnp.int32, (GLA_HEADS * CHUNK, CHUNK), 0) % CHUNK
    c4 = lax.broadcasted_iota(jnp.int32, (GLA_HEADS * CHUNK, CHUNK), 1)
    bd_r = lax.broadcasted_iota(jnp.int32, (MIX_HW, hk), 0) // GLA_DV
    bd_c = lax.broadcasted_iota(jnp.int32, (MIX_HW, hk), 1) // GLA_DK
    bd_mask = bd_r == bd_c

    dirs = []
    for z, (qk_ref, v_ref, s_ref) in enumerate(((qk_f, v_f, s_f), (qk_b, v_b, s_b))):
        qk = qk_ref[...]
        hi, lo = _split_bf16(s_ref[...])
        wh = wup_ref[z, 0]
        logit = _dot(hi, wh) + _dot(lo, wh) + _dot(hi, wup_ref[z, 1]) + bup_ref[z]
        log_a = -_softplus(-logit) / GLA_GATE_NORM
        dirs.append(dict(
            z=z, q=qk[:, :hk] * (GLA_DK ** -0.5), k=qk[:, hk:], v=v_ref[...].astype(BF16),
            cum=_dot2_left(tri_ref[z], log_a),
            mask=(c4 <= r4) if z == 0 else (c4 >= r4), st=st_scr[z], outs=[None] * nch))

    for c_step in range(nch):
        for dr in dirs:
            z = dr["z"]
            c = c_step if z == 0 else nch - 1 - c_step
            last = CHUNK - 1 if z == 0 else 0
            r0 = c * CHUNK
            cum_c = dr["cum"][r0:r0 + CHUNK]
            tot = cum_c[last:last + 1, :]
            q_dec = dr["q"][r0:r0 + CHUNK] * jnp.exp(cum_c)
            k_c = dr["k"][r0:r0 + CHUNK]
            k_inv = (k_c * jnp.exp(-cum_c)).astype(BF16)
            k_dec = (k_c * jnp.exp(tot - cum_c)).astype(BF16)
            q4 = jnp.concatenate([jnp.where(head_of_lane == h, q_dec, 0.0) for h in range(GLA_HEADS)],
                                 axis=0).astype(BF16)
            a4 = lax.dot_general(q4, k_inv, _NT, preferred_element_type=F32)
            a4 = jnp.where(dr["mask"], a4, 0.0).astype(BF16)
            v_c = dr["v"][r0:r0 + CHUNK]
            o4 = _dot(a4, v_c)
            o_intra = jnp.concatenate(
                [o4[h * CHUNK:(h + 1) * CHUNK, h * GLA_DV:(h + 1) * GLA_DV] for h in range(GLA_HEADS)], axis=1)
            o_inter = lax.dot_general(q_dec.astype(BF16), dr["st"].astype(BF16), _NT, preferred_element_type=F32)
            dr["outs"][c] = o_intra + o_inter
            kv_t = lax.dot_general(v_c, k_dec, _TN, preferred_element_type=F32)
            dr["st"] = dr["st"] * jnp.exp(tot) + jnp.where(bd_mask, kv_t, 0.0)

    for dr, o_ref in zip(dirs, (of_ref, ob_ref)):
        o_ref[...] = jnp.concatenate(dr["outs"], axis=0)
        st_scr[dr["z"]] = dr["st"]

    @pl.when(step == pl.num_programs(1) - 1)
    def _():
        stout_ref[...] = st_scr[...]


def _gla(p, ps, t, bsz, wup, bup, tri, st0):
    tb = SCAN_TB
    nblk = t // tb
    qk_col = _OFF["gla_q"] // (2 * GLA_HEADS * GLA_DK)
    v_col = _OFF["gla_v"] // MIX_HW
    fwd = lambda b, i: b * nblk + i
    bwd = lambda b, i: b * nblk + (nblk - 1 - i)
    hk = GLA_HEADS * GLA_DK
    return pl.pallas_call(
        _gla_kernel,
        out_shape=(jax.ShapeDtypeStruct((bsz * t, MIX_HW), F32), jax.ShapeDtypeStruct((bsz * t, MIX_HW), F32),
                   jax.ShapeDtypeStruct((bsz, 2, MIX_HW, hk), F32)),
        grid=(bsz, nblk),
        in_specs=[pl.BlockSpec((tb, 2 * hk), lambda b, i: (fwd(b, i), qk_col)),
                  pl.BlockSpec((tb, MIX_HW), lambda b, i: (fwd(b, i), v_col)),
                  pl.BlockSpec((tb, SMALL_W), lambda b, i: (fwd(b, i), 0)),
                  pl.BlockSpec((tb, 2 * hk), lambda b, i: (bwd(b, i), qk_col)),
                  pl.BlockSpec((tb, MIX_HW), lambda b, i: (bwd(b, i), v_col)),
                  pl.BlockSpec((tb, SMALL_W), lambda b, i: (bwd(b, i), 0)),
                  pl.BlockSpec((2, 2, SMALL_W, hk), lambda b, i: (0, 0, 0, 0)),
                  pl.BlockSpec((2, 1, hk), lambda b, i: (0, 0, 0)),
                  pl.BlockSpec((2, tb, tb), lambda b, i: (0, 0, 0)),
                  pl.BlockSpec((None, 2, MIX_HW, hk), lambda b, i: (b, 0, 0, 0))],
        out_specs=(pl.BlockSpec((tb, MIX_HW), lambda b, i: (fwd(b, i), 0)),
                   pl.BlockSpec((tb, MIX_HW), lambda b, i: (bwd(b, i), 0)),
                   pl.BlockSpec((None, 2, MIX_HW, hk), lambda b, i: (b, 0, 0, 0))),
        scratch_shapes=[pltpu.VMEM((2, MIX_HW, hk), F32)],
        compiler_params=_cparams(2),
        name="gla_scan",
    )(p, p, ps, p, p, ps, wup, bup, tri, st0)


def _gdn_prep_kernel(x_ref, prev_ref, next_ref, w_ref, o_ref):
    i = pl.program_id(1)
    tb = x_ref.shape[0]
    halo = prev_ref.shape[0]
    prev = jnp.where(i > 0, prev_ref[...], 0.0)
    nxt = jnp.where(i < pl.num_programs(1) - 1, next_ref[...], 0.0)
    ext = jnp.concatenate([prev, x_ref[...], nxt], axis=0)
    w = w_ref[...]
    acc = None
    for j in range(GDN_CONV):
        shift = GDN_CONV // 2 - j
        rolled = ext if shift == 0 else pltpu.roll(ext, shift % (tb + 2 * halo), 0)
        term = rolled[halo:halo + tb] * w[j:j + 1, :]
        acc = term if acc is None else acc + term
    y = acc * _sigmoid(acc)
    qk_w = 2 * GDN_HEADS * GDN_DK
    outs = []
    for h in range(2 * GDN_HEADS):
        yh = y[:, h * GDN_DK:(h + 1) * GDN_DK]
        yh = yh * lax.rsqrt(jnp.sum(yh * yh, axis=-1, keepdims=True) + 1e-6)
        if h < GDN_HEADS:
            yh = yh * (GDN_DK ** -0.5)
        outs.append(yh)
    outs.append(y[:, qk_w:])
    o_ref[...] = jnp.concatenate(outs, axis=1)


def _gdn_prep(p, t, bsz, conv_w):
    tb = SCAN_TB
    halo = 8
    nblk = t // tb
    width = conv_w.shape[1]
    col = _OFF["gdn_q"] // width
    n_halo_blocks = bsz * t // halo
    per = tb // halo
    return pl.pallas_call(
        _gdn_prep_kernel,
        out_shape=jax.ShapeDtypeStruct((bsz * t, width), F32),
        grid=(bsz, nblk),
        in_specs=[pl.BlockSpec((tb, width), lambda b, i: (b * nblk + i, col)),
                  pl.BlockSpec((halo, width), lambda b, i: (jnp.maximum((b * nblk + i) * per - 1, 0), col)),
                  pl.BlockSpec((halo, width),
                               lambda b, i: (jnp.minimum((b * nblk + i + 1) * per, n_halo_blocks - 1), col)),
                  pl.BlockSpec((GDN_CONV, width), lambda b, i: (0, 0))],
        out_specs=pl.BlockSpec((tb, width), lambda b, i: (b * nblk + i, 0)),
        compiler_params=_cparams(2),
        name="gdn_conv",
    )(p, p, p, conv_w)


def _gdn_kernel(x_f, s_f, x_b, s_b, sel_ref, ealog_ref, dtb_ref, tri_ref, s0_ref,
                of_ref, ob_ref, sout_ref, s_scr):
    step = pl.program_id(1)

    @pl.when(step == 0)
    def _():
        s_scr[...] = s0_ref[...]

    tb = x_f.shape[0]
    nch = tb // CHUNK
    i_idx, j_idx, same = _chunk_masks(tb)
    xor = i_idx ^ j_idx
    eye = jnp.where(i_idx == j_idx, 1.0, 0.0)
    hw = GDN_HEADS * GDN_DK

    chains = []
    for z, (x_ref, s_ref) in enumerate(((x_f, s_f), (x_b, s_b))):
        before = (j_idx <= i_idx) if z == 0 else (j_idx >= i_idx)
        incl = same & before
        strict = incl & (i_idx != j_idx)
        sm = s_ref[...]
        beta_all = _sigmoid(sm)
        g_all = -ealog_ref[...] * _softplus(sm + dtb_ref[...])
        b_sel = _dot2(beta_all, sel_ref[2 * z])
        g_sel = _dot2(g_all, sel_ref[2 * z + 1])
        c_col = _dot2_left(tri_ref[z], g_sel)
        x = x_ref[...]
        for h in range(GDN_HEADS):
            ch = dict(z=z, h=h, incl=incl)
            ch["q"] = x[:, h * GDN_DK:(h + 1) * GDN_DK]
            ch["k"] = x[:, hw + h * GDN_DK:hw + (h + 1) * GDN_DK]
            bc = b_sel[:, h * GDN_DK:(h + 1) * GDN_DK]
            cc = c_col[:, h * GDN_DK:(h + 1) * GDN_DK]
            ch["cc"] = cc
            kb = ch["k"] * bc
            ch["vb"] = (x[:, 2 * hw + h * GDN_DV:2 * hw + (h + 1) * GDN_DV] * bc).astype(BF16)
            ch["kbg"] = (kb * jnp.exp(cc)).astype(BF16)
            ch["k_bf"] = ch["k"].astype(BF16)
            c_row = cc.T[0:1, :]
            dm = jnp.concatenate([cc] * (tb // GDN_DK), axis=1) - c_row
            ch["gamma"] = jnp.where(incl, jnp.exp(jnp.where(incl, dm, 0.0)), 0.0)
            kk = lax.dot_general(kb.astype(BF16), ch["k_bf"], _NT, preferred_element_type=F32)
            ch["l_mat"] = jnp.where(strict, kk * ch["gamma"], 0.0)
            ch["t_inv"] = eye - jnp.where(xor == 1, ch["l_mat"], 0.0)
            chains.append(ch)

    s = 2
    while s < CHUNK:
        lvl = (xor >= s) & (xor < 2 * s)
        for ch in chains:
            ch["t_bf"] = ch["t_inv"].astype(BF16)
            ch["m1"] = _dot(ch["t_bf"], jnp.where(lvl, ch["l_mat"], 0.0).astype(BF16)).astype(BF16)
        for ch in chains:
            ch["t_inv"] = ch["t_inv"] - _dot(ch["m1"], ch["t_bf"])
        s *= 2

    for ch in chains:
        t_bf = ch["t_inv"].astype(BF16)
        ch["u"] = _dot(t_bf, ch["vb"])
        ch["w"] = _dot(t_bf, ch["kbg"]).astype(BF16)
        qk = lax.dot_general(ch["q"].astype(BF16), ch["k_bf"], _NT, preferred_element_type=F32)
        ch["a_int"] = jnp.where(ch["incl"], qk * ch["gamma"], 0.0).astype(BF16)
        ch["state"] = s_scr[ch["z"], ch["h"]]
        ch["outs"] = [None] * nch

    zeros = jnp.zeros((CHUNK, GDN_DV), BF16)
    for c_step in range(nch):
        for ch in chains:
            c = c_step if ch["z"] == 0 else nch - 1 - c_step
            last = CHUNK - 1 if ch["z"] == 0 else 0
            r0 = c * CHUNK
            cc_c = ch["cc"][r0:r0 + CHUNK]
            g_last = cc_c[last:last + 1, :]
            k_dec = (ch["k"][r0:r0 + CHUNK] * jnp.exp(g_last - cc_c)).astype(BF16)
            q_dec = (ch["q"][r0:r0 + CHUNK] * jnp.exp(cc_c)).astype(BF16)
            s_bf = ch["state"].astype(BF16)
            v_new = ch["u"][r0:r0 + CHUNK] - _dot(ch["w"][r0:r0 + CHUNK], s_bf)
            v_new_bf = v_new.astype(BF16)
            v_pad = jnp.concatenate([v_new_bf if cc_i == c else zeros for cc_i in range(nch)], axis=0)
            ch["outs"][c] = _dot(q_dec, s_bf) + _dot(ch["a_int"][r0:r0 + CHUNK], v_pad)
            ch["state"] = (ch["state"] * jnp.exp(g_last)
                           + lax.dot_general(k_dec, v_new_bf, _TN, preferred_element_type=F32))

    for z, o_ref in enumerate((of_ref, ob_ref)):
        mine = [ch for ch in chains if ch["z"] == z]
        o_ref[...] = jnp.concatenate([jnp.concatenate(ch["outs"], axis=0) for ch in mine], axis=1)
        for ch in mine:
            s_scr[z, ch["h"]] = ch["state"]

    @pl.when(step == pl.num_programs(1) - 1)
    def _():
        sout_ref[...] = s_scr[...]


def _gdn(xc, ps, t, bsz, sel, ealog, dtb, tri, s0):
    tb = SCAN_TB
    nblk = t // tb
    width = xc.shape[1]
    fwd = lambda b, i: b * nblk + i
    bwd = lambda b, i: b * nblk + (nblk - 1 - i)
    return pl.pallas_call(
        _gdn_kernel,
        out_shape=(jax.ShapeDtypeStruct((bsz * t, MIX_HW), F32), jax.ShapeDtypeStruct((bsz * t, MIX_HW), F32),
                   jax.ShapeDtypeStruct((bsz, 2, GDN_HEADS, GDN_DK, GDN_DV), F32)),
        grid=(bsz, nblk),
        in_specs=[pl.BlockSpec((tb, width), lambda b, i: (fwd(b, i), 0)),
                  pl.BlockSpec((tb, SMALL_W), lambda b, i: (fwd(b, i), 0)),
                  pl.BlockSpec((tb, width), lambda b, i: (bwd(b, i), 0)),
                  pl.BlockSpec((tb, SMALL_W), lambda b, i: (bwd(b, i), 0)),
                  pl.BlockSpec((4, SMALL_W, MIX_HW), lambda b, i: (0, 0, 0)),
                  pl.BlockSpec((1, SMALL_W), lambda b, i: (0, 0)),
                  pl.BlockSpec((1, SMALL_W), lambda b, i: (0, 0)),
                  pl.BlockSpec((2, tb, tb), lambda b, i: (0, 0, 0)),
                  pl.BlockSpec((None, 2, GDN_HEADS, GDN_DK, GDN_DV), lambda b, i: (b, 0, 0, 0, 0))],
        out_specs=(pl.BlockSpec((tb, MIX_HW), lambda b, i: (fwd(b, i), 0)),
                   pl.BlockSpec((tb, MIX_HW), lambda b, i: (bwd(b, i), 0)),
                   pl.BlockSpec((None, 2, GDN_HEADS, GDN_DK, GDN_DV), lambda b, i: (b, 0, 0, 0, 0))),
        scratch_shapes=[pltpu.VMEM((2, GDN_HEADS, GDN_DK, GDN_DV), F32)],
        compiler_params=_cparams(2),
        name="gdn_scan",
    )(xc, ps, xc, ps, sel, ealog, dtb, tri, s0)


def _layer_norm_rows(z, gain, bias):
    mu = jnp.mean(z, axis=-1, keepdims=True)
    zc = z - mu
    var = jnp.mean(zc * zc, axis=-1, keepdims=True)
    return zc * lax.rsqrt(var + 1e-5) * gain + bias


def _mixer_finish(o, gate, gain, n_heads, dv):
    outs = []
    for h in range(n_heads):
        oh = o[:, h * dv:(h + 1) * dv]
        oh = oh * lax.rsqrt(jnp.mean(oh * oh, axis=-1, keepdims=True) + 1e-6) * gain
        gh = gate[:, h * dv:(h + 1) * dv]
        outs.append((oh * (gh * _sigmoid(gh))).astype(BF16))
    return outs


def _outproj_kernel(glaf_ref, glab_ref, gdnf_ref, gdnb_ref, g_ref, z_ref, att_ref, x_ref, gate_ref, w_ref,
                    ngla_ref, ngdn_ref, gain_ref, bias_ref, o_ref):
    parts = _mixer_finish(glaf_ref[...] + glab_ref[...], g_ref[...], ngla_ref[...], GLA_HEADS, GLA_DV)
    parts += _mixer_finish(gdnf_ref[...] + gdnb_ref[...], z_ref[...], ngdn_ref[...], GDN_HEADS, GDN_DV)
    parts.append(att_ref[...])
    y = _dot(jnp.concatenate(parts, axis=1), w_ref[...])
    z = DEEPNORM_ALPHA * x_ref[...] + gate_ref[...] * y
    o_ref[...] = _layer_norm_rows(z, gain_ref[...], bias_ref[...])


def _outproj(gla_f, gla_b, gdn_f, gdn_b, p, att, x2d, mods, row_of_tile, w_out, n_gla, n_gdn, gain, bias, tm):
    rows, d = x2d.shape
    g_col = _OFF["gla_g"] // MIX_HW
    z_col = _OFF["gdn_z"] // MIX_HW
    mix = lambda: pl.BlockSpec((tm, MIX_HW), lambda i: (i, 0))
    vec = lambda n: pl.BlockSpec((1, n), lambda i: (0, 0))
    return pl.pallas_call(
        _outproj_kernel,
        out_shape=jax.ShapeDtypeStruct((rows, d), F32),
        grid=(rows // tm,),
        in_specs=[mix(), mix(), mix(), mix(),
                  pl.BlockSpec((tm, MIX_HW), lambda i: (i, g_col)),
                  pl.BlockSpec((tm, MIX_HW), lambda i: (i, z_col)),
                  pl.BlockSpec((tm, att.shape[1]), lambda i: (i, 0)),
                  pl.BlockSpec((tm, d), lambda i: (i, 0)),
                  pl.BlockSpec((None, 1, d), lambda i: (row_of_tile(i) * N_MOD + 2, 0, 0)),
                  pl.BlockSpec((w_out.shape[0], d), lambda i: (0, 0)),
                  vec(GLA_DV), vec(GDN_DV), vec(d), vec(d)],
        out_specs=pl.BlockSpec((tm, d), lambda i: (i, 0)),
        compiler_params=_cparams(1),
        name="outproj_ln",
    )(gla_f, gla_b, gdn_f, gdn_b, p, p, att, x2d, mods, w_out, n_gla.reshape(1, -1), n_gdn.reshape(1, -1),
      gain.reshape(1, d), bias.reshape(1, d))


def _router_kernel(x_ref, shift_ref, scale_ref, rh_ref, rl_ref, h_ref, lg_ref):
    h = x_ref[...] * (1.0 + scale_ref[...]) + shift_ref[...]
    hi, lo = _split_bf16(h)
    h_ref[...] = h
    rh = rh_ref[...]
    lg_ref[...] = _dot(hi, rh) + _dot(lo, rh) + _dot(hi, rl_ref[...])


def _router(x2d, mods, row_of_tile, r_hi, r_lo, tm):
    rows, d = x2d.shape
    return pl.pallas_call(
        _router_kernel,
        out_shape=(jax.ShapeDtypeStruct((rows, d), F32), jax.ShapeDtypeStruct((rows, 128), F32)),
        grid=(rows // tm,),
        in_specs=[pl.BlockSpec((tm, d), lambda i: (i, 0)),
                  pl.BlockSpec((None, 1, d), lambda i: (row_of_tile(i) * N_MOD + 3, 0, 0)),
                  pl.BlockSpec((None, 1, d), lambda i: (row_of_tile(i) * N_MOD + 4, 0, 0)),
                  pl.BlockSpec((d, 128), lambda i: (0, 0)),
                  pl.BlockSpec((d, 128), lambda i: (0, 0))],
        out_specs=(pl.BlockSpec((tm, d), lambda i: (i, 0)), pl.BlockSpec((tm, 128), lambda i: (i, 0))),
        compiler_params=_cparams(1),
        name="router",
    )(x2d, mods, mods, r_hi, r_lo)


def _row_copy(h_hbm, land, sem, src_row, dst_row):
    return pltpu.make_async_copy(h_hbm.at[pl.ds(src_row, 1), :], land.at[pl.ds(dst_row, 1), :], sem)


def _ffn_kernel(idx_ref, h_hbm, g_ref, w1_ref, w3_ref, w2_ref, o_ref, land, x_scr, h_scr, sem, *, nf):
    e = pl.program_id(0)
    s = pl.program_id(1)
    n_e = pl.num_programs(0)
    tf = w1_ref.shape[1]
    n_rows = land.shape[0]
    per = n_rows // nf

    def wait_all_rows():
        pltpu.make_async_copy(h_hbm.at[pl.ds(0, n_rows), :], land, sem).wait()

    @pl.when(s == 0)
    def _():
        @pl.when(e == 0)
        def _():
            def issue(r, carry):
                _row_copy(h_hbm, land, sem, idx_ref[0, r], r).start()
                return carry
            lax.fori_loop(0, n_rows, issue, 0)

        wait_all_rows()
        x_scr[...] = land[...].astype(BF16)

    @pl.when(s < nf)
    def _():
        nxt = lax.rem(e + 1, n_e)
        base = s * per
        for i in range(per):
            _row_copy(h_hbm, land, sem, idx_ref[nxt, base + i], base + i).start()
        x = x_scr[...]
        a = _dot(x, w1_ref[...].astype(BF16))
        u = _dot(x, w3_ref[...].astype(BF16))
        hmid = (a * _sigmoid(a) * u).astype(BF16)
        for f in range(nf):
            @pl.when(s == f)
            def _():
                h_scr[:, f * tf:(f + 1) * tf] = hmid

    @pl.when(s >= nf)
    def _():
        o_ref[...] = _dot(h_scr[...], w2_ref[...].astype(BF16)) * g_ref[...]

    @pl.when((e == n_e - 1) & (s == pl.num_programs(1) - 1))
    def _():
        wait_all_rows()


def _ffn(h_all, rows_idx, gates, w1, w3, w2, layer, tf=512, td=256):
    n_e, r = rows_idx.shape
    d = h_all.shape[1]
    ff = w1.shape[3]
    nf = ff // tf
    assert r % nf == 0
    up = lambda e, s, idx: (layer, e, 0, jnp.minimum(s, nf - 1))
    down = lambda e, s, idx: (layer, e, 0, jnp.maximum(s - nf, 0))
    grid_spec = pltpu.PrefetchScalarGridSpec(
        num_scalar_prefetch=1,
        grid=(n_e, nf + d // td),
        in_specs=[pl.BlockSpec(memory_space=pl.ANY),
                  pl.BlockSpec((None, r, 1), lambda e, s, idx: (e, 0, 0)),
                  pl.BlockSpec((None, None, d, tf), up),
                  pl.BlockSpec((None, None, d, tf), up),
                  pl.BlockSpec((None, None, ff, td), down)],
        out_specs=pl.BlockSpec((None, r, td), lambda e, s, idx: (e, 0, jnp.maximum(s - nf, 0))),
        scratch_shapes=[pltpu.VMEM((r, d), F32), pltpu.VMEM((r, d), BF16), pltpu.VMEM((r, ff), BF16),
                        pltpu.SemaphoreType.DMA(())],
    )
    return pl.pallas_call(
        functools.partial(_ffn_kernel, nf=nf),
        out_shape=jax.ShapeDtypeStruct((n_e, r, d), F32),
        grid_spec=grid_spec,
        compiler_params=_cparams(2),
        name="expert_ffn",
    )(rows_idx, h_all, gates, w1, w3, w2)


COMB_TT = 256
COMB_W = 64


def _combine_kernel(start_ref, y_hbm, pos_ref, x_ref, gate_ref, gain_ref, bias_ref, o_ref,
                    land, extra, acc_scr, sem, sem_x, *, tile0):
    i = pl.program_id(0)
    n_steps = pl.num_programs(0)
    n_e, tt = pos_ref.shape
    w = COMB_W
    d = x_ref.shape[1]
    r_total = y_hbm.shape[1]
    tile = tile0 + i

    adv = w - 8

    def base_of(t, e, k):
        aligned = lax.shift_right_logical(start_ref[t, e] + k * adv, 3) * 8
        return pl.multiple_of(jnp.minimum(aligned, r_total - w), 8)

    def window_copy(t, e, slot):
        return pltpu.make_async_copy(y_hbm.at[e, pl.ds(base_of(t, e, 0), w), :],
                                     land.at[slot, pl.ds(e * w, w), :], sem.at[slot])

    slot = lax.rem(i, 2)

    @pl.when(i == 0)
    def _():
        for e in range(n_e):
            window_copy(tile, e, 0).start()

    @pl.when(i + 1 < n_steps)
    def _():
        for e in range(n_e):
            window_copy(tile + 1, e, 1 - slot).start()

    for e in range(n_e):
        window_copy(tile, e, slot).wait()

    row_sel = lax.broadcasted_iota(jnp.int32, (w, tt), 0)
    row_val = lax.broadcasted_iota(jnp.int32, (w, d), 0)

    def select(e, base, lo, hi):
        pe = pos_ref[e:e + 1, :]
        hit = (pe - base == row_sel) & (pe >= lo) & (pe < hi)
        return jnp.where(hit, 1.0, 0.0).astype(BF16)

    def masked(rows, base, lo, hi):
        yrow = row_val + base
        return jnp.where((yrow >= lo) & (yrow < hi), rows, 0.0).astype(BF16)

    sels, vals = [], []
    for e in range(n_e):
        st, en = start_ref[tile, e], start_ref[tile + 1, e]
        base = base_of(tile, e, 0)
        hi = jnp.minimum(st + adv, en)
        sels.append(select(e, base, st, hi))
        vals.append(masked(land[slot, e * w:(e + 1) * w, :], base, st, hi))
    acc_scr[...] = lax.dot_general(jnp.concatenate(sels, axis=0), jnp.concatenate(vals, axis=0), _TN,
                                   preferred_element_type=F32)

    for e in range(n_e):
        st, en = start_ref[tile, e], start_ref[tile + 1, e]
        n_win = (en - st + adv - 1) // adv

        def more(k, carry, e=e, st=st, en=en):
            base = base_of(tile, e, k)
            cp = pltpu.make_async_copy(y_hbm.at[e, pl.ds(base, w), :], extra, sem_x)
            cp.start()
            cp.wait()
            lo = st + k * adv
            hi = jnp.minimum(lo + adv, en)
            acc_scr[...] += lax.dot_general(select(e, base, lo, hi), masked(extra[...], base, lo, hi), _TN,
                                            preferred_element_type=F32)
            return carry
        lax.fori_loop(1, n_win, more, 0)

    z = DEEPNORM_ALPHA * x_ref[...] + gate_ref[...] * acc_scr[...]
    o_ref[...] = _layer_norm_rows(z, gain_ref[...], bias_ref[...])


def _combine(x2d, y, pos_all, starts, tile0, mods, row_of_tile, gain, bias):
    rows, d = x2d.shape
    n_e = y.shape[0]
    tt = COMB_TT
    grid_spec = pltpu.PrefetchScalarGridSpec(
        num_scalar_prefetch=1,
        grid=(rows // tt,),
        in_specs=[pl.BlockSpec(memory_space=pl.ANY),
                  pl.BlockSpec((n_e, tt), lambda i, st: (0, tile0 + i)),
                  pl.BlockSpec((tt, d), lambda i, st: (i, 0)),
                  pl.BlockSpec((None, 1, d), lambda i, st: (row_of_tile(i) * N_MOD + 5, 0, 0)),
                  pl.BlockSpec((1, d), lambda i, st: (0, 0)),
                  pl.BlockSpec((1, d), lambda i, st: (0, 0))],
        out_specs=pl.BlockSpec((tt, d), lambda i, st: (i, 0)),
        scratch_shapes=[pltpu.VMEM((2, n_e * COMB_W, d), F32), pltpu.VMEM((COMB_W, d), F32),
                        pltpu.VMEM((tt, d), F32), pltpu.SemaphoreType.DMA((2,)), pltpu.SemaphoreType.DMA(())],
    )
    return pl.pallas_call(
        functools.partial(_combine_kernel, tile0=tile0),
        out_shape=jax.ShapeDtypeStruct((rows, d), F32),
        grid_spec=grid_spec,
        compiler_params=_cparams(1),
        name="combine_ln",
    )(starts, y, pos_all, x2d, mods, gain.reshape(1, d), bias.reshape(1, d))


def _rope_tables(n_tokens):
    rows = n_tokens // GRID_W
    row = jnp.broadcast_to(jnp.arange(rows)[:, None], (rows, GRID_W)).reshape(-1).astype(F32)
    col = jnp.broadcast_to(jnp.arange(GRID_W)[None, :], (rows, GRID_W)).reshape(-1).astype(F32)
    half = HEAD_DIM // 2
    inv = ROPE_THETA ** (-jnp.arange(0, half, 2, dtype=F32) / half)
    a_row = row[:, None] * inv
    a_col = col[:, None] * inv
    cos_t = jnp.concatenate([jnp.cos(a_row), jnp.cos(a_row), jnp.cos(a_col), jnp.cos(a_col)], axis=-1)
    sin_t = jnp.concatenate([-jnp.sin(a_row), jnp.sin(a_row), -jnp.sin(a_col), jnp.sin(a_col)], axis=-1)
    return cos_t, sin_t


def _permute_w_in(w):
    main = jnp.concatenate([w[:, _SRC[n][0]:_SRC[n][0] + _SRC[n][1]] for n in _MAIN_ORDER], axis=1)
    small = jnp.concatenate([w[:, _SRC[n][0]:_SRC[n][0] + _SRC[n][1]] for n in _SMALL_ORDER], axis=1)
    small = jnp.pad(small, ((0, 0), (0, SMALL_W - small.shape[1])))
    return main, small


def _scan_tri(tb):
    i = jnp.arange(tb)[:, None]
    j = jnp.arange(tb)[None, :]
    same = (i // CHUNK) == (j // CHUNK)
    return jnp.stack([same & (j <= i), same & (j >= i)]).astype(BF16)


def _gla_params(w_up, b_up):
    hk = GLA_HEADS * GLA_DK
    w = jnp.zeros((2, SMALL_W, hk), F32)
    for z in range(2):
        r0 = _OFF["gla_r"] + z * GLA_RANK
        w = w.at[z, r0:r0 + GLA_RANK].set(w_up[z])
    hi, lo = _split_bf16(w)
    return jnp.stack([hi, lo], axis=1), b_up.reshape(2, 1, hk)


def _gdn_params(a_log, dt_bias):
    sel = jnp.zeros((4, SMALL_W, MIX_HW), F32)
    ealog = jnp.zeros((1, SMALL_W), F32)
    dtb = jnp.zeros((1, SMALL_W), F32)
    for z in range(2):
        for h in range(GDN_HEADS):
            lane_b = _OFF["gdn_b"] + z * GDN_HEADS + h
            lane_a = _OFF["gdn_a"] + z * GDN_HEADS + h
            sel = sel.at[2 * z, lane_b, h * GDN_DK:(h + 1) * GDN_DK].set(1.0)
            sel = sel.at[2 * z + 1, lane_a, h * GDN_DK:(h + 1) * GDN_DK].set(1.0)
            ealog = ealog.at[0, lane_a].set(jnp.exp(a_log[z, h]))
            dtb = dtb.at[0, lane_a].set(dt_bias[z, h])
    return sel.astype(BF16), ealog, dtb


def _moe(x2d_sets, mods, row_fns, r_hi, r_lo, w1, w3, w2, layer, gain, bias, bsz, tms):
    hs, gates, flat_idx, row0s = [], [], [], []
    row0 = 0
    for x2d, row_fn, tm in zip(x2d_sets, row_fns, tms):
        h, lg = _router(x2d, mods, row_fn, r_hi, r_lo, tm)
        t = x2d.shape[0] // bsz
        cap = EC_FACTOR * t // N_EXPERTS
        aff = jax.nn.softmax(lg[:, :N_EXPERTS].reshape(bsz, t, N_EXPERTS), axis=-1).transpose(0, 2, 1)
        gate, idx = lax.top_k(aff, cap)
        idx, gate = lax.sort((idx, gate), dimension=2, num_keys=1)
        idx = idx + (row0 + jnp.arange(bsz) * t)[:, None, None]
        hs.append(h)
        gates.append(gate.transpose(1, 0, 2).reshape(N_EXPERTS, bsz * cap))
        flat_idx.append(idx.transpose(1, 0, 2).reshape(N_EXPERTS, bsz * cap))
        row0s.append(row0)
        row0 += x2d.shape[0]
    h_all = jnp.concatenate(hs, axis=0) if len(hs) > 1 else hs[0]
    fi = jnp.concatenate(flat_idx, axis=1) if len(hs) > 1 else flat_idx[0]
    gs = jnp.concatenate(gates, axis=1) if len(hs) > 1 else gates[0]
    fi = fi.astype(jnp.int32)
    y = _ffn(h_all, fi, gs[..., None], w1, w3, w2, layer)
    n_rows = fi.shape[1]
    pos_all = jnp.full((N_EXPERTS, row0), -1, jnp.int32).at[jnp.arange(N_EXPERTS)[:, None], fi].set(
        jnp.broadcast_to(jnp.arange(n_rows, dtype=jnp.int32), fi.shape))
    tile_edges = jnp.arange(row0 // COMB_TT + 1, dtype=jnp.int32) * COMB_TT
    starts = jnp.sum(fi[None, :, :] < tile_edges[:, None, None], axis=2, dtype=jnp.int32)
    return [_combine(x2d, y, pos_all, starts, r0 // COMB_TT, mods, row_fn, gain, bias)
            for x2d, row_fn, r0 in zip(x2d_sets, row_fns, row0s)]


def kernel(x, c, ctx, c_ctx, w_ada, b_ada, w_in, w_out, gla_w_up, gla_b_up, gla_norm, gdn_conv, gdn_a_log,
           gdn_dt_bias, gdn_norm, attn_qk_norm, ln_gain, ln_bias, router, w1, w3, w2):
    bsz, n_lat, d = x.shape
    n_ctx = ctx.shape[1]
    depth = w_ada.shape[0]
    cos_t, sin_t = _rope_tables(n_lat)
    tri = _scan_tri(SCAN_TB)

    cond8 = jnp.zeros((8, d), F32).at[:bsz].set(c).at[bsz].set(c_ctx)
    mods_all = _ada(cond8, w_ada, b_ada)

    tm_lat, tm_ctx = 1024, n_ctx
    lat_row = lambda i: i // (n_lat // tm_lat)
    ctx_row = lambda i: bsz + 0 * i
    tm_lat_ln = 256
    lat_row_ln = lambda i: i // (n_lat // tm_lat_ln)

    x_lat = x.reshape(bsz * n_lat, d)
    x_ctx = ctx.reshape(bsz * n_ctx, d)
    for l in range(depth):
        last = l == depth - 1
        mods = mods_all[l].reshape(8 * N_MOD, 1, d)
        w_main, w_small = _permute_w_in(w_in[l])
        w_main = w_main.astype(BF16)
        ws_hi, ws_lo = _split_bf16(w_small)

        p_lat, ps_lat = _inproj(x_lat, mods, lat_row, w_main, ws_hi, ws_lo, tm_lat, tn=512)
        p_ctx, ps_ctx = _inproj(x_ctx, mods, ctx_row, w_main, ws_hi, ws_lo, tm_ctx)

        wup, bup = _gla_params(gla_w_up[l], gla_b_up[l])
        gla_s0 = jnp.zeros((bsz, 2, MIX_HW, GLA_HEADS * GLA_DK), F32)
        gla_cf, gla_cb, gla_s = _gla(p_ctx, ps_ctx, n_ctx, bsz, wup, bup, tri, gla_s0)
        gla_lf, gla_lb, _ = _gla(p_lat, ps_lat, n_lat, bsz, wup, bup, tri, gla_s)

        sel, ealog, dtb = _gdn_params(gdn_a_log[l], gdn_dt_bias[l])
        gdn_s0 = jnp.zeros((bsz, 2, GDN_HEADS, GDN_DK, GDN_DV), F32)
        xc_ctx = _gdn_prep(p_ctx, n_ctx, bsz, gdn_conv[l])
        xc_lat = _gdn_prep(p_lat, n_lat, bsz, gdn_conv[l])
        gdn_cf, gdn_cb, gdn_s = _gdn(xc_ctx, ps_ctx, n_ctx, bsz, sel, ealog, dtb, tri, gdn_s0)
        gdn_lf, gdn_lb, _ = _gdn(xc_lat, ps_lat, n_lat, bsz, sel, ealog, dtb, tri, gdn_s)

        qn_lat, kv_lat = _qkv_prep(p_lat, attn_qk_norm[l], (cos_t, sin_t), n_lat, 512)
        qn_ctx, kv_ctx = _qkv_prep(p_ctx, attn_qk_norm[l], None, n_ctx, n_ctx)
        att_l = _attention(qn_lat, [(kv_ctx, n_ctx), (kv_lat, n_lat)], bsz, 256)
        w_out_b = w_out[l].astype(BF16)
        x_lat = _outproj(gla_lf, gla_lb, gdn_lf, gdn_lb, p_lat, att_l, x_lat, mods, lat_row_ln, w_out_b,
                         gla_norm[l], gdn_norm[l], ln_gain[l, 0], ln_bias[l, 0], tm_lat_ln)
        r_hi, r_lo = _split_bf16(jnp.pad(router[l], ((0, 0), (0, 128 - N_EXPERTS))))
        if not last:
            att_c = _attention(qn_ctx, [(kv_ctx, n_ctx)], bsz, n_ctx)
            x_ctx = _outproj(gla_cf, gla_cb, gdn_cf, gdn_cb, p_ctx, att_c, x_ctx, mods, ctx_row, w_out_b,
                             gla_norm[l], gdn_norm[l], ln_gain[l, 0], ln_bias[l, 0], n_ctx)
            x_lat, x_ctx = _moe([x_lat, x_ctx], mods, [lat_row_ln, ctx_row], r_hi, r_lo, w1, w3, w2, l,
                                ln_gain[l, 1], ln_bias[l, 1], bsz, [tm_lat_ln, n_ctx])
        else:
            (x_lat,) = _moe([x_lat], mods, [lat_row_ln], r_hi, r_lo, w1, w3, w2, l,
                            ln_gain[l, 1], ln_bias[l, 1], bsz, [tm_lat_ln])
    return x_lat.reshape(bsz, n_lat, d)
```

```python
import functools

import jax
import jax.numpy as jnp
from jax import lax
from jax.experimental import pallas as pl
from jax.experimental.pallas import tpu as pltpu

F32 = jnp.float32
BF16 = jnp.bfloat16

D_MODEL = 2048
DEPTH = 2
GRID_W = 64
HEAD_DIM = 128
CHUNK = 64
GLA_HEADS = 4
GLA_DK = 64
GLA_DV = 128
GLA_RANK = 16
GLA_GATE_NORM = 16.0
GDN_HEADS = 4
GDN_DK = 128
GDN_DV = 128
GDN_CONV = 5
ATTN_HEADS = 8
ATTN_KV_HEADS = 2
ATTN_GROUP = ATTN_HEADS // ATTN_KV_HEADS
ROPE_THETA = 10000.0
N_EXPERTS = 16
EC_FACTOR = 2
N_MOD = 6
DEEPNORM_ALPHA = (2 * DEPTH) ** 0.25

_SRC = dict(gla_q=(0, 256), gla_k=(256, 256), gla_v=(512, 512), gla_g=(1024, 512), gla_r=(1536, 32),
            gdn_q=(1568, 512), gdn_k=(2080, 512), gdn_v=(2592, 512), gdn_z=(3104, 512), gdn_b=(3616, 8),
            gdn_a=(3624, 8), att_q=(3632, 1024), att_k=(4656, 256), att_v=(4912, 256))
_MAIN_ORDER = ("att_q", "gla_v", "gdn_q", "gdn_k", "gdn_v", "gla_g", "gdn_z", "gla_q", "gla_k", "att_k", "att_v")
_SMALL_ORDER = ("gla_r", "gdn_b", "gdn_a")
_OFF = {}
_o = 0
for _n in _MAIN_ORDER:
    _OFF[_n] = _o
    _o += _SRC[_n][1]
MAIN_W = _o
_o = 0
for _n in _SMALL_ORDER:
    _OFF[_n] = _o
    _o += _SRC[_n][1]
SMALL_W = 128
MIX_HW = 512
SCAN_TB = 256

VMEM_LIMIT = 56 * 1024 * 1024

_NT = (((1,), (1,)), ((), ()))
_TN = (((0,), (0,)), ((), ()))


def _cparams(n_axes):
    return pltpu.CompilerParams(dimension_semantics=("arbitrary",) * n_axes, vmem_limit_bytes=VMEM_LIMIT)


def _split_bf16(a):
    hi = a.astype(BF16)
    lo = (a - hi.astype(F32)).astype(BF16)
    return hi, lo


def _dot(a, b):
    return jnp.dot(a, b, preferred_element_type=F32)


def _dot2(a, m):
    hi, lo = _split_bf16(a)
    return _dot(hi, m) + _dot(lo, m)


def _dot2_left(m, a):
    hi, lo = _split_bf16(a)
    return _dot(m, hi) + _dot(m, lo)


def _sigmoid(x):
    return 1.0 / (1.0 + jnp.exp(-x))


def _softplus(x):
    return jnp.maximum(x, 0.0) + jnp.log1p(jnp.exp(-jnp.abs(x)))


def _ada_kernel(c_ref, w_ref, b_ref, o_ref):
    c = c_ref[...]
    s = (c * _sigmoid(c)).astype(BF16)
    o_ref[...] = _dot(s, w_ref[...].astype(BF16)) + b_ref[...]


def _ada(cond8, w_ada, b_ada, tn=1024):
    depth, d, n = w_ada.shape
    return pl.pallas_call(
        _ada_kernel,
        out_shape=jax.ShapeDtypeStruct((depth, 8, n), F32),
        grid=(depth, n // tn),
        in_specs=[pl.BlockSpec((8, d), lambda l, j: (0, 0)),
                  pl.BlockSpec((None, d, tn), lambda l, j: (l, 0, j)),
                  pl.BlockSpec((None, 1, tn), lambda l, j: (l, 0, j))],
        out_specs=pl.BlockSpec((None, 8, tn), lambda l, j: (l, 0, j)),
        compiler_params=_cparams(2),
        name="ada",
    )(cond8, w_ada, b_ada.reshape(depth, 1, n))


def _inproj_kernel(x_ref, shift_ref, scale_ref, w_ref, wsh_ref, wsl_ref, o_ref, os_ref, h_ref):
    @pl.when(pl.program_id(1) == 0)
    def _():
        h = x_ref[...] * (1.0 + scale_ref[...]) + shift_ref[...]
        hi, lo = _split_bf16(h)
        h_ref[...] = hi
        wsh = wsh_ref[...]
        os_ref[...] = _dot(hi, wsh) + _dot(lo, wsh) + _dot(hi, wsl_ref[...])

    o_ref[...] = _dot(h_ref[...], w_ref[...]).astype(o_ref.dtype)


def _inproj(x2d, mods, row_of_tile, w_main, ws_hi, ws_lo, tm, tn=1024):
    rows, d = x2d.shape
    n = w_main.shape[1]
    return pl.pallas_call(
        _inproj_kernel,
        out_shape=(jax.ShapeDtypeStruct((rows, n), BF16), jax.ShapeDtypeStruct((rows, SMALL_W), F32)),
        grid=(rows // tm, n // tn),
        in_specs=[pl.BlockSpec((tm, d), lambda i, j: (i, 0)),
                  pl.BlockSpec((None, 1, d), lambda i, j: (row_of_tile(i) * N_MOD + 0, 0, 0)),
                  pl.BlockSpec((None, 1, d), lambda i, j: (row_of_tile(i) * N_MOD + 1, 0, 0)),
                  pl.BlockSpec((d, tn), lambda i, j: (0, j)),
                  pl.BlockSpec((d, SMALL_W), lambda i, j: (0, 0)),
                  pl.BlockSpec((d, SMALL_W), lambda i, j: (0, 0))],
        out_specs=(pl.BlockSpec((tm, tn), lambda i, j: (i, j)),
                   pl.BlockSpec((tm, SMALL_W), lambda i, j: (i, 0))),
        scratch_shapes=[pltpu.VMEM((tm, d), BF16)],
        compiler_params=_cparams(2),
        name="inproj",
    )(x2d, mods, mods, w_main, ws_hi, ws_lo)


def _rms_rope(x, gain, cos, sin):
    y = x * lax.rsqrt(jnp.mean(x * x, axis=-1, keepdims=True) + 1e-6) * gain
    if cos is not None:
        lane = lax.broadcasted_iota(jnp.int32, y.shape, 1)
        partner = jnp.where((lane % 64) < 32, pltpu.roll(y, 96, 1), pltpu.roll(y, 32, 1))
        y = y * cos + partner * sin
    return y


KV_W = ATTN_KV_HEADS * HEAD_DIM + ATTN_KV_HEADS * 2 * HEAD_DIM
LOG2E = 1.4426950408889634


def _qkv_prep_kernel(*refs, rope):
    if rope:
        q_ref, kv_ref, gain_ref, cos_ref, sin_ref, qo_ref, kvo_ref = refs
        cos, sin = cos_ref[...], sin_ref[...]
    else:
        q_ref, kv_ref, gain_ref, qo_ref, kvo_ref = refs
        cos = sin = None
    q = q_ref[...].astype(F32)
    q_scale = (HEAD_DIM ** -0.5) * LOG2E
    qo_ref[...] = jnp.concatenate(
        [(_rms_rope(q[:, g * HEAD_DIM:(g + 1) * HEAD_DIM], gain_ref[0:1, :], cos, sin) * q_scale).astype(BF16)
         for g in range(ATTN_HEADS)], axis=1)
    kv = kv_ref[...].astype(F32)
    outs = [_rms_rope(kv[:, h * HEAD_DIM:(h + 1) * HEAD_DIM], gain_ref[1:2, :], cos, sin).astype(BF16)
            for h in range(ATTN_KV_HEADS)]
    ones = jnp.ones((kv.shape[0], HEAD_DIM), BF16)
    for h in range(ATTN_KV_HEADS):
        outs += [kv[:, (ATTN_KV_HEADS + h) * HEAD_DIM:(ATTN_KV_HEADS + h + 1) * HEAD_DIM].astype(BF16), ones]
    kvo_ref[...] = jnp.concatenate(outs, axis=1)


def _qkv_prep(p, gain, rope_tabs, n_per_sample, tm):
    rows = p.shape[0]
    qw = ATTN_HEADS * HEAD_DIM
    kvw = 2 * ATTN_KV_HEADS * HEAD_DIM
    per = n_per_sample // tm
    in_specs = [pl.BlockSpec((tm, qw), lambda i: (i, _OFF["att_q"] // qw)),
                pl.BlockSpec((tm, kvw), lambda i: (i, _OFF["att_k"] // kvw)),
                pl.BlockSpec((2, HEAD_DIM), lambda i: (0, 0))]
    args = [p, p, gain]
    if rope_tabs is not None:
        in_specs += [pl.BlockSpec((tm, HEAD_DIM), lambda i: (i % per, 0))] * 2
        args += list(rope_tabs)
    return pl.pallas_call(
        functools.partial(_qkv_prep_kernel, rope=rope_tabs is not None),
        out_shape=(jax.ShapeDtypeStruct((rows, qw), BF16), jax.ShapeDtypeStruct((rows, KV_W), BF16)),
        grid=(rows // tm,),
        in_specs=in_specs,
        out_specs=(pl.BlockSpec((tm, qw), lambda i: (i, 0)), pl.BlockSpec((tm, KV_W), lambda i: (i, 0))),
        compiler_params=_cparams(1),
        name="qkv_prep",
    )(*args)


def _attn_kernel(*refs, n_seg):
    q_ref = refs[0]
    k_refs = refs[1:1 + 2 * n_seg:2]
    v_refs = refs[2:2 + 2 * n_seg:2]
    o_ref = refs[1 + 2 * n_seg]
    outs = []
    for g in range(ATTN_GROUP):
        qg = q_ref[:, g * HEAD_DIM:(g + 1) * HEAD_DIM]
        s = jnp.concatenate([lax.dot_general(qg, k_ref[...], _NT, preferred_element_type=F32) for k_ref in k_refs],
                            axis=1)
        p = jnp.exp2(s - jnp.max(s, axis=-1, keepdims=True)).astype(BF16)
        acc = None
        off = 0
        for v_ref in v_refs:
            n = v_ref.shape[0]
            part = _dot(p[:, off:off + n], v_ref[...])
            acc = part if acc is None else acc + part
            off += n
        outs.append((acc[:, :HEAD_DIM] / acc[:, HEAD_DIM:]).astype(o_ref.dtype))
    o_ref[...] = jnp.concatenate(outs, axis=1)


def _attention(qn, kv_segs, bsz, tq):
    rows = qn.shape[0]
    nq = rows // bsz // tq
    gw = ATTN_GROUP * HEAD_DIM
    in_specs = [pl.BlockSpec((tq, gw), lambda b, h, i: (b * nq + i, h))]
    args = [qn]
    for arr, n in kv_segs:
        in_specs.append(pl.BlockSpec((n, HEAD_DIM), lambda b, h, i: (b, h)))
        in_specs.append(pl.BlockSpec((n, 2 * HEAD_DIM), lambda b, h, i: (b, 1 + h)))
        args += [arr, arr]
    return pl.pallas_call(
        functools.partial(_attn_kernel, n_seg=len(kv_segs)),
        out_shape=jax.ShapeDtypeStruct((rows, ATTN_HEADS * HEAD_DIM), BF16),
        grid=(bsz, ATTN_KV_HEADS, nq),
        in_specs=in_specs,
        out_specs=pl.BlockSpec((tq, gw), lambda b, h, i: (b * nq + i, h)),
        compiler_params=_cparams(3),
        name="attention",
    )(*args)


def _chunk_masks(tb):
    i = lax.broadcasted_iota(jnp.int32, (tb, tb), 0)
    j = lax.broadcasted_iota(jnp.int32, (tb, tb), 1)
    same = (i // CHUNK) == (j // CHUNK)
    return i, j, same


def _gla_kernel(qk_f, v_f, s_f, qk_b, v_b, s_b, wup_ref, bup_ref, tri_ref, st0_ref,
                of_ref, ob_ref, stout_ref, st_scr):
    step = pl.program_id(1)

    @pl.when(step == 0)
    def _():
        st_scr[...] = st0_ref[...]

    tb = qk_f.shape[0]
    nch = tb // CHUNK
    hk = GLA_HEADS * GLA_DK
    lane = lax.broadcasted_iota(jnp.int32, (CHUNK, hk), 1)
    head_of_lane = lane // GLA_DK
    r4 = lax.broadcasted_iota(jnp.int32, (GLA_HEADS * CHUNK, CHUNK), 0) % CHUNK
    c4 = lax.broadcasted_iota(jnp.int32, (GLA_HEADS * CHUNK, CHUNK), 1)
    bd_r = lax.broadcasted_iota(jnp.int32, (MIX_HW, hk), 0) // GLA_DV
    bd_c = lax.broadcasted_iota(jnp.int32, (MIX_HW, hk), 1) // GLA_DK
    bd_mask = bd_r == bd_c

    dirs = []
    for z, (qk_ref, v_ref, s_ref) in enumerate(((qk_f, v_f, s_f), (qk_b, v_b, s_b))):
        qk = qk_ref[...].astype(F32)
        hi, lo = _split_bf16(s_ref[...])
        wh = wup_ref[z, 0]
        logit = _dot(hi, wh) + _dot(lo, wh) + _dot(hi, wup_ref[z, 1]) + bup_ref[z]
        log_a = -_softplus(-logit) / GLA_GATE_NORM
        dirs.append(dict(
            z=z, q=qk[:, :hk] * (GLA_DK ** -0.5), k=qk[:, hk:], v=v_ref[...].astype(BF16),
            cum=_dot2_left(tri_ref[z], log_a),
            mask=(c4 <= r4) if z == 0 else (c4 >= r4), st=st_scr[z], outs=[None] * nch))

    for c_step in range(nch):
        for dr in dirs:
            z = dr["z"]
            c = c_step if z == 0 else nch - 1 - c_step
            last = CHUNK - 1 if z == 0 else 0
            r0 = c * CHUNK
            cum_c = dr["cum"][r0:r0 + CHUNK]
            tot = cum_c[last:last + 1, :]
            q_dec = dr["q"][r0:r0 + CHUNK] * jnp.exp(cum_c)
            k_c = dr["k"][r0:r0 + CHUNK]
            k_inv = (k_c * jnp.exp(-cum_c)).astype(BF16)
            k_dec = (k_c * jnp.exp(tot - cum_c)).astype(BF16)
            q4 = jnp.concatenate([jnp.where(head_of_lane == h, q_dec, 0.0) for h in range(GLA_HEADS)],
                                 axis=0).astype(BF16)
            a4 = lax.dot_general(q4, k_inv, _NT, preferred_element_type=F32)
            a4 = jnp.where(dr["mask"], a4, 0.0).astype(BF16)
            v_c = dr["v"][r0:r0 + CHUNK]
            o4 = _dot(a4, v_c)
            o_intra = jnp.concatenate(
                [o4[h * CHUNK:(h + 1) * CHUNK, h * GLA_DV:(h + 1) * GLA_DV] for h in range(GLA_HEADS)], axis=1)
            o_inter = lax.dot_general(q_dec.astype(BF16), dr["st"].astype(BF16), _NT, preferred_element_type=F32)
            dr["outs"][c] = o_intra + o_inter
            kv_t = lax.dot_general(v_c, k_dec, _TN, preferred_element_type=F32)
            dr["st"] = dr["st"] * jnp.exp(tot) + jnp.where(bd_mask, kv_t, 0.0)

    for dr, o_ref in zip(dirs, (of_ref, ob_ref)):
        o_ref[...] = jnp.concatenate(dr["outs"], axis=0)
        st_scr[dr["z"]] = dr["st"]

    @pl.when(step == pl.num_programs(1) - 1)
    def _():
        stout_ref[...] = st_scr[...]


def _gla(p, ps, t, bsz, wup, bup, tri, st0):
    tb = SCAN_TB
    nblk = t // tb
    qk_col = _OFF["gla_q"] // (2 * GLA_HEADS * GLA_DK)
    v_col = _OFF["gla_v"] // MIX_HW
    fwd = lambda b, i: b * nblk + i
    bwd = lambda b, i: b * nblk + (nblk - 1 - i)
    hk = GLA_HEADS * GLA_DK
    return pl.pallas_call(
        _gla_kernel,
        out_shape=(jax.ShapeDtypeStruct((bsz * t, MIX_HW), F32), jax.ShapeDtypeStruct((bsz * t, MIX_HW), F32),
                   jax.ShapeDtypeStruct((bsz, 2, MIX_HW, hk), F32)),
        grid=(bsz, nblk),
        in_specs=[pl.BlockSpec((tb, 2 * hk), lambda b, i: (fwd(b, i), qk_col)),
                  pl.BlockSpec((tb, MIX_HW), lambda b, i: (fwd(b, i), v_col)),
                  pl.BlockSpec((tb, SMALL_W), lambda b, i: (fwd(b, i), 0)),
                  pl.BlockSpec((tb, 2 * hk), lambda b, i: (bwd(b, i), qk_col)),
                  pl.BlockSpec((tb, MIX_HW), lambda b, i: (bwd(b, i), v_col)),
                  pl.BlockSpec((tb, SMALL_W), lambda b, i: (bwd(b, i), 0)),
                  pl.BlockSpec((2, 2, SMALL_W, hk), lambda b, i: (0, 0, 0, 0)),
                  pl.BlockSpec((2, 1, hk), lambda b, i: (0, 0, 0)),
                  pl.BlockSpec((2, tb, tb), lambda b, i: (0, 0, 0)),
                  pl.BlockSpec((None, 2, MIX_HW, hk), lambda b, i: (b, 0, 0, 0))],
        out_specs=(pl.BlockSpec((tb, MIX_HW), lambda b, i: (fwd(b, i), 0)),
                   pl.BlockSpec((tb, MIX_HW), lambda b, i: (bwd(b, i), 0)),
                   pl.BlockSpec((None, 2, MIX_HW, hk), lambda b, i: (b, 0, 0, 0))),
        scratch_shapes=[pltpu.VMEM((2, MIX_HW, hk), F32)],
        compiler_params=_cparams(2),
        name="gla_scan",
    )(p, p, ps, p, p, ps, wup, bup, tri, st0)


def _gdn_prep_kernel(x_ref, prev_ref, next_ref, w_ref, o_ref):
    i = pl.program_id(1)
    tb = x_ref.shape[0]
    halo = prev_ref.shape[0]
    prev = jnp.where(i > 0, prev_ref[...].astype(F32), 0.0)
    nxt = jnp.where(i < pl.num_programs(1) - 1, next_ref[...].astype(F32), 0.0)
    ext = jnp.concatenate([prev, x_ref[...].astype(F32), nxt], axis=0)
    w = w_ref[...]
    acc = None
    for j in range(GDN_CONV):
        shift = GDN_CONV // 2 - j
        rolled = ext if shift == 0 else pltpu.roll(ext, shift % (tb + 2 * halo), 0)
        term = rolled[halo:halo + tb] * w[j:j + 1, :]
        acc = term if acc is None else acc + term
    y = acc * _sigmoid(acc)
    qk_w = 2 * GDN_HEADS * GDN_DK
    outs = []
    for h in range(2 * GDN_HEADS):
        yh = y[:, h * GDN_DK:(h + 1) * GDN_DK]
        yh = yh * lax.rsqrt(jnp.sum(yh * yh, axis=-1, keepdims=True) + 1e-6)
        if h < GDN_HEADS:
            yh = yh * (GDN_DK ** -0.5)
        outs.append(yh)
    outs.append(y[:, qk_w:])
    o_ref[...] = jnp.concatenate(outs, axis=1)


def _gdn_prep(p, t, bsz, conv_w):
    tb = SCAN_TB
    halo = 16
    nblk = t // tb
    width = conv_w.shape[1]
    col = _OFF["gdn_q"] // width
    n_halo_blocks = bsz * t // halo
    per = tb // halo
    return pl.pallas_call(
        _gdn_prep_kernel,
        out_shape=jax.ShapeDtypeStruct((bsz * t, width), F32),
        grid=(bsz, nblk),
        in_specs=[pl.BlockSpec((tb, width), lambda b, i: (b * nblk + i, col)),
                  pl.BlockSpec((halo, width), lambda b, i: (jnp.maximum((b * nblk + i) * per - 1, 0), col)),
                  pl.BlockSpec((halo, width),
                               lambda b, i: (jnp.minimum((b * nblk + i + 1) * per, n_halo_blocks - 1), col)),
                  pl.BlockSpec((GDN_CONV, width), lambda b, i: (0, 0))],
        out_specs=pl.BlockSpec((tb, width), lambda b, i: (b * nblk + i, 0)),
        compiler_params=_cparams(2),
        name="gdn_conv",
    )(p, p, p, conv_w)


def _gdn_kernel(x_f, s_f, x_b, s_b, sel_ref, ealog_ref, dtb_ref, tri_ref, s0_ref,
                of_ref, ob_ref, sout_ref, s_scr):
    step = pl.program_id(1)

    @pl.when(step == 0)
    def _():
        s_scr[...] = s0_ref[...]

    tb = x_f.shape[0]
    nch = tb // CHUNK
    i_idx, j_idx, same = _chunk_masks(tb)
    xor = i_idx ^ j_idx
    eye = jnp.where(i_idx == j_idx, 1.0, 0.0)
    hw = GDN_HEADS * GDN_DK

    chains = []
    for z, (x_ref, s_ref) in enumerate(((x_f, s_f), (x_b, s_b))):
        before = (j_idx <= i_idx) if z == 0 else (j_idx >= i_idx)
        incl = same & before
        strict = incl & (i_idx != j_idx)
        sm = s_ref[...]
        beta_all = _sigmoid(sm)
        g_all = -ealog_ref[...] * _softplus(sm + dtb_ref[...])
        b_sel = _dot2(beta_all, sel_ref[2 * z])
        g_sel = _dot2(g_all, sel_ref[2 * z + 1])
        c_col = _dot2_left(tri_ref[z], g_sel)
        x = x_ref[...]
        for h in range(GDN_HEADS):
            ch = dict(z=z, h=h, incl=incl)
            ch["q"] = x[:, h * GDN_DK:(h + 1) * GDN_DK]
            ch["k"] = x[:, hw + h * GDN_DK:hw + (h + 1) * GDN_DK]
            bc = b_sel[:, h * GDN_DK:(h + 1) * GDN_DK]
            cc = c_col[:, h * GDN_DK:(h + 1) * GDN_DK]
            ch["cc"] = cc
            kb = ch["k"] * bc
            ch["vb"] = (x[:, 2 * hw + h * GDN_DV:2 * hw + (h + 1) * GDN_DV] * bc).astype(BF16)
            ch["kbg"] = (kb * jnp.exp(cc)).astype(BF16)
            ch["k_bf"] = ch["k"].astype(BF16)
            c_row = cc.T[0:1, :]
            dm = jnp.concatenate([cc] * (tb // GDN_DK), axis=1) - c_row
            ch["gamma"] = jnp.where(incl, jnp.exp(jnp.where(incl, dm, 0.0)), 0.0)
            kk = lax.dot_general(kb.astype(BF16), ch["k_bf"], _NT, preferred_element_type=F32)
            ch["l_mat"] = jnp.where(strict, kk * ch["gamma"], 0.0)
            ch["t_inv"] = eye - jnp.where(xor == 1, ch["l_mat"], 0.0)
            chains.append(ch)

    s = 2
    while s < CHUNK:
        lvl = (xor >= s) & (xor < 2 * s)
        for ch in chains:
            ch["t_bf"] = ch["t_inv"].astype(BF16)
            ch["m1"] = _dot(ch["t_bf"], jnp.where(lvl, ch["l_mat"], 0.0).astype(BF16)).astype(BF16)
        for ch in chains:
            ch["t_inv"] = ch["t_inv"] - _dot(ch["m1"], ch["t_bf"])
        s *= 2

    for ch in chains:
        t_bf = ch["t_inv"].astype(BF16)
        ch["u"] = _dot(t_bf, ch["vb"])
        ch["w"] = _dot(t_bf, ch["kbg"]).astype(BF16)
        qk = lax.dot_general(ch["q"].astype(BF16), ch["k_bf"], _NT, preferred_element_type=F32)
        ch["a_int"] = jnp.where(ch["incl"], qk * ch["gamma"], 0.0).astype(BF16)
        ch["state"] = s_scr[ch["z"], ch["h"]]
        ch["outs"] = [None] * nch

    zeros = jnp.zeros((CHUNK, GDN_DV), BF16)
    for c_step in range(nch):
        for ch in chains:
            c = c_step if ch["z"] == 0 else nch - 1 - c_step
            last = CHUNK - 1 if ch["z"] == 0 else 0
            r0 = c * CHUNK
            cc_c = ch["cc"][r0:r0 + CHUNK]
            g_last = cc_c[last:last + 1, :]
            k_dec = (ch["k"][r0:r0 + CHUNK] * jnp.exp(g_last - cc_c)).astype(BF16)
            q_dec = (ch["q"][r0:r0 + CHUNK] * jnp.exp(cc_c)).astype(BF16)
            s_bf = ch["state"].astype(BF16)
            v_new = ch["u"][r0:r0 + CHUNK] - _dot(ch["w"][r0:r0 + CHUNK], s_bf)
            v_new_bf = v_new.astype(BF16)
            v_pad = jnp.concatenate([v_new_bf if cc_i == c else zeros for cc_i in range(nch)], axis=0)
            ch["outs"][c] = _dot(q_dec, s_bf) + _dot(ch["a_int"][r0:r0 + CHUNK], v_pad)
            ch["state"] = (ch["state"] * jnp.exp(g_last)
                           + lax.dot_general(k_dec, v_new_bf, _TN, preferred_element_type=F32))

    for z, o_ref in enumerate((of_ref, ob_ref)):
        mine = [ch for ch in chains if ch["z"] == z]
        o_ref[...] = jnp.concatenate([jnp.concatenate(ch["outs"], axis=0) for ch in mine], axis=1)
        for ch in mine:
            s_scr[z, ch["h"]] = ch["state"]

    @pl.when(step == pl.num_programs(1) - 1)
    def _():
        sout_ref[...] = s_scr[...]


def _gdn(xc, ps, t, bsz, sel, ealog, dtb, tri, s0):
    tb = SCAN_TB
    nblk = t // tb
    width = xc.shape[1]
    fwd = lambda b, i: b * nblk + i
    bwd = lambda b, i: b * nblk + (nblk - 1 - i)
    return pl.pallas_call(
        _gdn_kernel,
        out_shape=(jax.ShapeDtypeStruct((bsz * t, MIX_HW), F32), jax.ShapeDtypeStruct((bsz * t, MIX_HW), F32),
                   jax.ShapeDtypeStruct((bsz, 2, GDN_HEADS, GDN_DK, GDN_DV), F32)),
        grid=(bsz, nblk),
        in_specs=[pl.BlockSpec((tb, width), lambda b, i: (fwd(b, i), 0)),
                  pl.BlockSpec((tb, SMALL_W), lambda b, i: (fwd(b, i), 0)),
                  pl.BlockSpec((tb, width), lambda b, i: (bwd(b, i), 0)),
                  pl.BlockSpec((tb, SMALL_W), lambda b, i: (bwd(b, i), 0)),
                  pl.BlockSpec((4, SMALL_W, MIX_HW), lambda b, i: (0, 0, 0)),
                  pl.BlockSpec((1, SMALL_W), lambda b, i: (0, 0)),
                  pl.BlockSpec((1, SMALL_W), lambda b, i: (0, 0)),
                  pl.BlockSpec((2, tb, tb), lambda b, i: (0, 0, 0)),
                  pl.BlockSpec((None, 2, GDN_HEADS, GDN_DK, GDN_DV), lambda b, i: (b, 0, 0, 0, 0))],
        out_specs=(pl.BlockSpec((tb, MIX_HW), lambda b, i: (fwd(b, i), 0)),
                   pl.BlockSpec((tb, MIX_HW), lambda b, i: (bwd(b, i), 0)),
                   pl.BlockSpec((None, 2, GDN_HEADS, GDN_DK, GDN_DV), lambda b, i: (b, 0, 0, 0, 0))),
        scratch_shapes=[pltpu.VMEM((2, GDN_HEADS, GDN_DK, GDN_DV), F32)],
        compiler_params=_cparams(2),
        name="gdn_scan",
    )(xc, ps, xc, ps, sel, ealog, dtb, tri, s0)


def _layer_norm_rows(z, gain, bias):
    mu = jnp.mean(z, axis=-1, keepdims=True)
    zc = z - mu
    var = jnp.mean(zc * zc, axis=-1, keepdims=True)
    return zc * lax.rsqrt(var + 1e-5) * gain + bias


def _mixer_finish(o, gate, gain, n_heads, dv):
    outs = []
    for h in range(n_heads):
        oh = o[:, h * dv:(h + 1) * dv]
        oh = oh * lax.rsqrt(jnp.mean(oh * oh, axis=-1, keepdims=True) + 1e-6) * gain
        gh = gate[:, h * dv:(h + 1) * dv]
        outs.append((oh * (gh * _sigmoid(gh))).astype(BF16))
    return outs


def _outproj_kernel(glaf_ref, glab_ref, gdnf_ref, gdnb_ref, g_ref, z_ref, att_ref, x_ref, gate_ref, w_ref,
                    ngla_ref, ngdn_ref, gain_ref, bias_ref, o_ref):
    parts = _mixer_finish(glaf_ref[...] + glab_ref[...], g_ref[...].astype(F32), ngla_ref[...], GLA_HEADS, GLA_DV)
    parts += _mixer_finish(gdnf_ref[...] + gdnb_ref[...], z_ref[...].astype(F32), ngdn_ref[...], GDN_HEADS,
                           GDN_DV)
    parts.append(att_ref[...])
    y = _dot(jnp.concatenate(parts, axis=1), w_ref[...])
    z = DEEPNORM_ALPHA * x_ref[...] + gate_ref[...] * y
    o_ref[...] = _layer_norm_rows(z, gain_ref[...], bias_ref[...])


def _outproj(gla_f, gla_b, gdn_f, gdn_b, p, att, x2d, mods, row_of_tile, w_out, n_gla, n_gdn, gain, bias, tm):
    rows, d = x2d.shape
    g_col = _OFF["gla_g"] // MIX_HW
    z_col = _OFF["gdn_z"] // MIX_HW
    mix = lambda: pl.BlockSpec((tm, MIX_HW), lambda i: (i, 0))
    vec = lambda n: pl.BlockSpec((1, n), lambda i: (0, 0))
    return pl.pallas_call(
        _outproj_kernel,
        out_shape=jax.ShapeDtypeStruct((rows, d), F32),
        grid=(rows // tm,),
        in_specs=[mix(), mix(), mix(), mix(),
                  pl.BlockSpec((tm, MIX_HW), lambda i: (i, g_col)),
                  pl.BlockSpec((tm, MIX_HW), lambda i: (i, z_col)),
                  pl.BlockSpec((tm, att.shape[1]), lambda i: (i, 0)),
                  pl.BlockSpec((tm, d), lambda i: (i, 0)),
                  pl.BlockSpec((None, 1, d), lambda i: (row_of_tile(i) * N_MOD + 2, 0, 0)),
                  pl.BlockSpec((w_out.shape[0], d), lambda i: (0, 0)),
                  vec(GLA_DV), vec(GDN_DV), vec(d), vec(d)],
        out_specs=pl.BlockSpec((tm, d), lambda i: (i, 0)),
        compiler_params=_cparams(1),
        name="outproj_ln",
    )(gla_f, gla_b, gdn_f, gdn_b, p, p, att, x2d, mods, w_out, n_gla.reshape(1, -1), n_gdn.reshape(1, -1),
      gain.reshape(1, d), bias.reshape(1, d))


def _router_kernel(x_ref, shift_ref, scale_ref, rh_ref, rl_ref, h_ref, lg_ref):
    h = x_ref[...] * (1.0 + scale_ref[...]) + shift_ref[...]
    hi, lo = _split_bf16(h)
    h_ref[...] = h
    rh = rh_ref[...]
    lg_ref[...] = _dot(hi, rh) + _dot(lo, rh) + _dot(hi, rl_ref[...])


def _router(x2d, mods, row_of_tile, r_hi, r_lo, tm):
    rows, d = x2d.shape
    return pl.pallas_call(
        _router_kernel,
        out_shape=(jax.ShapeDtypeStruct((rows, d), F32), jax.ShapeDtypeStruct((rows, 128), F32)),
        grid=(rows // tm,),
        in_specs=[pl.BlockSpec((tm, d), lambda i: (i, 0)),
                  pl.BlockSpec((None, 1, d), lambda i: (row_of_tile(i) * N_MOD + 3, 0, 0)),
                  pl.BlockSpec((None, 1, d), lambda i: (row_of_tile(i) * N_MOD + 4, 0, 0)),
                  pl.BlockSpec((d, 128), lambda i: (0, 0)),
                  pl.BlockSpec((d, 128), lambda i: (0, 0))],
        out_specs=(pl.BlockSpec((tm, d), lambda i: (i, 0)), pl.BlockSpec((tm, 128), lambda i: (i, 0))),
        compiler_params=_cparams(1),
        name="router",
    )(x2d, mods, mods, r_hi, r_lo)


def _row_copy(h_hbm, land, sem, src_row, dst_row):
    return pltpu.make_async_copy(h_hbm.at[pl.ds(src_row, 1), :], land.at[pl.ds(dst_row, 1), :], sem)


def _ffn_kernel(idx_ref, h_hbm, g_ref, w1_ref, w3_ref, w2_ref, o_ref, land, x_scr, h_scr, sem, *, nf):
    e = pl.program_id(0)
    s = pl.program_id(1)
    n_e = pl.num_programs(0)
    tf = w1_ref.shape[1]
    n_rows = land.shape[0]
    per = n_rows // nf

    def wait_all_rows():
        pltpu.make_async_copy(h_hbm.at[pl.ds(0, n_rows), :], land, sem).wait()

    @pl.when(s == 0)
    def _():
        @pl.when(e == 0)
        def _():
            def issue(r, carry):
                _row_copy(h_hbm, land, sem, idx_ref[0, r], r).start()
                return carry
            lax.fori_loop(0, n_rows, issue, 0)

        wait_all_rows()
        x_scr[...] = land[...].astype(BF16)

    @pl.when(s < nf)
    def _():
        nxt = lax.rem(e + 1, n_e)
        base = s * per
        for i in range(per):
            _row_copy(h_hbm, land, sem, idx_ref[nxt, base + i], base + i).start()
        x = x_scr[...]
        a = _dot(x, w1_ref[...].astype(BF16))
        u = _dot(x, w3_ref[...].astype(BF16))
        hmid = (a * _sigmoid(a) * u).astype(BF16)
        for f in range(nf):
            @pl.when(s == f)
            def _():
                h_scr[:, f * tf:(f + 1) * tf] = hmid

    @pl.when(s >= nf)
    def _():
        o_ref[...] = _dot(h_scr[...], w2_ref[...].astype(BF16)) * g_ref[...]

    @pl.when((e == n_e - 1) & (s == pl.num_programs(1) - 1))
    def _():
        wait_all_rows()


def _ffn(h_all, rows_idx, gates, w1, w3, w2, layer, tf=512, td=256):
    n_e, r = rows_idx.shape
    d = h_all.shape[1]
    ff = w1.shape[3]
    nf = ff // tf
    assert r % nf == 0
    up = lambda e, s, idx: (layer, e, 0, jnp.minimum(s, nf - 1))
    down = lambda e, s, idx: (layer, e, 0, jnp.maximum(s - nf, 0))
    grid_spec = pltpu.PrefetchScalarGridSpec(
        num_scalar_prefetch=1,
        grid=(n_e, nf + d // td),
        in_specs=[pl.BlockSpec(memory_space=pl.ANY),
                  pl.BlockSpec((None, r, 1), lambda e, s, idx: (e, 0, 0)),
                  pl.BlockSpec((None, None, d, tf), up),
                  pl.BlockSpec((None, None, d, tf), up),
                  pl.BlockSpec((None, None, ff, td), down)],
        out_specs=pl.BlockSpec((None, r, td), lambda e, s, idx: (e, 0, jnp.maximum(s - nf, 0))),
        scratch_shapes=[pltpu.VMEM((r, d), F32), pltpu.VMEM((r, d), BF16), pltpu.VMEM((r, ff), BF16),
                        pltpu.SemaphoreType.DMA(())],
    )
    return pl.pallas_call(
        functools.partial(_ffn_kernel, nf=nf),
        out_shape=jax.ShapeDtypeStruct((n_e, r, d), F32),
        grid_spec=grid_spec,
        compiler_params=_cparams(2),
        name="expert_ffn",
    )(rows_idx, h_all, gates, w1, w3, w2)


COMB_TT = 256
COMB_W = 64


def _combine_kernel(start_ref, y_hbm, pos_ref, x_ref, gate_ref, gain_ref, bias_ref, o_ref,
                    land, extra, acc_scr, sem, sem_x, *, tile0):
    i = pl.program_id(0)
    n_steps = pl.num_programs(0)
    n_e, tt = pos_ref.shape
    w = COMB_W
    d = x_ref.shape[1]
    r_total = y_hbm.shape[1]
    tile = tile0 + i

    adv = w - 8

    def base_of(t, e, k):
        aligned = lax.shift_right_logical(start_ref[t, e] + k * adv, 3) * 8
        return pl.multiple_of(jnp.minimum(aligned, r_total - w), 8)

    def window_copy(t, e, slot):
        return pltpu.make_async_copy(y_hbm.at[e, pl.ds(base_of(t, e, 0), w), :],
                                     land.at[slot, pl.ds(e * w, w), :], sem.at[slot])

    slot = lax.rem(i, 2)

    @pl.when(i == 0)
    def _():
        for e in range(n_e):
            window_copy(tile, e, 0).start()

    @pl.when(i + 1 < n_steps)
    def _():
        for e in range(n_e):
            window_copy(tile + 1, e, 1 - slot).start()

    for e in range(n_e):
        window_copy(tile, e, slot).wait()

    row_sel = lax.broadcasted_iota(jnp.int32, (w, tt), 0)
    row_val = lax.broadcasted_iota(jnp.int32, (w, d), 0)

    def select(e, base, lo, hi):
        pe = pos_ref[e:e + 1, :]
        hit = (pe - base == row_sel) & (pe >= lo) & (pe < hi)
        return jnp.where(hit, 1.0, 0.0).astype(BF16)

    def masked(rows, base, lo, hi):
        yrow = row_val + base
        return jnp.where((yrow >= lo) & (yrow < hi), rows, 0.0).astype(BF16)

    sels, vals = [], []
    for e in range(n_e):
        st, en = start_ref[tile, e], start_ref[tile + 1, e]
        base = base_of(tile, e, 0)
        hi = jnp.minimum(st + adv, en)
        sels.append(select(e, base, st, hi))
        vals.append(masked(land[slot, e * w:(e + 1) * w, :], base, st, hi))
    acc_scr[...] = lax.dot_general(jnp.concatenate(sels, axis=0), jnp.concatenate(vals, axis=0), _TN,
                                   preferred_element_type=F32)

    for e in range(n_e):
        st, en = start_ref[tile, e], start_ref[tile + 1, e]
        n_win = (en - st + adv - 1) // adv

        def more(k, carry, e=e, st=st, en=en):
            base = base_of(tile, e, k)
            cp = pltpu.make_async_copy(y_hbm.at[e, pl.ds(base, w), :], extra, sem_x)
            cp.start()
            cp.wait()
            lo = st + k * adv
            hi = jnp.minimum(lo + adv, en)
            acc_scr[...] += lax.dot_general(select(e, base, lo, hi), masked(extra[...], base, lo, hi), _TN,
                                            preferred_element_type=F32)
            return carry
        lax.fori_loop(1, n_win, more, 0)

    z = DEEPNORM_ALPHA * x_ref[...] + gate_ref[...] * acc_scr[...]
    o_ref[...] = _layer_norm_rows(z, gain_ref[...], bias_ref[...])


def _combine(x2d, y, pos_all, starts, tile0, mods, row_of_tile, gain, bias):
    rows, d = x2d.shape
    n_e = y.shape[0]
    tt = COMB_TT
    grid_spec = pltpu.PrefetchScalarGridSpec(
        num_scalar_prefetch=1,
        grid=(rows // tt,),
        in_specs=[pl.BlockSpec(memory_space=pl.ANY),
                  pl.BlockSpec((n_e, tt), lambda i, st: (0, tile0 + i)),
                  pl.BlockSpec((tt, d), lambda i, st: (i, 0)),
                  pl.BlockSpec((None, 1, d), lambda i, st: (row_of_tile(i) * N_MOD + 5, 0, 0)),
                  pl.BlockSpec((1, d), lambda i, st: (0, 0)),
                  pl.BlockSpec((1, d), lambda i, st: (0, 0))],
        out_specs=pl.BlockSpec((tt, d), lambda i, st: (i, 0)),
        scratch_shapes=[pltpu.VMEM((2, n_e * COMB_W, d), F32), pltpu.VMEM((COMB_W, d), F32),
                        pltpu.VMEM((tt, d), F32), pltpu.SemaphoreType.DMA((2,)), pltpu.SemaphoreType.DMA(())],
    )
    return pl.pallas_call(
        functools.partial(_combine_kernel, tile0=tile0),
        out_shape=jax.ShapeDtypeStruct((rows, d), F32),
        grid_spec=grid_spec,
        compiler_params=_cparams(1),
        name="combine_ln",
    )(starts, y, pos_all, x2d, mods, gain.reshape(1, d), bias.reshape(1, d))


def _rope_tables(n_tokens):
    rows = n_tokens // GRID_W
    row = jnp.broadcast_to(jnp.arange(rows)[:, None], (rows, GRID_W)).reshape(-1).astype(F32)
    col = jnp.broadcast_to(jnp.arange(GRID_W)[None, :], (rows, GRID_W)).reshape(-1).astype(F32)
    half = HEAD_DIM // 2
    inv = ROPE_THETA ** (-jnp.arange(0, half, 2, dtype=F32) / half)
    a_row = row[:, None] * inv
    a_col = col[:, None] * inv
    cos_t = jnp.concatenate([jnp.cos(a_row), jnp.cos(a_row), jnp.cos(a_col), jnp.cos(a_col)], axis=-1)
    sin_t = jnp.concatenate([-jnp.sin(a_row), jnp.sin(a_row), -jnp.sin(a_col), jnp.sin(a_col)], axis=-1)
    return cos_t, sin_t


def _permute_w_in(w):
    main = jnp.concatenate([w[:, _SRC[n][0]:_SRC[n][0] + _SRC[n][1]] for n in _MAIN_ORDER], axis=1)
    small = jnp.concatenate([w[:, _SRC[n][0]:_SRC[n][0] + _SRC[n][1]] for n in _SMALL_ORDER], axis=1)
    small = jnp.pad(small, ((0, 0), (0, SMALL_W - small.shape[1])))
    return main, small


def _scan_tri(tb):
    i = jnp.arange(tb)[:, None]
    j = jnp.arange(tb)[None, :]
    same = (i // CHUNK) == (j // CHUNK)
    return jnp.stack([same & (j <= i), same & (j >= i)]).astype(BF16)


def _gla_params(w_up, b_up):
    hk = GLA_HEADS * GLA_DK
    w = jnp.zeros((2, SMALL_W, hk), F32)
    for z in range(2):
        r0 = _OFF["gla_r"] + z * GLA_RANK
        w = w.at[z, r0:r0 + GLA_RANK].set(w_up[z])
    hi, lo = _split_bf16(w)
    return jnp.stack([hi, lo], axis=1), b_up.reshape(2, 1, hk)


def _gdn_params(a_log, dt_bias):
    sel = jnp.zeros((4, SMALL_W, MIX_HW), F32)
    ealog = jnp.zeros((1, SMALL_W), F32)
    dtb = jnp.zeros((1, SMALL_W), F32)
    for z in range(2):
        for h in range(GDN_HEADS):
            lane_b = _OFF["gdn_b"] + z * GDN_HEADS + h
            lane_a = _OFF["gdn_a"] + z * GDN_HEADS + h
            sel = sel.at[2 * z, lane_b, h * GDN_DK:(h + 1) * GDN_DK].set(1.0)
            sel = sel.at[2 * z + 1, lane_a, h * GDN_DK:(h + 1) * GDN_DK].set(1.0)
            ealog = ealog.at[0, lane_a].set(jnp.exp(a_log[z, h]))
            dtb = dtb.at[0, lane_a].set(dt_bias[z, h])
    return sel.astype(BF16), ealog, dtb


def _moe(x2d_sets, mods, row_fns, r_hi, r_lo, w1, w3, w2, layer, gain, bias, bsz, tms):
    hs, gates, flat_idx, row0s = [], [], [], []
    row0 = 0
    for x2d, row_fn, tm in zip(x2d_sets, row_fns, tms):
        h, lg = _router(x2d, mods, row_fn, r_hi, r_lo, tm)
        t = x2d.shape[0] // bsz
        cap = EC_FACTOR * t // N_EXPERTS
        aff = jax.nn.softmax(lg[:, :N_EXPERTS].reshape(bsz, t, N_EXPERTS), axis=-1).transpose(0, 2, 1)
        gate, idx = lax.top_k(aff, cap)
        idx, gate = lax.sort((idx, gate), dimension=2, num_keys=1)
        idx = idx + (row0 + jnp.arange(bsz) * t)[:, None, None]
        hs.append(h)
        gates.append(gate.transpose(1, 0, 2).reshape(N_EXPERTS, bsz * cap))
        flat_idx.append(idx.transpose(1, 0, 2).reshape(N_EXPERTS, bsz * cap))
        row0s.append(row0)
        row0 += x2d.shape[0]
    h_all = jnp.concatenate(hs, axis=0) if len(hs) > 1 else hs[0]
    fi = jnp.concatenate(flat_idx, axis=1) if len(hs) > 1 else flat_idx[0]
    gs = jnp.concatenate(gates, axis=1) if len(hs) > 1 else gates[0]
    fi = fi.astype(jnp.int32)
    y = _ffn(h_all, fi, gs[..., None], w1, w3, w2, layer)
    n_rows = fi.shape[1]
    pos_all = jnp.full((N_EXPERTS, row0), -1, jnp.int32).at[jnp.arange(N_EXPERTS)[:, None], fi].set(
        jnp.broadcast_to(jnp.arange(n_rows, dtype=jnp.int32), fi.shape))
    tile_edges = jnp.arange(row0 // COMB_TT + 1, dtype=jnp.int32) * COMB_TT
    starts = jnp.sum(fi[None, :, :] < tile_edges[:, None, None], axis=2, dtype=jnp.int32)
    return [_combine(x2d, y, pos_all, starts, r0 // COMB_TT, mods, row_fn, gain, bias)
            for x2d, row_fn, r0 in zip(x2d_sets, row_fns, row0s)]


def kernel(x, c, ctx, c_ctx, w_ada, b_ada, w_in, w_out, gla_w_up, gla_b_up, gla_norm, gdn_conv, gdn_a_log,
           gdn_dt_bias, gdn_norm, attn_qk_norm, ln_gain, ln_bias, router, w1, w3, w2):
    bsz, n_lat, d = x.shape
    n_ctx = ctx.shape[1]
    depth = w_ada.shape[0]
    cos_t, sin_t = _rope_tables(n_lat)
    tri = _scan_tri(SCAN_TB)

    cond8 = jnp.zeros((8, d), F32).at[:bsz].set(c).at[bsz].set(c_ctx)
    mods_all = _ada(cond8, w_ada, b_ada)

    tm_lat, tm_ctx = 1024, n_ctx
    lat_row = lambda i: i // (n_lat // tm_lat)
    ctx_row = lambda i: bsz + 0 * i
    tm_lat_ln = 256
    lat_row_ln = lambda i: i // (n_lat // tm_lat_ln)

    x_lat = x.reshape(bsz * n_lat, d)
    x_ctx = ctx.reshape(bsz * n_ctx, d)
    for l in range(depth):
        last = l == depth - 1
        mods = mods_all[l].reshape(8 * N_MOD, 1, d)
        w_main, w_small = _permute_w_in(w_in[l])
        w_main = w_main.astype(BF16)
        ws_hi, ws_lo = _split_bf16(w_small)

        p_lat, ps_lat = _inproj(x_lat, mods, lat_row, w_main, ws_hi, ws_lo, tm_lat, tn=512)
        p_ctx, ps_ctx = _inproj(x_ctx, mods, ctx_row, w_main, ws_hi, ws_lo, tm_ctx)

        wup, bup = _gla_params(gla_w_up[l], gla_b_up[l])
        gla_s0 = jnp.zeros((bsz, 2, MIX_HW, GLA_HEADS * GLA_DK), F32)
        gla_cf, gla_cb, gla_s = _gla(p_ctx, ps_ctx, n_ctx, bsz, wup, bup, tri, gla_s0)
        gla_lf, gla_lb, _ = _gla(p_lat, ps_lat, n_lat, bsz, wup, bup, tri, gla_s)

        sel, ealog, dtb = _gdn_params(gdn_a_log[l], gdn_dt_bias[l])
        gdn_s0 = jnp.zeros((bsz, 2, GDN_HEADS, GDN_DK, GDN_DV), F32)
        xc_ctx = _gdn_prep(p_ctx, n_ctx, bsz, gdn_conv[l])
        xc_lat = _gdn_prep(p_lat, n_lat, bsz, gdn_conv[l])
        gdn_cf, gdn_cb, gdn_s = _gdn(xc_ctx, ps_ctx, n_ctx, bsz, sel, ealog, dtb, tri, gdn_s0)
        gdn_lf, gdn_lb, _ = _gdn(xc_lat, ps_lat, n_lat, bsz, sel, ealog, dtb, tri, gdn_s)

        qn_lat, kv_lat = _qkv_prep(p_lat, attn_qk_norm[l], (cos_t, sin_t), n_lat, 512)
        qn_ctx, kv_ctx = _qkv_prep(p_ctx, attn_qk_norm[l], None, n_ctx, n_ctx)
        att_l = _attention(qn_lat, [(kv_ctx, n_ctx), (kv_lat, n_lat)], bsz, 256)
        w_out_b = w_out[l].astype(BF16)
        x_lat = _outproj(gla_lf, gla_lb, gdn_lf, gdn_lb, p_lat, att_l, x_lat, mods, lat_row_ln, w_out_b,
                         gla_norm[l], gdn_norm[l], ln_gain[l, 0], ln_bias[l, 0], tm_lat_ln)
        r_hi, r_lo = _split_bf16(jnp.pad(router[l], ((0, 0), (0, 128 - N_EXPERTS))))
        if not last:
            att_c = _attention(qn_ctx, [(kv_ctx, n_ctx)], bsz, n_ctx)
            x_ctx = _outproj(gla_cf, gla_cb, gdn_cf, gdn_cb, p_ctx, att_c, x_ctx, mods, ctx_row, w_out_b,
                             gla_norm[l], gdn_norm[l], ln_gain[l, 0], ln_bias[l, 0], n_ctx)
            x_lat, x_ctx = _moe([x_lat, x_ctx], mods, [lat_row_ln, ctx_row], r_hi, r_lo, w1, w3, w2, l,
                                ln_gain[l, 1], ln_bias[l, 1], bsz, [tm_lat_ln, n_ctx])
        else:
            (x_lat,) = _moe([x_lat], mods, [lat_row_ln], r_hi, r_lo, w1, w3, w2, l,
                            ln_gain[l, 1], ln_bias[l, 1], bsz, [tm_lat_ln])
    return x_lat.reshape(bsz, n_lat, d)
```
